```python
import jax, jax.numpy as jnp
from jax import lax
import numpy as np

D_MODEL = 1024
BATCH = 8
SEQ = 16384
DEPTH = 4

BLOCK = 128
EPS = 1e-6
NEG = -1e30
GM_GROUPS = 4
GM_GROUP_DIM = 128
GM_WIDTH = GM_GROUPS * GM_GROUP_DIM
HEAD_DIM = 64
SW_HEADS = 8
SW_KV_HEADS = 2
SW_WIDTH = SW_HEADS * HEAD_DIM
SW_KV_WIDTH = SW_KV_HEADS * HEAD_DIM
WINDOW = 128
ROPE_THETA = 10000.0
SB_HEADS = 4
SB_HEAD_DIM = 128
SB_WIDTH = SB_HEADS * SB_HEAD_DIM
N_BRANCH = 3
IN_SPLITS = (
    GM_WIDTH, GM_WIDTH, GM_WIDTH,
    SW_WIDTH, SW_KV_WIDTH, SW_KV_WIDTH, SW_WIDTH,
    SB_WIDTH, SB_WIDTH, SB_WIDTH, SB_WIDTH,
    N_BRANCH * D_MODEL,
)
IN_WIDTH = sum(IN_SPLITS)

kernel_name = "hybrid_gmlp_swa_sink_stickbreak_block"


def rms_norm(x, g):
    xf = x.astype(jnp.float32)
    y = xf * lax.rsqrt(jnp.mean(xf * xf, axis=-1, keepdims=True) + EPS)
    return (y * g.astype(jnp.float32)).astype(x.dtype)


def rope(x, pos):
    half = HEAD_DIM // 2
    freqs = ROPE_THETA ** (-jnp.arange(half, dtype=jnp.float32) / half)
    ang = pos.astype(jnp.float32)[:, None] * freqs[None, :]
    cos = jnp.cos(ang)[None, :, None, :]
    sin = jnp.sin(ang)[None, :, None, :]
    xf = x.astype(jnp.float32)
    x1, x2 = xf[..., :half], xf[..., half:]
    out = jnp.concatenate([x1 * cos - x2 * sin, x2 * cos + x1 * sin], axis=-1)
    return out.astype(x.dtype)


def chunk_gmlp(u, v, w_s, b_s, g_v):
    B, S, _ = v.shape
    n = S // BLOCK
    vf = v.astype(jnp.float32)
    mu = jnp.mean(vf, axis=-1, keepdims=True)
    var = jnp.mean(jnp.square(vf - mu), axis=-1, keepdims=True)
    vn = ((vf - mu) * lax.rsqrt(var + EPS) * g_v.astype(jnp.float32)).astype(v.dtype)
    vn = vn.reshape(B, n, BLOCK, GM_GROUPS, GM_GROUP_DIM)
    causal = jnp.tril(jnp.ones((BLOCK, BLOCK), dtype=bool))
    w = jnp.where(causal[None], w_s, jnp.zeros_like(w_s))
    mixed = jnp.einsum('gts,bnsgc->bntgc', w, vn)
    mixed = mixed + b_s.T[None, None, :, :, None]
    return u * mixed.reshape(B, S, GM_WIDTH)


def sliding_window_attention(q, k, v, sinks):
    B, S, H, Dh = q.shape
    n = S // BLOCK
    rep = H // SW_KV_HEADS
    qb = q.reshape(B, n, BLOCK, SW_KV_HEADS, rep, Dh)

    def band(t):
        tb = t.reshape(B, n, BLOCK, SW_KV_HEADS, Dh)
        prev = jnp.pad(tb[:, :-1], ((0, 0), (1, 0), (0, 0), (0, 0), (0, 0)))
        return jnp.concatenate([prev, tb], axis=2)

    kb, vb = band(k), band(v)
    s = jnp.einsum('bnqgrd,bnkgd->bngrqk', qb, kb).astype(jnp.float32) * (Dh ** -0.5)
    qi = jnp.arange(BLOCK)[:, None]
    kj = jnp.arange(2 * BLOCK)[None, :]
    diff = qi + BLOCK - kj
    local = (diff >= 0) & (diff < WINDOW)
    blk = jnp.arange(n)[:, None, None]
    valid = local[None] & ((blk > 0) | (kj >= BLOCK)[None])
    s = jnp.where(valid[None, :, None, None], s, NEG)
    sink = jnp.broadcast_to(
        sinks.astype(jnp.float32).reshape(1, 1, SW_KV_HEADS, rep, 1, 1), s.shape[:-1] + (1,))
    p = jax.nn.softmax(jnp.concatenate([s, sink], axis=-1), axis=-1)[..., :-1]
    o = jnp.einsum('bngrqk,bnkgd->bnqgrd', p.astype(v.dtype), vb)
    return o.reshape(B, S, H * Dh)


def stick_breaking_attention(q, k, v):
    B, S, H, Dh = q.shape
    n = S // BLOCK
    qt = q.transpose(0, 2, 1, 3) * (Dh ** -0.5)
    kt = k.transpose(0, 2, 1, 3)
    vt = v.transpose(0, 2, 1, 3)
    qoff = jnp.arange(BLOCK)
    outs = []
    for i in range(n):
        kl = (i + 1) * BLOCK
        z = jnp.einsum('bhqd,bhkd->bhqk', qt[:, :, i * BLOCK:kl],
                       kt[:, :, :kl]).astype(jnp.float32)
        before = (jnp.arange(kl)[None, :] < (i * BLOCK + qoff)[:, None])[None, None]
        log_fail = jnp.where(before, jax.nn.log_sigmoid(-z), 0.0)
        a = jnp.where(before, jnp.exp(z + lax.cumsum(log_fail, axis=3, reverse=True)), 0.0)
        outs.append(jnp.einsum('bhqk,bhkd->bhqd', a.astype(v.dtype), vt[:, :, :kl]))
    o = jnp.concatenate(outs, axis=2)
    return o.transpose(0, 2, 1, 3).reshape(B, S, H * Dh)


def hybrid_layer(x, pos, w_in, gm_w_s, gm_b_s, gm_norm_gain, sw_sinks,
                 w_branch_a, w_branch_b, w_branch_c, b_merge, w_out, g_pre, g_post):
    B, S, _ = x.shape
    h = rms_norm(x, g_pre)
    p = h @ w_in
    offsets = np.cumsum(np.array(IN_SPLITS))[:-1].tolist()
    (u_a, v_a, gate_a, q_b, k_b, v_b, gate_b,
     q_c, k_c, v_c, gate_c, merge_logits) = jnp.split(p, offsets, axis=-1)

    y_a = chunk_gmlp(u_a, v_a, gm_w_s, gm_b_s, gm_norm_gain) * jax.nn.silu(gate_a)

    qh = rope(q_b.reshape(B, S, SW_HEADS, HEAD_DIM), pos)
    kh = rope(k_b.reshape(B, S, SW_KV_HEADS, HEAD_DIM), pos)
    vh = v_b.reshape(B, S, SW_KV_HEADS, HEAD_DIM)
    y_b = sliding_window_attention(qh, kh, vh, sw_sinks) * jax.nn.silu(gate_b)

    y_c = stick_breaking_attention(q_c.reshape(B, S, SB_HEADS, SB_HEAD_DIM),
                                   k_c.reshape(B, S, SB_HEADS, SB_HEAD_DIM),
                                   v_c.reshape(B, S, SB_HEADS, SB_HEAD_DIM)) * jax.nn.silu(gate_c)

    gates = jax.nn.sigmoid(merge_logits.reshape(B, S, N_BRANCH, D_MODEL) + b_merge)
    merged = (gates[:, :, 0] * (y_a @ w_branch_a)
              + gates[:, :, 1] * (y_b @ w_branch_b)
              + gates[:, :, 2] * (y_c @ w_branch_c))
    out = merged @ w_out
    return x + rms_norm(out, g_post)


def _fwd_setup_inputs(seed: int = 0) -> dict:
    key = jax.random.key(seed)
    ks = jax.random.split(key, 16)
    f32 = jnp.float32
    nrm = lambda k, shape, s: jax.random.normal(k, shape, f32) * s
    return {
        "x": nrm(ks[0], (BATCH, SEQ, D_MODEL), 1.0),
        "w_in": nrm(ks[1], (DEPTH, D_MODEL, IN_WIDTH), D_MODEL ** -0.5),
        "gm_w_s": nrm(ks[2], (DEPTH, GM_GROUPS, BLOCK, BLOCK), BLOCK ** -0.5),
        "gm_b_s": 1.0 + nrm(ks[3], (DEPTH, GM_GROUPS, BLOCK), 0.02),
        "gm_norm_gain": 1.0 + nrm(ks[4], (DEPTH, GM_WIDTH), 0.02),
        "sw_sinks": nrm(ks[5], (DEPTH, SW_HEADS), 1.0),
        "w_branch_a": nrm(ks[6], (DEPTH, GM_WIDTH, D_MODEL), GM_WIDTH ** -0.5),
        "w_branch_b": nrm(ks[7], (DEPTH, SW_WIDTH, D_MODEL), SW_WIDTH ** -0.5),
        "w_branch_c": nrm(ks[8], (DEPTH, SB_WIDTH, D_MODEL), SB_WIDTH ** -0.5),
        "b_merge": nrm(ks[9], (DEPTH, N_BRANCH, D_MODEL), 0.02),
        "w_out": nrm(ks[10], (DEPTH, D_MODEL, D_MODEL), D_MODEL ** -0.5),
        "g_pre": 1.0 + nrm(ks[11], (DEPTH, D_MODEL), 0.02),
        "g_post": 1.0 + nrm(ks[12], (DEPTH, D_MODEL), 0.02),
    }


def _fwd_reference(x, w_in, gm_w_s, gm_b_s, gm_norm_gain, sw_sinks, w_branch_a, w_branch_b,
              w_branch_c, b_merge, w_out, g_pre, g_post):
    pos = jnp.arange(x.shape[1])
    for l in range(DEPTH):
        x = hybrid_layer(x, pos, w_in[l], gm_w_s[l], gm_b_s[l], gm_norm_gain[l], sw_sinks[l],
                         w_branch_a[l], w_branch_b[l], w_branch_c[l], b_merge[l], w_out[l],
                         g_pre[l], g_post[l])
    return x


import jax as _jax
import jax.numpy as _jnp

TWIN_FORMAT = 'train_step'
FWD_PARAMS = ['x', 'w_in', 'gm_w_s', 'gm_b_s', 'gm_norm_gain', 'sw_sinks', 'w_branch_a', 'w_branch_b', 'w_branch_c', 'b_merge', 'w_out', 'g_pre', 'g_post']
TWIN_WEIGHTS = ['w_in', 'gm_w_s', 'gm_b_s', 'gm_norm_gain', 'sw_sinks', 'w_branch_a', 'w_branch_b', 'w_branch_c', 'b_merge', 'w_out', 'g_pre', 'g_post']
TWIN_DIFF_INPUT = 'x'
TWIN_INPUTS = ['x', 'w_in', 'gm_w_s', 'gm_b_s', 'gm_norm_gain', 'sw_sinks', 'w_branch_a', 'w_branch_b', 'w_branch_c', 'b_merge', 'w_out', 'g_pre', 'g_post', 'loss_target', 'm_w_in', 'm_gm_w_s', 'm_gm_b_s', 'm_gm_norm_gain', 'm_sw_sinks', 'm_w_branch_a', 'm_w_branch_b', 'm_w_branch_c', 'm_b_merge', 'm_w_out', 'm_g_pre', 'm_g_post', 'v_w_in', 'v_gm_w_s', 'v_gm_b_s', 'v_gm_norm_gain', 'v_sw_sinks', 'v_w_branch_a', 'v_w_branch_b', 'v_w_branch_c', 'v_b_merge', 'v_w_out', 'v_g_pre', 'v_g_post']
TWIN_OUTPUTS = ['loss', 'grad_x', 'grad_w_in', 'grad_gm_w_s', 'grad_gm_b_s', 'grad_gm_norm_gain', 'grad_sw_sinks', 'grad_w_branch_a', 'grad_w_branch_b', 'grad_w_branch_c', 'grad_b_merge', 'grad_w_out', 'grad_g_pre', 'grad_g_post', 'delta_w_in', 'delta_gm_w_s', 'delta_gm_b_s', 'delta_gm_norm_gain', 'delta_sw_sinks', 'delta_w_branch_a', 'delta_w_branch_b', 'delta_w_branch_c', 'delta_b_merge', 'delta_w_out', 'delta_g_pre', 'delta_g_post', 'new_m_w_in', 'new_m_gm_w_s', 'new_m_gm_b_s', 'new_m_gm_norm_gain', 'new_m_sw_sinks', 'new_m_w_branch_a', 'new_m_w_branch_b', 'new_m_w_branch_c', 'new_m_b_merge', 'new_m_w_out', 'new_m_g_pre', 'new_m_g_post', 'new_v_w_in', 'new_v_gm_w_s', 'new_v_gm_b_s', 'new_v_gm_norm_gain', 'new_v_sw_sinks', 'new_v_w_branch_a', 'new_v_w_branch_b', 'new_v_w_branch_c', 'new_v_b_merge', 'new_v_w_out', 'new_v_g_pre', 'new_v_g_post']
TWIN_LEAF_KINDS = {'loss': 'loss', 'grad_x': 'grad_x', 'grad_w_in': 'grad_w', 'grad_gm_w_s': 'grad_w', 'grad_gm_b_s': 'grad_w', 'grad_gm_norm_gain': 'grad_w', 'grad_sw_sinks': 'grad_w', 'grad_w_branch_a': 'grad_w', 'grad_w_branch_b': 'grad_w', 'grad_w_branch_c': 'grad_w', 'grad_b_merge': 'grad_w', 'grad_w_out': 'grad_w', 'grad_g_pre': 'grad_w', 'grad_g_post': 'grad_w', 'delta_w_in': 'delta_w', 'delta_gm_w_s': 'delta_w', 'delta_gm_b_s': 'delta_w', 'delta_gm_norm_gain': 'delta_w', 'delta_sw_sinks': 'delta_w', 'delta_w_branch_a': 'delta_w', 'delta_w_branch_b': 'delta_w', 'delta_w_branch_c': 'delta_w', 'delta_b_merge': 'delta_w', 'delta_w_out': 'delta_w', 'delta_g_pre': 'delta_w', 'delta_g_post': 'delta_w', 'new_m_w_in': 'new_m', 'new_m_gm_w_s': 'new_m', 'new_m_gm_b_s': 'new_m', 'new_m_gm_norm_gain': 'new_m', 'new_m_sw_sinks': 'new_m', 'new_m_w_branch_a': 'new_m', 'new_m_w_branch_b': 'new_m', 'new_m_w_branch_c': 'new_m', 'new_m_b_merge': 'new_m', 'new_m_w_out': 'new_m', 'new_m_g_pre': 'new_m', 'new_m_g_post': 'new_m', 'new_v_w_in': 'new_v', 'new_v_gm_w_s': 'new_v', 'new_v_gm_b_s': 'new_v', 'new_v_gm_norm_gain': 'new_v', 'new_v_sw_sinks': 'new_v', 'new_v_w_branch_a': 'new_v', 'new_v_w_branch_b': 'new_v', 'new_v_w_branch_c': 'new_v', 'new_v_b_merge': 'new_v', 'new_v_w_out': 'new_v', 'new_v_g_pre': 'new_v', 'new_v_g_post': 'new_v'}


def _forward(args):
    return _fwd_reference(*[args[k] for k in FWD_PARAMS])


def _output_shape():
    def fwd():
        inp = _fwd_setup_inputs(0)
        return _fwd_reference(*[inp[k] for k in FWD_PARAMS])
    out = _jax.eval_shape(fwd)
    return out.shape, out.dtype

N_MICROBATCH = 1
ADAM_LR = 0.001
ADAM_B1 = 0.9
ADAM_B2 = 0.999
ADAM_EPS = 1e-08
ADAM_WD = 0.01
ADAM_STEP = 10
PER_EXAMPLE_BATCH_AXIS = {'x': 0, 'loss_target': 0}
SHARED_INPUTS = []
_WEIGHT_DTYPES = {'w_in': _jnp.float32, 'gm_w_s': _jnp.float32, 'gm_b_s': _jnp.float32, 'gm_norm_gain': _jnp.float32, 'sw_sinks': _jnp.float32, 'w_branch_a': _jnp.float32, 'w_branch_b': _jnp.float32, 'w_branch_c': _jnp.float32, 'b_merge': _jnp.float32, 'w_out': _jnp.float32, 'g_pre': _jnp.float32, 'g_post': _jnp.float32}
MOMENT_SCALE = {'w_in': 8.599328e-01, 'gm_w_s': 9.711938e-01, 'gm_b_s': 1.382954e+00, 'gm_norm_gain': 1.053725e+00, 'sw_sinks': 3.644076e-01, 'w_branch_a': 1.497049e+00, 'w_branch_b': 2.260379e-01, 'w_branch_c': 6.968209e-01, 'b_merge': 4.022108e-01, 'w_out': 1.691525e+00, 'g_pre': 2.409842e+00, 'g_post': 1.276774e+02}


def _to_microbatches(a, axis):
    t = _jnp.moveaxis(a, axis, 0)
    t = t.reshape((N_MICROBATCH, t.shape[0] // N_MICROBATCH) + t.shape[1:])
    return _jnp.moveaxis(t, 1, axis + 1)


def setup_inputs(seed: int = 0) -> dict:
    inp = _fwd_setup_inputs(seed)
    key = _jax.random.fold_in(_jax.random.key(seed), 7919)
    shape, _ = _output_shape()
    out = dict(inp)
    out["loss_target"] = _jax.random.normal(_jax.random.fold_in(key, 0), shape, _jnp.float32)
    for i, name in enumerate(TWIN_WEIGHTS):
        w = inp[name].astype(_jnp.float32)
        if MOMENT_SCALE is None:
            s = _jnp.sqrt(_jnp.mean(_jnp.square(w)) + 1e-30)
        else:
            s = MOMENT_SCALE[name]
        km, kv = _jax.random.split(_jax.random.fold_in(key, i + 1))
        out[name] = w
        out["m_" + name] = s * _jax.random.normal(km, w.shape, _jnp.float32)
        out["v_" + name] = (s * s) * _jax.random.uniform(kv, w.shape, _jnp.float32, 0.5, 1.5)
    if N_MICROBATCH > 1:
        for name, axis in PER_EXAMPLE_BATCH_AXIS.items():
            out[name] = _to_microbatches(out[name], axis)
    return {'x': out['x'], 'w_in': out['w_in'], 'gm_w_s': out['gm_w_s'], 'gm_b_s': out['gm_b_s'], 'gm_norm_gain': out['gm_norm_gain'], 'sw_sinks': out['sw_sinks'], 'w_branch_a': out['w_branch_a'], 'w_branch_b': out['w_branch_b'], 'w_branch_c': out['w_branch_c'], 'b_merge': out['b_merge'], 'w_out': out['w_out'], 'g_pre': out['g_pre'], 'g_post': out['g_post'], 'loss_target': out['loss_target'], 'm_w_in': out['m_w_in'], 'm_gm_w_s': out['m_gm_w_s'], 'm_gm_b_s': out['m_gm_b_s'], 'm_gm_norm_gain': out['m_gm_norm_gain'], 'm_sw_sinks': out['m_sw_sinks'], 'm_w_branch_a': out['m_w_branch_a'], 'm_w_branch_b': out['m_w_branch_b'], 'm_w_branch_c': out['m_w_branch_c'], 'm_b_merge': out['m_b_merge'], 'm_w_out': out['m_w_out'], 'm_g_pre': out['m_g_pre'], 'm_g_post': out['m_g_post'], 'v_w_in': out['v_w_in'], 'v_gm_w_s': out['v_gm_w_s'], 'v_gm_b_s': out['v_gm_b_s'], 'v_gm_norm_gain': out['v_gm_norm_gain'], 'v_sw_sinks': out['v_sw_sinks'], 'v_w_branch_a': out['v_w_branch_a'], 'v_w_branch_b': out['v_w_branch_b'], 'v_w_branch_c': out['v_w_branch_c'], 'v_b_merge': out['v_b_merge'], 'v_w_out': out['v_w_out'], 'v_g_pre': out['v_g_pre'], 'v_g_post': out['v_g_post']}


def _loss(weights, diff, rest, loss_target):
    with _jax.named_scope("forward"):
        args = {**rest, TWIN_DIFF_INPUT: diff, **{k: w.astype(_WEIGHT_DTYPES[k]) for k, w in weights.items()}}
        y = _forward(args)
    with _jax.named_scope("loss_head"):
        err = _jnp.square(y.astype(_jnp.float32) - loss_target)
        return 0.5 * _jnp.sum(_jnp.mean(err, axis=-1)) if err.ndim else 0.5 * err


def _adamw(w, g, m, v):
    m = ADAM_B1 * m + (1.0 - ADAM_B1) * g
    v = ADAM_B2 * v + (1.0 - ADAM_B2) * _jnp.square(g)
    m_hat = m / (1.0 - ADAM_B1 ** ADAM_STEP)
    v_hat = v / (1.0 - ADAM_B2 ** ADAM_STEP)
    delta = -ADAM_LR * (m_hat / (_jnp.sqrt(v_hat) + ADAM_EPS) + ADAM_WD * w)
    return delta, m, v


def reference(x, w_in, gm_w_s, gm_b_s, gm_norm_gain, sw_sinks, w_branch_a, w_branch_b, w_branch_c, b_merge, w_out, g_pre, g_post, loss_target, m_w_in, m_gm_w_s, m_gm_b_s, m_gm_norm_gain, m_sw_sinks, m_w_branch_a, m_w_branch_b, m_w_branch_c, m_b_merge, m_w_out, m_g_pre, m_g_post, v_w_in, v_gm_w_s, v_gm_b_s, v_gm_norm_gain, v_sw_sinks, v_w_branch_a, v_w_branch_b, v_w_branch_c, v_b_merge, v_w_out, v_g_pre, v_g_post):
    given = dict(x=x, w_in=w_in, gm_w_s=gm_w_s, gm_b_s=gm_b_s, gm_norm_gain=gm_norm_gain, sw_sinks=sw_sinks, w_branch_a=w_branch_a, w_branch_b=w_branch_b, w_branch_c=w_branch_c, b_merge=b_merge, w_out=w_out, g_pre=g_pre, g_post=g_post, loss_target=loss_target, m_w_in=m_w_in, m_gm_w_s=m_gm_w_s, m_gm_b_s=m_gm_b_s, m_gm_norm_gain=m_gm_norm_gain, m_sw_sinks=m_sw_sinks, m_w_branch_a=m_w_branch_a, m_w_branch_b=m_w_branch_b, m_w_branch_c=m_w_branch_c, m_b_merge=m_b_merge, m_w_out=m_w_out, m_g_pre=m_g_pre, m_g_post=m_g_post, v_w_in=v_w_in, v_gm_w_s=v_gm_w_s, v_gm_b_s=v_gm_b_s, v_gm_norm_gain=v_gm_norm_gain, v_sw_sinks=v_sw_sinks, v_w_branch_a=v_w_branch_a, v_w_branch_b=v_w_branch_b, v_w_branch_c=v_w_branch_c, v_b_merge=v_b_merge, v_w_out=v_w_out, v_g_pre=v_g_pre, v_g_post=v_g_post)
    weights = {n: given[n] for n in TWIN_WEIGHTS}
    shared = {n: given[n] for n in SHARED_INPUTS}
    per_example = {n: given[n] for n in ['x']}
    grad_fn = _jax.value_and_grad(_loss, argnums=(0, 1))

    def one_microbatch(ex, loss_target):
        ex = dict(ex)
        diff = ex.pop(TWIN_DIFF_INPUT)
        return grad_fn(weights, diff, {**shared, **ex}, loss_target)

    if N_MICROBATCH == 1:
        loss, (grad_w, grad_x) = one_microbatch(per_example, given["loss_target"])
    else:
        def body(carry, xs):
            loss_sum, grad_sum = carry
            l_k, (gw_k, gx_k) = one_microbatch(xs[0], xs[1])
            with _jax.named_scope("update"):
                return (loss_sum + l_k, _jax.tree.map(_jnp.add, grad_sum, gw_k)), gx_k

        init = (_jnp.zeros((), _jnp.float32), _jax.tree.map(_jnp.zeros_like, weights))
        (loss, grad_w), grad_x = _jax.lax.scan(body, init, (per_example, given["loss_target"]))
    with _jax.named_scope("update"):
        delta_w, new_m, new_v = {}, {}, {}
        for n in TWIN_WEIGHTS:
            delta_w[n], new_m[n], new_v[n] = _adamw(weights[n], grad_w[n], given["m_" + n], given["v_" + n])
    return (loss, grad_x, *[grad_w[n] for n in TWIN_WEIGHTS], *[delta_w[n] for n in TWIN_WEIGHTS],
            *[new_m[n] for n in TWIN_WEIGHTS], *[new_v[n] for n in TWIN_WEIGHTS])
```

```python
import functools

import numpy as np
import jax
import jax.numpy as jnp
from jax import lax
from jax.experimental import pallas as pl
from jax.experimental.pallas import tpu as pltpu

F32 = jnp.float32
BF16 = jnp.bfloat16
ACT = jnp.bfloat16

D = 1024
DEPTH = 4
BLOCK = 128
EPS = 1e-6
NEG = -1e30
GM_GROUPS = 4
GM_WIDTH = 512
HEAD_DIM = 64
SW_HEADS = 8
SB_HEADS = 4
SB_HEAD_DIM = 128
ROPE_THETA = 10000.0
IN_WIDTH = 7936
IN_PAD = 8192

O_UA, O_VA, O_GA, O_QB, O_KB, O_VB, O_GB = 0, 512, 1024, 1536, 2048, 2176, 2304
O_QC, O_KC, O_VC, O_GC, O_MG = 2816, 3328, 3840, 4352, 4864
_PERM = ((O_MG, 3072), (O_UA, 512), (O_VA, 512), (O_GA, 512), (O_QB, 512), (O_GB, 512),
         (O_QC, 512), (O_KC, 512), (O_VC, 512), (O_GC, 512), (O_KB, 128), (O_VB, 128))
N_MG, N_UA, N_VA, N_GA, N_QB, N_GB = 0, 3072, 3584, 4096, 4608, 5120
N_QC, N_KC, N_VC, N_GC, N_KB, N_VB = 5632, 6144, 6656, 7168, 7680, 7808

ADAM_LR, ADAM_B1, ADAM_B2, ADAM_EPS, ADAM_WD, ADAM_STEP = 0.001, 0.9, 0.999, 1e-08, 0.01, 10

SB_EXIT = -104.0
V7X_VMEM_LIMIT = 48 * 1024 * 1024
MESH = pl.DeviceIdType.MESH


def _call(body, *, name, grid, in_specs, out_specs, out_shape, scratch=()):
    return pl.pallas_call(
        body, name=name, grid=grid, in_specs=in_specs, out_specs=out_specs, out_shape=out_shape,
        scratch_shapes=list(scratch),
        compiler_params=pltpu.CompilerParams(
            dimension_semantics=("arbitrary",) * len(grid), vmem_limit_bytes=V7X_VMEM_LIMIT))


def _sigmoid(x):
    return 1.0 / (1.0 + jnp.exp(-x))


def _silu_and_grad(x):
    s = _sigmoid(x)
    return x * s, s * (1.0 + x * (1.0 - s))


def _dot(a, b):
    return jnp.dot(a, b, preferred_element_type=F32)


def _dot_nt(a, b):
    return lax.dot_general(a, b, (((1,), (1,)), ((), ())), preferred_element_type=F32)


def _dot_tn(a, b):
    return lax.dot_general(a, b, (((0,), (0,)), ((), ())), preferred_element_type=F32)


def _full(shape):
    return pl.BlockSpec(shape, lambda *_: (0,) * len(shape))


def _in_proj(x, g_pre, w_in):
    T = x.shape[0]
    tm, tn = min(1024, T), 512

    def body(x_ref, g_ref, w_ref, p_ref, h_ref):
        @pl.when(pl.program_id(1) == 0)
        def _():
            xf = x_ref[...]
            r = lax.rsqrt(jnp.mean(xf * xf, axis=-1, keepdims=True) + EPS)
            h_ref[...] = (xf * r * g_ref[...]).astype(BF16)
        p_ref[...] = _dot(h_ref[...], w_ref[...]).astype(p_ref.dtype)

    return _call(
        body, name="in_proj", grid=(T // tm, IN_PAD // tn),
        in_specs=[pl.BlockSpec((tm, D), lambda i, j: (i, 0)), _full((1, D)),
                  pl.BlockSpec((D, tn), lambda i, j: (0, j))],
        out_specs=[pl.BlockSpec((tm, tn), lambda i, j: (i, j)), pl.BlockSpec((tm, D), lambda i, j: (i, 0))],
        out_shape=[jax.ShapeDtypeStruct((T, IN_PAD), ACT), jax.ShapeDtypeStruct((T, D), BF16)],
    )(x, g_pre, w_in)


def _in_bwd(dp, w_in, x, dxn, g_pre):
    T = x.shape[0]
    tm, tk = min(1024, T), 1024
    nk = IN_PAD // tk

    def body(dp_ref, w_ref, x_ref, dxn_ref, g_ref, dx_ref, dg_ref, acc):
        i, k = pl.program_id(0), pl.program_id(1)

        @pl.when(k == 0)
        def _():
            acc[...] = jnp.zeros_like(acc)

        @pl.when((i == 0) & (k == 0))
        def _():
            dg_ref[...] = jnp.zeros_like(dg_ref)

        acc[...] += _dot_nt(dp_ref[...], w_ref[...])

        @pl.when(k == nk - 1)
        def _():
            dh = acc[...]
            xf = x_ref[...]
            r = lax.rsqrt(jnp.mean(xf * xf, axis=-1, keepdims=True) + EPS)
            a = dh * g_ref[...]
            dx_ref[...] = dxn_ref[...] + r * a - xf * (r * r * r) * jnp.mean(a * xf, axis=-1, keepdims=True)
            dg_ref[...] += jnp.sum(dh * xf * r, axis=0, keepdims=True)

    return _call(
        body, name="in_bwd", grid=(T // tm, nk),
        in_specs=[pl.BlockSpec((tm, tk), lambda i, k: (i, k)), pl.BlockSpec((D, tk), lambda i, k: (0, k)),
                  pl.BlockSpec((tm, D), lambda i, k: (i, 0)), pl.BlockSpec((tm, D), lambda i, k: (i, 0)),
                  _full((1, D))],
        out_specs=[pl.BlockSpec((tm, D), lambda i, k: (i, 0)), _full((1, D))],
        out_shape=[jax.ShapeDtypeStruct((T, D), F32), jax.ShapeDtypeStruct((1, D), F32)],
        scratch=[pltpu.VMEM((tm, D), F32)],
    )(dp, w_in, x, dxn, g_pre)


def _matmul_tn(a, b, name):
    T, K = a.shape
    N = b.shape[1]
    tk, tn, tt = min(K, 1024), min(N, 512), min(T, 1024)

    def body(a_ref, b_ref, o_ref):
        @pl.when(pl.program_id(2) == 0)
        def _():
            o_ref[...] = jnp.zeros_like(o_ref)
        o_ref[...] += _dot_tn(a_ref[...], b_ref[...])

    return _call(
        body, name=name, grid=(K // tk, N // tn, T // tt),
        in_specs=[pl.BlockSpec((tt, tk), lambda i, j, t: (t, i)), pl.BlockSpec((tt, tn), lambda i, j, t: (t, j))],
        out_specs=pl.BlockSpec((tk, tn), lambda i, j, t: (i, j)),
        out_shape=jax.ShapeDtypeStruct((K, N), F32),
    )(a, b)


def _gm_forward_parts(v_ref, gv_ref, ws_ref, bf_ref, nch):
    vf = v_ref[...].astype(F32)
    mu = jnp.mean(vf, axis=-1, keepdims=True)
    xc = vf - mu
    rstd = lax.rsqrt(jnp.mean(xc * xc, axis=-1, keepdims=True) + EPS)
    xhat = xc * rstd
    vnb = (xhat * gv_ref[...]).astype(BF16)
    row = lax.broadcasted_iota(jnp.int32, (BLOCK, BLOCK), 0)
    col = lax.broadcasted_iota(jnp.int32, (BLOCK, BLOCK), 1)
    mixed, vcats, wgs = [], [], []
    for g in range(GM_GROUPS):
        vg = vnb[:, BLOCK * g:BLOCK * (g + 1)]
        vcat = jnp.concatenate([vg[BLOCK * k:BLOCK * (k + 1), :] for k in range(nch)], axis=1)
        wg = jnp.where(row >= col, ws_ref[g], 0.0)
        m = _dot(wg.astype(BF16), vcat)
        mixed.append(jnp.concatenate(
            [m[:, BLOCK * k:BLOCK * (k + 1)] + bf_ref[g] for k in range(nch)], axis=0))
        vcats.append(vcat)
        wgs.append(wg)
    return xhat, rstd, jnp.concatenate(mixed, axis=1), vcats, wgs, row >= col


def _gmlp_fwd(p, ws, bfull, gv):
    T = p.shape[0]
    tm = min(512, T)
    nch = tm // BLOCK

    def body(u_ref, v_ref, gt_ref, ws_ref, bf_ref, gv_ref, y_ref):
        _, _, mixed, _, _, _ = _gm_forward_parts(v_ref, gv_ref, ws_ref, bf_ref, nch)
        sg, _ = _silu_and_grad(gt_ref[...].astype(F32))
        y_ref[...] = (u_ref[...].astype(F32) * mixed * sg).astype(y_ref.dtype)

    seg = lambda off: pl.BlockSpec((tm, 512), lambda i: (i, off // 512))
    return _call(
        body, name="gmlp_fwd", grid=(T // tm,),
        in_specs=[seg(N_UA), seg(N_VA), seg(N_GA), _full((GM_GROUPS, BLOCK, BLOCK)),
                  _full((GM_GROUPS, BLOCK, BLOCK)), _full((1, GM_WIDTH))],
        out_specs=pl.BlockSpec((tm, 512), lambda i: (i, 0)),
        out_shape=jax.ShapeDtypeStruct((T, GM_WIDTH), BF16),
    )(p, p, p, ws, bfull, gv)


def _gmlp_bwd(p, dy, ws, bfull, gv):
    T = p.shape[0]
    tm = min(512, T)
    nch = tm // BLOCK

    def body(u_ref, v_ref, gt_ref, dy_ref, ws_ref, bf_ref, gv_ref, dp_ref, dws_ref, db_ref, dgv_ref):
        @pl.when(pl.program_id(0) == 0)
        def _():
            dws_ref[...] = jnp.zeros_like(dws_ref)
            db_ref[...] = jnp.zeros_like(db_ref)
            dgv_ref[...] = jnp.zeros_like(dgv_ref)

        xhat, rstd, mixed, vcats, wgs, tril = _gm_forward_parts(v_ref, gv_ref, ws_ref, bf_ref, nch)
        u = u_ref[...].astype(F32)
        gt = gt_ref[...].astype(F32)
        dyf = dy_ref[...].astype(F32)
        sg, dsg = _silu_and_grad(gt)
        du = dyf * mixed * sg
        dmixed = dyf * u * sg
        dgate = dyf * (u * mixed) * dsg
        dvn = []
        for g in range(GM_GROUPS):
            dmg = dmixed[:, BLOCK * g:BLOCK * (g + 1)]
            chunks = [dmg[BLOCK * k:BLOCK * (k + 1), :] for k in range(nch)]
            dmcat = jnp.concatenate(chunks, axis=1).astype(BF16)
            dws_ref[g] += jnp.where(tril, _dot_nt(dmcat, vcats[g]), 0.0)
            dvcat = _dot(wgs[g].T.astype(BF16), dmcat)
            dvn.append(jnp.concatenate([dvcat[:, BLOCK * k:BLOCK * (k + 1)] for k in range(nch)], axis=0))
            db_ref[:, BLOCK * g:BLOCK * (g + 1)] += functools.reduce(lambda a, b: a + b, chunks)
        dvn = jnp.concatenate(dvn, axis=1)
        dgv_ref[...] += jnp.sum(dvn * xhat, axis=0, keepdims=True)
        dxh = dvn * gv_ref[...]
        dv = rstd * (dxh - jnp.mean(dxh, axis=-1, keepdims=True)
                     - xhat * jnp.mean(dxh * xhat, axis=-1, keepdims=True))
        dp_ref[:, 0:512] = du.astype(dp_ref.dtype)
        dp_ref[:, 512:1024] = dv.astype(dp_ref.dtype)
        dp_ref[:, 1024:1536] = dgate.astype(dp_ref.dtype)

    seg = lambda off: pl.BlockSpec((tm, 512), lambda i: (i, off // 512))
    return _call(
        body, name="gmlp_bwd", grid=(T // tm,),
        in_specs=[seg(N_UA), seg(N_VA), seg(N_GA), pl.BlockSpec((tm, 512), lambda i: (i, 0)),
                  _full((GM_GROUPS, BLOCK, BLOCK)), _full((GM_GROUPS, BLOCK, BLOCK)), _full((1, GM_WIDTH))],
        out_specs=[pl.BlockSpec((tm, 1536), lambda i: (i, 0)), _full((GM_GROUPS, BLOCK, BLOCK)),
                   _full((BLOCK, GM_WIDTH)), _full((1, GM_WIDTH))],
        out_shape=[jax.ShapeDtypeStruct((T, 1536), ACT), jax.ShapeDtypeStruct((GM_GROUPS, BLOCK, BLOCK), F32),
                   jax.ShapeDtypeStruct((BLOCK, GM_WIDTH), F32), jax.ShapeDtypeStruct((1, GM_WIDTH), F32)],
    )(p, p, p, dy, ws, bfull, gv)


def _swap_halves(x):
    lane = lax.broadcasted_iota(jnp.int32, x.shape, 1) % HEAD_DIM
    return jnp.where(lane < HEAD_DIM // 2, pltpu.roll(x, 96, 1), pltpu.roll(x, 32, 1))


def _rope_fwd(p, cos_t, sin_t):
    T = p.shape[0]
    tm = min(512, T)

    def body(q_ref, k_ref, c_ref, s_ref, o_ref):
        c, s = c_ref[...], s_ref[...]
        for G in range(5):
            xg = (q_ref[:, 128 * G:128 * (G + 1)] if G < 4 else k_ref[...]).astype(F32)
            o_ref[:, 128 * G:128 * (G + 1)] = (xg * c + _swap_halves(xg) * s).astype(o_ref.dtype)

    return _call(
        body, name="rope_fwd", grid=(T // tm,),
        in_specs=[pl.BlockSpec((tm, 512), lambda i: (i, N_QB // 512)),
                  pl.BlockSpec((tm, 128), lambda i: (i, N_KB // 128)),
                  pl.BlockSpec((tm, 128), lambda i: (i, 0)), pl.BlockSpec((tm, 128), lambda i: (i, 0))],
        out_specs=pl.BlockSpec((tm, 640), lambda i: (i, 0)),
        out_shape=jax.ShapeDtypeStruct((T, 640), BF16),
    )(p, p, cos_t, sin_t)


def _rope_bwd(dq, dk, cos_t, sin_t):
    T = dq.shape[0]
    tm = min(512, T)

    def body(dq_ref, dk_ref, c_ref, s_ref, oq_ref, ok_ref):
        c, s = c_ref[...], s_ref[...]
        for G in range(4):
            d = dq_ref[:, 128 * G:128 * (G + 1)]
            oq_ref[:, 128 * G:128 * (G + 1)] = (d * c + _swap_halves(d * s)).astype(oq_ref.dtype)
        d = dk_ref[...]
        ok_ref[...] = (d * c + _swap_halves(d * s)).astype(ok_ref.dtype)

    return _call(
        body, name="rope_bwd", grid=(T // tm,),
        in_specs=[pl.BlockSpec((tm, 512), lambda i: (i, 0)), pl.BlockSpec((tm, 128), lambda i: (i, 0)),
                  pl.BlockSpec((tm, 128), lambda i: (i, 0)), pl.BlockSpec((tm, 128), lambda i: (i, 0))],
        out_specs=[pl.BlockSpec((tm, 512), lambda i: (i, 0)), pl.BlockSpec((tm, 128), lambda i: (i, 0))],
        out_shape=[jax.ShapeDtypeStruct((T, 512), ACT), jax.ShapeDtypeStruct((T, 128), ACT)],
    )(dq, dk, cos_t, sin_t)


def _dup_heads(x):
    left = lax.broadcasted_iota(jnp.int32, x.shape, 1) < HEAD_DIM
    r = pltpu.roll(x, HEAD_DIM, 1)
    return jnp.where(left, x, r).astype(BF16), jnp.where(left, r, x).astype(BF16)


def _fold_heads(acc0, acc1):
    left = lax.broadcasted_iota(jnp.int32, acc0.shape, 1) < HEAD_DIM
    t0 = acc0 + pltpu.roll(acc0, HEAD_DIM, 1)
    t1 = acc1 + pltpu.roll(acc1, HEAD_DIM, 1)
    return jnp.where(left, t0, t1)


def _swa_valid(base):
    qpos = base + lax.broadcasted_iota(jnp.int32, (BLOCK, 2 * BLOCK), 0)
    kpos = base - BLOCK + lax.broadcasted_iota(jnp.int32, (BLOCK, 2 * BLOCK), 1)
    return jnp.logical_and(kpos >= 0, jnp.logical_and(kpos <= qpos, kpos > qpos - BLOCK))


def _swa_probs(qm, kk, valid, sink):
    s = jnp.where(valid, _dot_nt(qm, kk) * (HEAD_DIM ** -0.5), NEG)
    m = jnp.maximum(jnp.max(s, axis=-1, keepdims=True), sink)
    e = jnp.exp(s - m)
    es = jnp.exp(sink - m)
    inv = 1.0 / (jnp.sum(e, axis=-1, keepdims=True) + es)
    return e * inv, es * inv


def _swa_specs(T, bq, rev):
    n = T // bq
    blk = (lambda i: n - 1 - i) if rev else (lambda i: i)
    halo = lambda i: jnp.maximum(blk(i) * (bq // BLOCK) - 1, 0)
    return blk, [
        pl.BlockSpec(memory_space=pltpu.SMEM),
        pl.BlockSpec((bq, 512), lambda i: (blk(i), 0)),
        pl.BlockSpec((bq, 128), lambda i: (blk(i), 4)),
        pl.BlockSpec((BLOCK, 128), lambda i: (halo(i), 4)),
        pl.BlockSpec((bq, 128), lambda i: (blk(i), N_VB // 128)),
        pl.BlockSpec((BLOCK, 128), lambda i: (halo(i), N_VB // 128)),
        pl.BlockSpec((bq, 512), lambda i: (blk(i), N_GB // 512)),
    ]


def _swa_fwd(qkr, p, sinks):
    T = p.shape[0]
    bq = min(512, T)
    nsb = bq // BLOCK
    blk, specs = _swa_specs(T, bq, False)

    def body(sink_ref, q_ref, kc_ref, kh_ref, vc_ref, vh_ref, gt_ref, y_ref):
        base = blk(pl.program_id(0)) * bq
        kd = _dup_heads(jnp.concatenate([kh_ref[...], kc_ref[...]], axis=0).astype(F32))
        vd = _dup_heads(jnp.concatenate([vh_ref[...], vc_ref[...]], axis=0).astype(F32))
        left = lax.broadcasted_iota(jnp.int32, (BLOCK, BLOCK), 1) < HEAD_DIM
        for sb in range(nsb):
            r0 = sb * BLOCK
            valid = _swa_valid(base + r0)
            for G in range(4):
                kk, vv = kd[G // 2][r0:r0 + 2 * BLOCK], vd[G // 2][r0:r0 + 2 * BLOCK]
                q128 = q_ref[r0:r0 + BLOCK, 128 * G:128 * (G + 1)]
                outs = []
                for hh in range(2):
                    qm = jnp.where(left if hh == 0 else jnp.logical_not(left), q128, jnp.zeros_like(q128))
                    pr, _ = _swa_probs(qm, kk, valid, sink_ref[2 * G + hh])
                    outs.append(_dot(pr.astype(BF16), vv))
                o = jnp.where(left, outs[0], outs[1])
                sg, _ = _silu_and_grad(gt_ref[r0:r0 + BLOCK, 128 * G:128 * (G + 1)].astype(F32))
                y_ref[r0:r0 + BLOCK, 128 * G:128 * (G + 1)] = (o * sg).astype(y_ref.dtype)

    return _call(
        body, name="swa_fwd", grid=(T // bq,), in_specs=specs,
        out_specs=pl.BlockSpec((bq, 512), lambda i: (i, 0)),
        out_shape=jax.ShapeDtypeStruct((T, 512), BF16),
    )(sinks, qkr, qkr, qkr, p, p, p)


def _swa_bwd(qkr, p, dy, sinks):
    T = p.shape[0]
    bq = min(512, T)
    nsb = bq // BLOCK
    blk, specs = _swa_specs(T, bq, True)

    def body(sink_ref, q_ref, kc_ref, kh_ref, vc_ref, vh_ref, gt_ref, dy_ref,
             dq_ref, dk_ref, dv_ref, dg_ref, ds_ref, dk_acc, dv_acc, k_carry, v_carry):
        @pl.when(pl.program_id(0) == 0)
        def _():
            k_carry[...] = jnp.zeros_like(k_carry)
            v_carry[...] = jnp.zeros_like(v_carry)
            ds_ref[...] = jnp.zeros_like(ds_ref)

        base = blk(pl.program_id(0)) * bq
        dk_acc[...] = jnp.zeros_like(dk_acc)
        dv_acc[...] = jnp.zeros_like(dv_acc)
        kd = _dup_heads(jnp.concatenate([kh_ref[...], kc_ref[...]], axis=0).astype(F32))
        vd = _dup_heads(jnp.concatenate([vh_ref[...], vc_ref[...]], axis=0).astype(F32))
        left = lax.broadcasted_iota(jnp.int32, (BLOCK, BLOCK), 1) < HEAD_DIM
        for sb in range(nsb):
            r0 = sb * BLOCK
            valid = _swa_valid(base + r0)
            dkp = [jnp.zeros((2 * BLOCK, BLOCK), F32)] * 2
            dvp = [jnp.zeros((2 * BLOCK, BLOCK), F32)] * 2
            for G in range(4):
                g = G // 2
                kk, vv = kd[g][r0:r0 + 2 * BLOCK], vd[g][r0:r0 + 2 * BLOCK]
                q128 = q_ref[r0:r0 + BLOCK, 128 * G:128 * (G + 1)]
                cols = slice(128 * G, 128 * (G + 1))
                qms, prs, pss, outs = [], [], [], []
                for hh in range(2):
                    qm = jnp.where(left if hh == 0 else jnp.logical_not(left), q128, jnp.zeros_like(q128))
                    pr, ps = _swa_probs(qm, kk, valid, sink_ref[2 * G + hh])
                    qms.append(qm)
                    prs.append(pr)
                    pss.append(ps)
                    outs.append(_dot(pr.astype(BF16), vv))
                o = jnp.where(left, outs[0], outs[1])
                sg, dsg = _silu_and_grad(gt_ref[r0:r0 + BLOCK, cols].astype(F32))
                dyf = dy_ref[r0:r0 + BLOCK, cols].astype(F32)
                do = dyf * sg
                dg_ref[r0:r0 + BLOCK, cols] = (dyf * o * dsg).astype(dg_ref.dtype)
                dqs = []
                for hh in range(2):
                    dom = jnp.where(left if hh == 0 else jnp.logical_not(left), do, 0.0).astype(BF16)
                    dpv = _dot_nt(dom, vv)
                    delta = jnp.sum(prs[hh] * dpv, axis=-1, keepdims=True)
                    dsc = prs[hh] * (dpv - delta) * (HEAD_DIM ** -0.5)
                    h = 2 * G + hh
                    ds_ref[h:h + 1, :] += jnp.broadcast_to(
                        -jnp.sum(pss[hh] * delta, axis=0, keepdims=True), (1, 128))
                    dqs.append(_dot(dsc.astype(BF16), kk))
                    dkp[g] = dkp[g] + _dot(dsc.T.astype(BF16), qms[hh])
                    dvp[g] = dvp[g] + _dot(prs[hh].T.astype(BF16), dom)
                dq_ref[r0:r0 + BLOCK, cols] = jnp.where(left, dqs[0], dqs[1])
            dk_acc[r0:r0 + 2 * BLOCK, :] += _fold_heads(dkp[0], dkp[1])
            dv_acc[r0:r0 + 2 * BLOCK, :] += _fold_heads(dvp[0], dvp[1])
        if bq > BLOCK:
            dk_ref[0:bq - BLOCK, :] = dk_acc[BLOCK:bq, :]
            dv_ref[0:bq - BLOCK, :] = dv_acc[BLOCK:bq, :]
        dk_ref[bq - BLOCK:bq, :] = dk_acc[bq:bq + BLOCK, :] + k_carry[...]
        dv_ref[bq - BLOCK:bq, :] = dv_acc[bq:bq + BLOCK, :] + v_carry[...]
        k_carry[...] = dk_acc[0:BLOCK, :]
        v_carry[...] = dv_acc[0:BLOCK, :]

    rowblk = lambda w: pl.BlockSpec((bq, w), lambda i: (blk(i), 0))
    return _call(
        body, name="swa_bwd", grid=(T // bq,), in_specs=specs + [rowblk(512)],
        out_specs=[rowblk(512), rowblk(128), rowblk(128), rowblk(512), _full((SW_HEADS, 128))],
        out_shape=[jax.ShapeDtypeStruct((T, 512), F32), jax.ShapeDtypeStruct((T, 128), F32),
                   jax.ShapeDtypeStruct((T, 128), F32), jax.ShapeDtypeStruct((T, 512), ACT),
                   jax.ShapeDtypeStruct((SW_HEADS, 128), F32)],
        scratch=[pltpu.VMEM((bq + BLOCK, 128), F32), pltpu.VMEM((bq + BLOCK, 128), F32),
                 pltpu.VMEM((BLOCK, 128), F32), pltpu.VMEM((BLOCK, 128), F32)],
    )(sinks, qkr, qkr, qkr, p, p, p, dy)


def _split_dot(x, tri):
    hi = x.astype(BF16)
    lo = (x - hi.astype(F32)).astype(BF16)
    return _dot(hi, tri) + _dot(lo, tri)


def _sb_block(qs, kj, qpos0, kpos0, r_in, tri_incl):
    tq, kb = qs.shape[0], kj.shape[0]
    z = _dot_nt(qs, kj)
    qpos = qpos0 + lax.broadcasted_iota(jnp.int32, (tq, kb), 0)
    kpos = kpos0 + lax.broadcasted_iota(jnp.int32, (tq, kb), 1)
    before = kpos < qpos
    lf = jnp.where(before, -(jnp.maximum(z, 0.0) + jnp.log(1.0 + jnp.exp(-jnp.abs(z)))), 0.0)
    c = _split_dot(lf, tri_incl) + r_in
    a = jnp.where(before, jnp.exp(z + c), 0.0)
    return z, before, lf, a


def _sb_tris(kb):
    row = lax.broadcasted_iota(jnp.int32, (kb, kb), 0)
    col = lax.broadcasted_iota(jnp.int32, (kb, kb), 1)
    return jnp.where(row >= col, 1.0, 0.0).astype(BF16), jnp.where(row > col, 1.0, 0.0).astype(BF16)


def _sb_fwd(p):
    T = p.shape[0]
    tq = min(256, T)
    scale = SB_HEAD_DIM ** -0.5

    def body(q_ref, gt_ref, k_ref, v_ref, o_ref, y_ref, acc, r_ref):
        qi = pl.program_id(1)
        qs = (q_ref[...].astype(F32) * scale).astype(BF16)
        tri_incl, _ = _sb_tris(tq)
        acc[...] = jnp.zeros_like(acc)
        r_ref[...] = jnp.zeros_like(r_ref)

        def step(carry):
            j, _ = carry
            rows = pl.ds(pl.multiple_of(j * tq, tq), tq)
            _, _, lf, a = _sb_block(qs, k_ref[rows, :], qi * tq, j * tq, r_ref[...], tri_incl)
            acc[...] += _dot(a.astype(BF16), v_ref[rows, :])
            r_new = r_ref[...] + jnp.sum(lf, axis=-1, keepdims=True)
            r_ref[...] = r_new
            return j - 1, jnp.max(r_new) > SB_EXIT

        lax.while_loop(lambda c: jnp.logical_and(c[0] >= 0, c[1]), step, (qi, True))
        o = acc[...]
        o_ref[...] = o
        sg, _ = _silu_and_grad(gt_ref[...].astype(F32))
        y_ref[...] = (o * sg).astype(y_ref.dtype)

    col = lambda off: pl.BlockSpec((tq, 128), lambda h, i: (i, off // 128 + h))
    whole = lambda off: pl.BlockSpec((T, 128), lambda h, i: (0, off // 128 + h))
    out = pl.BlockSpec((tq, 128), lambda h, i: (i, h))
    return _call(
        body, name="sb_fwd", grid=(SB_HEADS, T // tq),
        in_specs=[col(N_QC), col(N_GC), whole(N_KC), whole(N_VC)],
        out_specs=[out, out],
        out_shape=[jax.ShapeDtypeStruct((T, 512), F32), jax.ShapeDtypeStruct((T, 512), BF16)],
        scratch=[pltpu.VMEM((tq, 128), F32), pltpu.VMEM((tq, 1), F32)],
    )(p, p, p, p)


def _sb_bwd(p, o, dy):
    T = p.shape[0]
    tq = min(256, T)
    nq = T // tq
    scale = SB_HEAD_DIM ** -0.5

    def body(q_ref, gt_ref, k_ref, v_ref, o_ref, dy_ref, dq_ref, dg_ref, dk_hbm, dv_hbm,
             dq_acc, r_ref, s_ref, dk_acc, dv_acc, sem):
        h, qi = pl.program_id(0), pl.program_id(1)

        @pl.when(qi == 0)
        def _():
            dk_acc[...] = jnp.zeros_like(dk_acc)
            dv_acc[...] = jnp.zeros_like(dv_acc)

        qs = (q_ref[...].astype(F32) * scale).astype(BF16)
        of = o_ref[...]
        dyf = dy_ref[...].astype(F32)
        sg, dsg = _silu_and_grad(gt_ref[...].astype(F32))
        do = dyf * sg
        dg_ref[...] = (dyf * of * dsg).astype(dg_ref.dtype)
        delta = jnp.sum(do * of, axis=-1, keepdims=True)
        dob = do.astype(BF16)
        tri_incl, tri_strict = _sb_tris(tq)
        dq_acc[...] = jnp.zeros_like(dq_acc)
        r_ref[...] = jnp.zeros_like(r_ref)
        s_ref[...] = jnp.zeros_like(s_ref)

        def step(carry):
            j, _ = carry
            rows = pl.ds(pl.multiple_of(j * tq, tq), tq)
            kj, vj = k_ref[rows, :], v_ref[rows, :]
            z, before, lf, a = _sb_block(qs, kj, qi * tq, j * tq, r_ref[...], tri_incl)
            gz = a * _dot_nt(dob, vj)
            later = _split_dot(gz, tri_strict) + s_ref[...]
            dz = jnp.where(before, gz - _sigmoid(z) * (delta - later), 0.0)
            dq_acc[...] += _dot(dz.astype(BF16), kj)
            dk_acc[rows, :] += _dot(dz.T.astype(BF16), qs)
            dv_acc[rows, :] += _dot(a.T.astype(BF16), dob)
            s_ref[...] += jnp.sum(gz, axis=-1, keepdims=True)
            r_new = r_ref[...] + jnp.sum(lf, axis=-1, keepdims=True)
            r_ref[...] = r_new
            return j - 1, jnp.max(r_new) > SB_EXIT

        lax.while_loop(lambda c: jnp.logical_and(c[0] >= 0, c[1]), step, (qi, True))
        dq_ref[...] = (dq_acc[...] * scale).astype(dq_ref.dtype)

        @pl.when(qi == nq - 1)
        def _():
            cols = pl.ds(pl.multiple_of(h * 128, 128), 128)
            ck = pltpu.make_async_copy(dk_acc, dk_hbm.at[:, cols], sem.at[0])
            cv = pltpu.make_async_copy(dv_acc, dv_hbm.at[:, cols], sem.at[1])
            ck.start()
            cv.start()
            ck.wait()
            cv.wait()

    col = lambda off: pl.BlockSpec((tq, 128), lambda h, i: (i, off // 128 + h))
    whole = lambda off: pl.BlockSpec((T, 128), lambda h, i: (0, off // 128 + h))
    blk = pl.BlockSpec((tq, 128), lambda h, i: (i, h))
    hbm = pl.BlockSpec(memory_space=pl.ANY)
    return _call(
        body, name="sb_bwd", grid=(SB_HEADS, nq),
        in_specs=[col(N_QC), col(N_GC), whole(N_KC), whole(N_VC), blk, blk],
        out_specs=[blk, blk, hbm, hbm],
        out_shape=[jax.ShapeDtypeStruct((T, 512), ACT), jax.ShapeDtypeStruct((T, 512), ACT),
                   jax.ShapeDtypeStruct((T, 512), F32), jax.ShapeDtypeStruct((T, 512), F32)],
        scratch=[pltpu.VMEM((tq, 128), F32), pltpu.VMEM((tq, 1), F32), pltpu.VMEM((tq, 1), F32),
                 pltpu.VMEM((T, 128), F32), pltpu.VMEM((T, 128), F32), pltpu.SemaphoreType.DMA((2,))],
    )(p, p, p, p, o, dy)


def _post_fwd(ya, yb, yc, p, bm, wa, wb, wc, wo, g_post, x):
    T = x.shape[0]
    tm = min(512, T)

    def body(ya_ref, yb_ref, yc_ref, la_ref, lb_ref, lc_ref, bm_ref, wa_ref, wb_ref, wc_ref, wo_ref,
             g_ref, x_ref, m_ref, out_ref, xn_ref):
        merged = None
        for k, (y_ref, l_ref, w_ref) in enumerate(
                ((ya_ref, la_ref, wa_ref), (yb_ref, lb_ref, wb_ref), (yc_ref, lc_ref, wc_ref))):
            gate = _sigmoid(l_ref[...].astype(F32) + bm_ref[k:k + 1, :])
            term = gate * _dot(y_ref[...], w_ref[...])
            merged = term if merged is None else merged + term
        mb = merged.astype(BF16)
        m_ref[...] = mb
        out = _dot(mb, wo_ref[...])
        out_ref[...] = out
        r = lax.rsqrt(jnp.mean(out * out, axis=-1, keepdims=True) + EPS)
        xn_ref[...] = x_ref[...] + out * r * g_ref[...]

    yspec = pl.BlockSpec((tm, 512), lambda i: (i, 0))
    lspec = lambda k: pl.BlockSpec((tm, D), lambda i: (i, k))
    row = pl.BlockSpec((tm, D), lambda i: (i, 0))
    return _call(
        body, name="post_fwd", grid=(T // tm,),
        in_specs=[yspec, yspec, yspec, lspec(0), lspec(1), lspec(2), _full((3, D)),
                  _full((512, D)), _full((512, D)), _full((512, D)), _full((D, D)), _full((1, D)), row],
        out_specs=[row, row, row],
        out_shape=[jax.ShapeDtypeStruct((T, D), BF16), jax.ShapeDtypeStruct((T, D), F32),
                   jax.ShapeDtypeStruct((T, D), F32)],
    )(ya, yb, yc, p, p, p, bm, wa, wb, wc, wo, g_post, x)


def _post_bwd(dxn, out, ya, yb, yc, p, bm, wa, wb, wc, wo, g_post):
    T = dxn.shape[0]
    tm = min(256, T)

    def body(dxn_ref, out_ref, ya_ref, yb_ref, yc_ref, la_ref, lb_ref, lc_ref, bm_ref,
             wa_ref, wb_ref, wc_ref, wo_ref, g_ref,
             do_ref, dpa_ref, dpb_ref, dpc_ref, dl_ref, dya_ref, dyb_ref, dyc_ref, dg_ref, db_ref):
        @pl.when(pl.program_id(0) == 0)
        def _():
            dg_ref[...] = jnp.zeros_like(dg_ref)
            db_ref[...] = jnp.zeros_like(db_ref)

        out = out_ref[...]
        dxn_ = dxn_ref[...]
        r = lax.rsqrt(jnp.mean(out * out, axis=-1, keepdims=True) + EPS)
        a = dxn_ * g_ref[...]
        d_out = r * a - out * (r * r * r) * jnp.mean(a * out, axis=-1, keepdims=True)
        dg_ref[...] += jnp.sum(dxn_ * out * r, axis=0, keepdims=True)
        dob = d_out.astype(BF16)
        do_ref[...] = dob
        dmerged = _dot_nt(dob, wo_ref[...])
        for k, (y_ref, l_ref, w_ref, dp_ref, dy_ref) in enumerate((
                (ya_ref, la_ref, wa_ref, dpa_ref, dya_ref), (yb_ref, lb_ref, wb_ref, dpb_ref, dyb_ref),
                (yc_ref, lc_ref, wc_ref, dpc_ref, dyc_ref))):
            gate = _sigmoid(l_ref[...].astype(F32) + bm_ref[k:k + 1, :])
            proj = _dot(y_ref[...], w_ref[...])
            dproj = (dmerged * gate).astype(BF16)
            dp_ref[...] = dproj
            dlog = dmerged * proj * gate * (1.0 - gate)
            dl_ref[:, D * k:D * (k + 1)] = dlog.astype(dl_ref.dtype)
            db_ref[k:k + 1, :] += jnp.sum(dlog, axis=0, keepdims=True)
            dy_ref[...] = _dot_nt(dproj, w_ref[...]).astype(dy_ref.dtype)

    yspec = pl.BlockSpec((tm, 512), lambda i: (i, 0))
    lspec = lambda k: pl.BlockSpec((tm, D), lambda i: (i, k))
    row = pl.BlockSpec((tm, D), lambda i: (i, 0))
    sds = jax.ShapeDtypeStruct
    return _call(
        body, name="post_bwd", grid=(T // tm,),
        in_specs=[row, row, yspec, yspec, yspec, lspec(0), lspec(1), lspec(2), _full((3, D)),
                  _full((512, D)), _full((512, D)), _full((512, D)), _full((D, D)), _full((1, D))],
        out_specs=[row, row, row, row, pl.BlockSpec((tm, 3 * D), lambda i: (i, 0)), yspec, yspec, yspec,
                   _full((1, D)), _full((3, D))],
        out_shape=[sds((T, D), BF16), sds((T, D), BF16), sds((T, D), BF16), sds((T, D), BF16),
                   sds((T, 3 * D), ACT), sds((T, 512), ACT), sds((T, 512), ACT), sds((T, 512), ACT),
                   sds((1, D), F32), sds((3, D), F32)],
    )(dxn, out, ya, yb, yc, p, p, p, bm, wa, wb, wc, wo, g_post)


def _loss_and_grad(y, target):
    T = y.shape[0]
    tm = min(1024, T)

    def body(y_ref, t_ref, l_ref, dy_ref):
        @pl.when(pl.program_id(0) == 0)
        def _():
            l_ref[...] = jnp.zeros_like(l_ref)
        e = y_ref[...] - t_ref[...]
        l_ref[...] += jnp.sum(e * e, axis=0, keepdims=True)
        dy_ref[...] = e * (1.0 / D)

    row = pl.BlockSpec((tm, D), lambda i: (i, 0))
    return _call(
        body, name="loss", grid=(T // tm,), in_specs=[row, row], out_specs=[_full((1, D)), row],
        out_shape=[jax.ShapeDtypeStruct((1, D), F32), jax.ShapeDtypeStruct((T, D), F32)],
    )(y, target)


_CHIP_STEPS = ((1, 0), (0, 1), (1, 1))


def _chip_exchange(src, per_target, name):
    shape = src.shape[-2:]

    def body(s_ref, r_ref, send_sems, recv_sems, local_sem):
        x, y, c = lax.axis_index("x"), lax.axis_index("y"), lax.axis_index("c")
        me = 2 * x + y
        pick = (lambda j: s_ref.at[j]) if per_target else (lambda j: s_ref)
        own = pltpu.make_async_copy(pick(me), r_ref.at[me], local_sem)
        own.start()
        copies = []
        for k, (dx, dy) in enumerate(_CHIP_STEPS):
            tx, ty = (x + dx) % 2, (y + dy) % 2
            peer = 2 * tx + ty
            cp = pltpu.make_async_remote_copy(
                src_ref=pick(peer), dst_ref=r_ref.at[me], send_sem=send_sems.at[k], recv_sem=recv_sems.at[k],
                device_id=(tx, ty, c), device_id_type=MESH)
            cp.start()
            copies.append((cp, peer))
        for k, (cp, peer) in enumerate(copies):
            pltpu.make_async_remote_copy(
                src_ref=pick(peer), dst_ref=r_ref.at[peer], send_sem=send_sems.at[k], recv_sem=recv_sems.at[k],
                device_id=(x, y, c), device_id_type=MESH).wait_recv()
        for cp, _ in copies:
            cp.wait_send()
        own.wait()

    hbm = pl.BlockSpec(memory_space=pl.ANY)
    return pl.pallas_call(
        body, name=name, in_specs=[hbm], out_specs=hbm,
        out_shape=jax.ShapeDtypeStruct((4,) + shape, src.dtype),
        scratch_shapes=[pltpu.SemaphoreType.DMA((3,)), pltpu.SemaphoreType.DMA((3,)), pltpu.SemaphoreType.DMA],
    )(src)


def _sibling_swap(src, name):
    def body(s_ref, r_ref, send_sem, recv_sem):
        x, y, c = lax.axis_index("x"), lax.axis_index("y"), lax.axis_index("c")
        cp = pltpu.make_async_remote_copy(
            src_ref=s_ref, dst_ref=r_ref, send_sem=send_sem, recv_sem=recv_sem,
            device_id=(x, y, 1 - c), device_id_type=MESH)
        cp.start()
        cp.wait()

    hbm = pl.BlockSpec(memory_space=pl.ANY)
    return pl.pallas_call(
        body, name=name, in_specs=[hbm], out_specs=hbm,
        out_shape=jax.ShapeDtypeStruct(src.shape, src.dtype),
        scratch_shapes=[pltpu.SemaphoreType.DMA, pltpu.SemaphoreType.DMA],
    )(src)


def _sum_chips(r):
    _, R, C = r.shape
    tr = 512

    def body(r_ref, o_ref):
        o_ref[...] = ((r_ref[0] + r_ref[1]) + r_ref[2]) + r_ref[3]

    return _call(
        body, name="sum_chips", grid=(R // tr,),
        in_specs=[pl.BlockSpec((4, tr, C), lambda i: (0, i, 0))],
        out_specs=pl.BlockSpec((tr, C), lambda i: (i, 0)),
        out_shape=jax.ShapeDtypeStruct((R, C), F32),
    )(r)


def _adamw(w, m, v, g_mine, g_other):
    R, C = w.shape
    tr = 512

    def body(w_ref, m_ref, v_ref, a_ref, b_ref, g_ref, d_ref, nm_ref, nv_ref):
        g = a_ref[...] + b_ref[...]
        g_ref[...] = g
        m_new = ADAM_B1 * m_ref[...] + (1.0 - ADAM_B1) * g
        v_new = ADAM_B2 * v_ref[...] + (1.0 - ADAM_B2) * (g * g)
        nm_ref[...] = m_new
        nv_ref[...] = v_new
        m_hat = m_new / (1.0 - ADAM_B1 ** ADAM_STEP)
        v_hat = v_new / (1.0 - ADAM_B2 ** ADAM_STEP)
        d_ref[...] = -ADAM_LR * (m_hat / (jnp.sqrt(v_hat) + ADAM_EPS) + ADAM_WD * w_ref[...])

    blk = pl.BlockSpec((tr, C), lambda i: (i, 0))
    sds = jax.ShapeDtypeStruct((R, C), F32)
    return _call(body, name="adamw", grid=(R // tr,), in_specs=[blk] * 5, out_specs=[blk] * 4,
                 out_shape=[sds] * 4)(w, m, v, g_mine, g_other)


_NAMES = ('w_in', 'gm_w_s', 'gm_b_s', 'gm_norm_gain', 'sw_sinks', 'w_branch_a', 'w_branch_b',
          'w_branch_c', 'b_merge', 'w_out', 'g_pre', 'g_post')
_PACK_COLS = 1024
_PACK_ROW_TILE = 512


def _pack(arrays, dtype):
    flat = jnp.concatenate([a.reshape(-1).astype(dtype) for a in arrays])
    rows = -(-flat.shape[0] // _PACK_COLS)
    rows = -(-rows // _PACK_ROW_TILE) * _PACK_ROW_TILE
    flat = jnp.pad(flat, (0, rows * _PACK_COLS - flat.shape[0]))
    return flat.reshape(rows, _PACK_COLS)


def _unpack(buf, shapes):
    flat = buf.reshape(-1)
    out, off = [], 0
    for s in shapes:
        n = int(np.prod(s))
        out.append(flat[off:off + n].reshape(s))
        off += n
    return out


def _permute_cols(w):
    parts = [w[..., o:o + n] for o, n in _PERM]
    parts.append(jnp.zeros(w.shape[:-1] + (IN_PAD - IN_WIDTH,), w.dtype))
    return jnp.concatenate(parts, axis=-1)


def _unpermute_cols(w):
    new_off, off = {}, 0
    for o, n in _PERM:
        new_off[o] = (off, n)
        off += n
    return jnp.concatenate([w[..., new_off[o][0]:new_off[o][0] + new_off[o][1]] for o in sorted(new_off)], axis=-1)


def _rope_tables(T):
    half = HEAD_DIM // 2
    freqs = ROPE_THETA ** (-jnp.arange(half, dtype=F32) / half)
    ang = jnp.arange(T).astype(F32)[:, None] * freqs[None, :]
    cos, sin = jnp.cos(ang), jnp.sin(ang)
    return jnp.tile(jnp.concatenate([cos, cos], axis=1), (1, 2)), jnp.tile(jnp.concatenate([-sin, sin], axis=1), (1, 2))


def _layer_fwd(x, lw, tables):
    gpre, gpost = lw['g_pre'][None, :], lw['g_post'][None, :]
    bfull = jnp.broadcast_to(lw['bs'][:, :, None], (GM_GROUPS, BLOCK, BLOCK))
    gv = lw['gv'][None, :]
    p, h = _in_proj(x, gpre, lw['w_in'])
    ya = _gmlp_fwd(p, lw['ws'], bfull, gv)
    qkr = _rope_fwd(p, *tables)
    yb = _swa_fwd(qkr, p, lw['sinks'])
    oc, yc = _sb_fwd(p)
    merged, out, xn = _post_fwd(ya, yb, yc, p, lw['bm'], lw['wa'], lw['wb'], lw['wc'], lw['wo'], gpost, x)
    return xn, (lw, p, h, ya, qkr, yb, oc, yc, merged, out, bfull, gv, gpre, gpost)


def _layer_bwd(x, saved, dxn, tables):
    lw, p, h, ya, qkr, yb, oc, yc, merged, out, bfull, gv, gpre, gpost = saved
    T = x.shape[0]
    g = {}
    (d_out, dpa, dpb, dpc, dlog, dya, dyb, dyc, dgpost, dbm) = _post_bwd(
        dxn, out, ya, yb, yc, p, lw['bm'], lw['wa'], lw['wb'], lw['wc'], lw['wo'], gpost)
    g['w_out'] = _matmul_tn(merged, d_out, "grad_w_out")
    g['w_branch_a'] = _matmul_tn(ya, dpa, "grad_w_a")
    g['w_branch_b'] = _matmul_tn(yb, dpb, "grad_w_b")
    g['w_branch_c'] = _matmul_tn(yc, dpc, "grad_w_c")
    g['g_post'] = dgpost[0]
    g['b_merge'] = dbm
    dp_a, dws, dbacc, dgv = _gmlp_bwd(p, dya, lw['ws'], bfull, gv)
    g['gm_w_s'] = dws
    g['gm_b_s'] = jnp.sum(dbacc.reshape(BLOCK, GM_GROUPS, BLOCK), axis=2).T
    g['gm_norm_gain'] = dgv[0]
    dqr, dkr, dvb, dgb, dsink = _swa_bwd(qkr, p, dyb, lw['sinks'])
    g['sw_sinks'] = dsink[:, 0]
    dqb, dkb = _rope_bwd(dqr, dkr, *tables)
    dqc, dgc, dkc, dvc = _sb_bwd(p, oc, dyc)
    dp = jnp.concatenate(
        [dlog, dp_a, dqb, dgb, dqc, dkc.astype(ACT), dvc.astype(ACT), dgc, dkb, dvb.astype(ACT),
         jnp.zeros((T, IN_PAD - IN_WIDTH), ACT)], axis=1)
    g['w_in'] = _unpermute_cols(_matmul_tn(h, dp, "grad_w_in")[:, :IN_WIDTH])
    dx, dgpre = _in_bwd(dp, lw['w_in'], x, dxn, gpre)
    g['g_pre'] = dgpre[0]
    return dx, g


def kernel(x, w_in, gm_w_s, gm_b_s, gm_norm_gain, sw_sinks, w_branch_a, w_branch_b, w_branch_c, b_merge, w_out, g_pre, g_post, loss_target, m_w_in, m_gm_w_s, m_gm_b_s, m_gm_norm_gain, m_sw_sinks, m_w_branch_a, m_w_branch_b, m_w_branch_c, m_b_merge, m_w_out, m_g_pre, m_g_post, v_w_in, v_gm_w_s, v_gm_b_s, v_gm_norm_gain, v_sw_sinks, v_w_branch_a, v_w_branch_b, v_w_branch_c, v_b_merge, v_w_out, v_g_pre, v_g_post):
    T = x.shape[1]
    weights = (w_in, gm_w_s, gm_b_s, gm_norm_gain, sw_sinks, w_branch_a, w_branch_b, w_branch_c, b_merge,
               w_out, g_pre, g_post)
    moments_m = (m_w_in, m_gm_w_s, m_gm_b_s, m_gm_norm_gain, m_sw_sinks, m_w_branch_a, m_w_branch_b,
                 m_w_branch_c, m_b_merge, m_w_out, m_g_pre, m_g_post)
    moments_v = (v_w_in, v_gm_w_s, v_gm_b_s, v_gm_norm_gain, v_sw_sinks, v_w_branch_a, v_w_branch_b,
                 v_w_branch_c, v_b_merge, v_w_out, v_g_pre, v_g_post)
    shapes = [w.shape for w in weights]

    bm_hi = b_merge.astype(BF16)
    bm_lo = (b_merge - bm_hi.astype(F32)).astype(BF16)
    shard_parts = (w_in, w_branch_a, w_branch_b, w_branch_c, bm_hi, bm_lo, w_out)
    gathered = _chip_exchange(_pack(shard_parts, BF16), False, "gather_weights")
    per_chip = [_unpack(gathered[j], [a.shape for a in shard_parts]) for j in range(4)]
    cat = lambda idx, axis: jnp.concatenate([per_chip[j][idx] for j in range(4)], axis=axis)
    w_in_full = _permute_cols(cat(0, 2))
    wa_full, wb_full, wc_full = cat(1, 2), cat(2, 2), cat(3, 2)
    bm_full = cat(4, 2).astype(F32) + cat(5, 2).astype(F32)
    wo_full = cat(6, 1)

    tables = _rope_tables(T)
    xs = [x[0]]
    saved = []
    for l in range(DEPTH):
        lw = dict(w_in=w_in_full[l], wa=wa_full[l], wb=wb_full[l], wc=wc_full[l], bm=bm_full[l],
                  wo=wo_full[l], ws=gm_w_s[l], bs=gm_b_s[l], gv=gm_norm_gain[l], sinks=sw_sinks[l],
                  g_pre=g_pre[l], g_post=g_post[l])
        xn, sv = _layer_fwd(xs[l], lw, tables)
        xs.append(xn)
        saved.append(sv)

    lsum, dxn = _loss_and_grad(xs[DEPTH], loss_target[0])
    loss = lax.psum(0.5 * jnp.sum(lsum) / D, ("x", "y", "c"))

    grads = {n: [None] * DEPTH for n in _NAMES}
    for l in reversed(range(DEPTH)):
        dxn, gl = _layer_bwd(xs[l], saved[l], dxn, tables)
        for n in _NAMES:
            grads[n][l] = gl[n]
    grad_x = dxn[None]

    full = {n: jnp.stack(grads[n]) for n in _NAMES}

    def partial_for(j):
        cols = lambda a, w: a[..., j * w:(j + 1) * w]
        return [cols(full['w_in'], IN_WIDTH // 4), full['gm_w_s'], full['gm_b_s'], full['gm_norm_gain'],
                full['sw_sinks'], cols(full['w_branch_a'], D // 4), cols(full['w_branch_b'], D // 4),
                cols(full['w_branch_c'], D // 4), cols(full['b_merge'], D // 4),
                full['w_out'][:, j * (D // 4):(j + 1) * (D // 4), :], full['g_pre'], full['g_post']]

    partials = jnp.stack([_pack(partial_for(j), F32) for j in range(4)])
    core_sum = _sum_chips(_chip_exchange(partials, True, "scatter_grads"))
    other_sum = _sibling_swap(core_sum, "swap_core_sums")
    g_buf, d_buf, m_buf, v_buf = _adamw(_pack(weights, F32), _pack(moments_m, F32), _pack(moments_v, F32),
                                        core_sum, other_sum)
    return (loss, grad_x, *_unpack(g_buf, shapes), *_unpack(d_buf, shapes), *_unpack(m_buf, shapes),
            *_unpack(v_buf, shapes))
```

```python
import functools

import numpy as np
import jax
import jax.numpy as jnp
from jax import lax
from jax.experimental import pallas as pl
from jax.experimental.pallas import tpu as pltpu

F32 = jnp.float32
BF16 = jnp.bfloat16
ACT = jnp.bfloat16

D = 1024
DEPTH = 4
BLOCK = 128
EPS = 1e-6
NEG = -1e30
GM_GROUPS = 4
GM_WIDTH = 512
HEAD_DIM = 64
SW_HEADS = 8
SB_HEADS = 4
SB_HEAD_DIM = 128
ROPE_THETA = 10000.0
IN_WIDTH = 7936
IN_PAD = 8192

O_UA, O_VA, O_GA, O_QB, O_KB, O_VB, O_GB = 0, 512, 1024, 1536, 2048, 2176, 2304
O_QC, O_KC, O_VC, O_GC, O_MG = 2816, 3328, 3840, 4352, 4864
_PERM = ((O_MG, 3072), (O_UA, 512), (O_VA, 512), (O_GA, 512),
         (O_QB, 512), (O_KB, 128), (O_VB, 128), (None, IN_PAD - IN_WIDTH), (O_GB, 512)) + tuple(
    (o + 128 * h, 128) for h in range(4) for o in (O_QC, O_GC, O_KC, O_VC))
N_MG, N_UA, N_VA, N_GA = 0, 3072, 3584, 4096
N_QB, N_KB, N_VB, N_GB = 4608, 5120, 5248, 5632
N_C = 6144

ADAM_LR, ADAM_B1, ADAM_B2, ADAM_EPS, ADAM_WD, ADAM_STEP = 0.001, 0.9, 0.999, 1e-08, 0.01, 10

SB_EXIT = -104.0
V7X_VMEM_LIMIT = 48 * 1024 * 1024
MESH = pl.DeviceIdType.MESH


_HBM = pl.BlockSpec(memory_space=pl.ANY)


def _call(body, *, name, grid, in_specs, out_specs, out_shape, scratch=(), aliases=None):
    return pl.pallas_call(
        body, name=name, grid=grid, in_specs=in_specs, out_specs=out_specs, out_shape=out_shape,
        scratch_shapes=list(scratch), input_output_aliases=aliases or {},
        compiler_params=pltpu.CompilerParams(
            dimension_semantics=("arbitrary",) * len(grid), vmem_limit_bytes=V7X_VMEM_LIMIT))


def _sigmoid(x):
    return 1.0 / (1.0 + jnp.exp(-x))


def _silu_and_grad(x):
    s = _sigmoid(x)
    return x * s, s * (1.0 + x * (1.0 - s))


def _dot(a, b):
    return jnp.dot(a, b, preferred_element_type=F32)


def _dot_nt(a, b):
    return lax.dot_general(a, b, (((1,), (1,)), ((), ())), preferred_element_type=F32)


def _dot_tn(a, b):
    return lax.dot_general(a, b, (((0,), (0,)), ((), ())), preferred_element_type=F32)


def _full(shape):
    return pl.BlockSpec(shape, lambda *_: (0,) * len(shape))


def _in_proj(x, g_pre, w_in):
    T = x.shape[0]
    tm, tn = min(2048, T), 512

    def body(x_ref, g_ref, w_ref, p_ref, h_ref):
        @pl.when(pl.program_id(1) == 0)
        def _():
            xf = x_ref[...]
            r = lax.rsqrt(jnp.mean(xf * xf, axis=-1, keepdims=True) + EPS)
            h_ref[...] = (xf * r * g_ref[...]).astype(BF16)
        p_ref[...] = _dot(h_ref[...], w_ref[...]).astype(p_ref.dtype)

    return _call(
        body, name="in_proj", grid=(T // tm, IN_PAD // tn),
        in_specs=[pl.BlockSpec((tm, D), lambda i, j: (i, 0)), _full((1, D)),
                  pl.BlockSpec((D, tn), lambda i, j: (0, j))],
        out_specs=[pl.BlockSpec((tm, tn), lambda i, j: (i, j)), pl.BlockSpec((tm, D), lambda i, j: (i, 0))],
        out_shape=[jax.ShapeDtypeStruct((T, IN_PAD), ACT), jax.ShapeDtypeStruct((T, D), BF16)],
    )(x, g_pre, w_in)


def _in_bwd(dp, w_in, x, dxn, g_pre):
    T = x.shape[0]
    tm, tk = min(1024, T), 1024
    nk = IN_PAD // tk

    def body(dp_ref, w_ref, x_ref, dxn_ref, g_ref, dx_ref, dg_ref, acc):
        i, k = pl.program_id(0), pl.program_id(1)

        @pl.when(k == 0)
        def _():
            acc[...] = jnp.zeros_like(acc)

        @pl.when((i == 0) & (k == 0))
        def _():
            dg_ref[...] = jnp.zeros_like(dg_ref)

        acc[...] += _dot_nt(dp_ref[...], w_ref[...])

        @pl.when(k == nk - 1)
        def _():
            dh = acc[...]
            xf = x_ref[...]
            r = lax.rsqrt(jnp.mean(xf * xf, axis=-1, keepdims=True) + EPS)
            a = dh * g_ref[...]
            dx_ref[...] = dxn_ref[...] + r * a - xf * (r * r * r) * jnp.mean(a * xf, axis=-1, keepdims=True)
            dg_ref[...] += jnp.sum(dh * xf * r, axis=0, keepdims=True)

    return _call(
        body, name="in_bwd", grid=(T // tm, nk),
        in_specs=[pl.BlockSpec((tm, tk), lambda i, k: (i, k)), pl.BlockSpec((D, tk), lambda i, k: (0, k)),
                  pl.BlockSpec((tm, D), lambda i, k: (i, 0)), pl.BlockSpec((tm, D), lambda i, k: (i, 0)),
                  _full((1, D))],
        out_specs=[pl.BlockSpec((tm, D), lambda i, k: (i, 0)), _full((1, D))],
        out_shape=[jax.ShapeDtypeStruct((T, D), F32), jax.ShapeDtypeStruct((1, D), F32)],
        scratch=[pltpu.VMEM((tm, D), F32)],
    )(dp, w_in, x, dxn, g_pre)


def _matmul_tn(a, b, name):
    T, K = a.shape
    N = b.shape[1]
    tk, tn, tt = min(K, 1024), min(N, 2048), min(T, 512)

    def body(a_ref, b_ref, o_ref):
        @pl.when(pl.program_id(2) == 0)
        def _():
            o_ref[...] = jnp.zeros_like(o_ref)
        o_ref[...] += _dot_tn(a_ref[...], b_ref[...])

    return _call(
        body, name=name, grid=(K // tk, N // tn, T // tt),
        in_specs=[pl.BlockSpec((tt, tk), lambda i, j, t: (t, i)), pl.BlockSpec((tt, tn), lambda i, j, t: (t, j))],
        out_specs=pl.BlockSpec((tk, tn), lambda i, j, t: (i, j)),
        out_shape=jax.ShapeDtypeStruct((K, N), F32),
    )(a, b)


def _gm_forward_parts(v_ref, gv_ref, ws_ref, bf_ref, nch):
    vf = v_ref[...].astype(F32)
    mu = jnp.mean(vf, axis=-1, keepdims=True)
    xc = vf - mu
    rstd = lax.rsqrt(jnp.mean(xc * xc, axis=-1, keepdims=True) + EPS)
    xhat = xc * rstd
    vnb = (xhat * gv_ref[...]).astype(BF16)
    row = lax.broadcasted_iota(jnp.int32, (BLOCK, BLOCK), 0)
    col = lax.broadcasted_iota(jnp.int32, (BLOCK, BLOCK), 1)
    mixed, vcats, wgs = [], [], []
    for g in range(GM_GROUPS):
        vg = vnb[:, BLOCK * g:BLOCK * (g + 1)]
        vcat = jnp.concatenate([vg[BLOCK * k:BLOCK * (k + 1), :] for k in range(nch)], axis=1)
        wg = jnp.where(row >= col, ws_ref[g], 0.0)
        m = _dot(wg.astype(BF16), vcat)
        mixed.append(jnp.concatenate(
            [m[:, BLOCK * k:BLOCK * (k + 1)] + bf_ref[g] for k in range(nch)], axis=0))
        vcats.append(vcat)
        wgs.append(wg)
    return xhat, rstd, jnp.concatenate(mixed, axis=1), vcats, wgs, row >= col


def _gmlp_fwd(p, ws, bfull, gv):
    T = p.shape[0]
    tm = min(512, T)
    nch = tm // BLOCK

    def body(u_ref, v_ref, gt_ref, ws_ref, bf_ref, gv_ref, y_ref):
        _, _, mixed, _, _, _ = _gm_forward_parts(v_ref, gv_ref, ws_ref, bf_ref, nch)
        sg, _ = _silu_and_grad(gt_ref[...].astype(F32))
        y_ref[...] = (u_ref[...].astype(F32) * mixed * sg).astype(y_ref.dtype)

    seg = lambda off: pl.BlockSpec((tm, 512), lambda i: (i, off // 512))
    return _call(
        body, name="gmlp_fwd", grid=(T // tm,),
        in_specs=[seg(N_UA), seg(N_VA), seg(N_GA), _full((GM_GROUPS, BLOCK, BLOCK)),
                  _full((GM_GROUPS, BLOCK, BLOCK)), _full((1, GM_WIDTH))],
        out_specs=pl.BlockSpec((tm, 512), lambda i: (i, 0)),
        out_shape=jax.ShapeDtypeStruct((T, GM_WIDTH), BF16),
    )(p, p, p, ws, bfull, gv)


def _gmlp_bwd(p, dy, ws, bfull, gv, dp):
    T = p.shape[0]
    tm = min(512, T)
    nch = tm // BLOCK

    def body(u_ref, v_ref, gt_ref, dy_ref, ws_ref, bf_ref, gv_ref, _, dp_ref, dws_ref, db_ref, dgv_ref):
        @pl.when(pl.program_id(0) == 0)
        def _():
            dws_ref[...] = jnp.zeros_like(dws_ref)
            db_ref[...] = jnp.zeros_like(db_ref)
            dgv_ref[...] = jnp.zeros_like(dgv_ref)

        xhat, rstd, mixed, vcats, wgs, tril = _gm_forward_parts(v_ref, gv_ref, ws_ref, bf_ref, nch)
        u = u_ref[...].astype(F32)
        gt = gt_ref[...].astype(F32)
        dyf = dy_ref[...].astype(F32)
        sg, dsg = _silu_and_grad(gt)
        du = dyf * mixed * sg
        dmixed = dyf * u * sg
        dgate = dyf * (u * mixed) * dsg
        dvn = []
        for g in range(GM_GROUPS):
            dmg = dmixed[:, BLOCK * g:BLOCK * (g + 1)]
            chunks = [dmg[BLOCK * k:BLOCK * (k + 1), :] for k in range(nch)]
            dmcat = jnp.concatenate(chunks, axis=1).astype(BF16)
            dws_ref[g] += jnp.where(tril, _dot_nt(dmcat, vcats[g]), 0.0)
            dvcat = _dot(wgs[g].T.astype(BF16), dmcat)
            dvn.append(jnp.concatenate([dvcat[:, BLOCK * k:BLOCK * (k + 1)] for k in range(nch)], axis=0))
            db_ref[:, BLOCK * g:BLOCK * (g + 1)] += functools.reduce(lambda a, b: a + b, chunks)
        dvn = jnp.concatenate(dvn, axis=1)
        dgv_ref[...] += jnp.sum(dvn * xhat, axis=0, keepdims=True)
        dxh = dvn * gv_ref[...]
        dv = rstd * (dxh - jnp.mean(dxh, axis=-1, keepdims=True)
                     - xhat * jnp.mean(dxh * xhat, axis=-1, keepdims=True))
        dp_ref[:, 0:512] = du.astype(dp_ref.dtype)
        dp_ref[:, 512:1024] = dv.astype(dp_ref.dtype)
        dp_ref[:, 1024:1536] = dgate.astype(dp_ref.dtype)

    seg = lambda off: pl.BlockSpec((tm, 512), lambda i: (i, off // 512))
    return _call(
        body, name="gmlp_bwd", grid=(T // tm,),
        in_specs=[seg(N_UA), seg(N_VA), seg(N_GA), pl.BlockSpec((tm, 512), lambda i: (i, 0)),
                  _full((GM_GROUPS, BLOCK, BLOCK)), _full((GM_GROUPS, BLOCK, BLOCK)), _full((1, GM_WIDTH)), _HBM],
        out_specs=[pl.BlockSpec((tm, 1536), lambda i: (i, N_UA // 1536)), _full((GM_GROUPS, BLOCK, BLOCK)),
                   _full((BLOCK, GM_WIDTH)), _full((1, GM_WIDTH))],
        out_shape=[jax.ShapeDtypeStruct(dp.shape, dp.dtype), jax.ShapeDtypeStruct((GM_GROUPS, BLOCK, BLOCK), F32),
                   jax.ShapeDtypeStruct((BLOCK, GM_WIDTH), F32), jax.ShapeDtypeStruct((1, GM_WIDTH), F32)],
        aliases={7: 0},
    )(p, p, p, dy, ws, bfull, gv, dp)


def _swap_halves(x):
    lane = lax.broadcasted_iota(jnp.int32, x.shape, 1) % HEAD_DIM
    return jnp.where(lane < HEAD_DIM // 2, pltpu.roll(x, 96, 1), pltpu.roll(x, 32, 1))


def _rope_fwd(p, cos_t, sin_t):
    T = p.shape[0]
    tm = min(512, T)

    def body(q_ref, k_ref, c_ref, s_ref, o_ref):
        c, s = c_ref[...], s_ref[...]
        for G in range(5):
            xg = (q_ref[:, 128 * G:128 * (G + 1)] if G < 4 else k_ref[...]).astype(F32)
            o_ref[:, 128 * G:128 * (G + 1)] = (xg * c + _swap_halves(xg) * s).astype(o_ref.dtype)

    return _call(
        body, name="rope_fwd", grid=(T // tm,),
        in_specs=[pl.BlockSpec((tm, 512), lambda i: (i, N_QB // 512)),
                  pl.BlockSpec((tm, 128), lambda i: (i, N_KB // 128)),
                  pl.BlockSpec((tm, 128), lambda i: (i, 0)), pl.BlockSpec((tm, 128), lambda i: (i, 0))],
        out_specs=pl.BlockSpec((tm, 640), lambda i: (i, 0)),
        out_shape=jax.ShapeDtypeStruct((T, 640), BF16),
    )(p, p, cos_t, sin_t)


def _unrotate(d, c, s):
    return d * c + _swap_halves(d * s)


def _dup_heads(x):
    left = lax.broadcasted_iota(jnp.int32, x.shape, 1) < HEAD_DIM
    r = pltpu.roll(x, HEAD_DIM, 1)
    return jnp.where(left, x, r).astype(BF16), jnp.where(left, r, x).astype(BF16)


def _fold_heads(acc0, acc1):
    left = lax.broadcasted_iota(jnp.int32, acc0.shape, 1) < HEAD_DIM
    t0 = acc0 + pltpu.roll(acc0, HEAD_DIM, 1)
    t1 = acc1 + pltpu.roll(acc1, HEAD_DIM, 1)
    return jnp.where(left, t0, t1)


def _swa_valid(base):
    qpos = base + lax.broadcasted_iota(jnp.int32, (BLOCK, 2 * BLOCK), 0)
    kpos = base - BLOCK + lax.broadcasted_iota(jnp.int32, (BLOCK, 2 * BLOCK), 1)
    return jnp.logical_and(kpos >= 0, jnp.logical_and(kpos <= qpos, kpos > qpos - BLOCK))


def _swa_probs(qm, kk, valid, sink):
    s = jnp.where(valid, _dot_nt(qm, kk) * (HEAD_DIM ** -0.5), NEG)
    m = jnp.maximum(jnp.max(s, axis=-1, keepdims=True), sink)
    e = jnp.exp(s - m)
    es = jnp.exp(sink - m)
    inv = 1.0 / (jnp.sum(e, axis=-1, keepdims=True) + es)
    return e * inv, es * inv


def _swa_specs(T, bq, rev):
    n = T // bq
    blk = (lambda i: n - 1 - i) if rev else (lambda i: i)
    halo = lambda i: jnp.maximum(blk(i) * (bq // BLOCK) - 1, 0)
    return blk, [
        pl.BlockSpec(memory_space=pltpu.SMEM),
        pl.BlockSpec((bq, 512), lambda i: (blk(i), 0)),
        pl.BlockSpec((bq, 128), lambda i: (blk(i), 4)),
        pl.BlockSpec((BLOCK, 128), lambda i: (halo(i), 4)),
        pl.BlockSpec((bq, 128), lambda i: (blk(i), N_VB // 128)),
        pl.BlockSpec((BLOCK, 128), lambda i: (halo(i), N_VB // 128)),
        pl.BlockSpec((bq, 512), lambda i: (blk(i), N_GB // 512)),
    ]


def _swa_fwd(qkr, p, sinks):
    T = p.shape[0]
    bq = min(512, T)
    nsb = bq // BLOCK
    blk, specs = _swa_specs(T, bq, False)

    def body(sink_ref, q_ref, kc_ref, kh_ref, vc_ref, vh_ref, gt_ref, y_ref):
        base = blk(pl.program_id(0)) * bq
        kd = _dup_heads(jnp.concatenate([kh_ref[...], kc_ref[...]], axis=0).astype(F32))
        vd = _dup_heads(jnp.concatenate([vh_ref[...], vc_ref[...]], axis=0).astype(F32))
        left = lax.broadcasted_iota(jnp.int32, (BLOCK, BLOCK), 1) < HEAD_DIM
        for sb in range(nsb):
            r0 = sb * BLOCK
            valid = _swa_valid(base + r0)
            for G in range(4):
                kk, vv = kd[G // 2][r0:r0 + 2 * BLOCK], vd[G // 2][r0:r0 + 2 * BLOCK]
                q128 = q_ref[r0:r0 + BLOCK, 128 * G:128 * (G + 1)]
                outs = []
                for hh in range(2):
                    qm = jnp.where(left if hh == 0 else jnp.logical_not(left), q128, jnp.zeros_like(q128))
                    pr, _ = _swa_probs(qm, kk, valid, sink_ref[2 * G + hh])
                    outs.append(_dot(pr.astype(BF16), vv))
                o = jnp.where(left, outs[0], outs[1])
                sg, _ = _silu_and_grad(gt_ref[r0:r0 + BLOCK, 128 * G:128 * (G + 1)].astype(F32))
                y_ref[r0:r0 + BLOCK, 128 * G:128 * (G + 1)] = (o * sg).astype(y_ref.dtype)

    return _call(
        body, name="swa_fwd", grid=(T // bq,), in_specs=specs,
        out_specs=pl.BlockSpec((bq, 512), lambda i: (i, 0)),
        out_shape=jax.ShapeDtypeStruct((T, 512), BF16),
    )(sinks, qkr, qkr, qkr, p, p, p)


def _swa_bwd(qkr, p, dy, sinks, cos_t, sin_t, dp):
    T = p.shape[0]
    bq = min(512, T)
    nsb = bq // BLOCK
    blk, specs = _swa_specs(T, bq, True)

    def body(sink_ref, q_ref, kc_ref, kh_ref, vc_ref, vh_ref, gt_ref, dy_ref, c_ref, s_ref, _,
             dp_ref, ds_ref, dk_acc, dv_acc, k_carry, v_carry):
        @pl.when(pl.program_id(0) == 0)
        def _():
            k_carry[...] = jnp.zeros_like(k_carry)
            v_carry[...] = jnp.zeros_like(v_carry)
            ds_ref[...] = jnp.zeros_like(ds_ref)

        base = blk(pl.program_id(0)) * bq
        dk_acc[...] = jnp.zeros_like(dk_acc)
        dv_acc[...] = jnp.zeros_like(dv_acc)
        kd = _dup_heads(jnp.concatenate([kh_ref[...], kc_ref[...]], axis=0).astype(F32))
        vd = _dup_heads(jnp.concatenate([vh_ref[...], vc_ref[...]], axis=0).astype(F32))
        left = lax.broadcasted_iota(jnp.int32, (BLOCK, BLOCK), 1) < HEAD_DIM
        for sb in range(nsb):
            r0 = sb * BLOCK
            rows = slice(r0, r0 + BLOCK)
            valid = _swa_valid(base + r0)
            dkp = [jnp.zeros((2 * BLOCK, BLOCK), F32)] * 2
            dvp = [jnp.zeros((2 * BLOCK, BLOCK), F32)] * 2
            for G in range(4):
                g = G // 2
                kk, vv = kd[g][r0:r0 + 2 * BLOCK], vd[g][r0:r0 + 2 * BLOCK]
                cols = slice(128 * G, 128 * (G + 1))
                q128 = q_ref[rows, cols]
                qms, prs, pss, outs = [], [], [], []
                for hh in range(2):
                    qm = jnp.where(left if hh == 0 else jnp.logical_not(left), q128, jnp.zeros_like(q128))
                    pr, ps = _swa_probs(qm, kk, valid, sink_ref[2 * G + hh])
                    qms.append(qm)
                    prs.append(pr)
                    pss.append(ps)
                    outs.append(_dot(pr.astype(BF16), vv))
                o = jnp.where(left, outs[0], outs[1])
                sg, dsg = _silu_and_grad(gt_ref[rows, cols].astype(F32))
                dyf = dy_ref[rows, cols].astype(F32)
                do = dyf * sg
                dp_ref[rows, 1024 + 128 * G:1024 + 128 * (G + 1)] = (dyf * o * dsg).astype(dp_ref.dtype)
                dqs = []
                for hh in range(2):
                    dom = jnp.where(left if hh == 0 else jnp.logical_not(left), do, 0.0).astype(BF16)
                    dpv = _dot_nt(dom, vv)
                    delta = jnp.sum(prs[hh] * dpv, axis=-1, keepdims=True)
                    dsc = prs[hh] * (dpv - delta) * (HEAD_DIM ** -0.5)
                    h = 2 * G + hh
                    ds_ref[h:h + 1, :] += jnp.broadcast_to(
                        -jnp.sum(pss[hh] * delta, axis=0, keepdims=True), (1, 128))
                    dqs.append(_dot(dsc.astype(BF16), kk))
                    dkp[g] = dkp[g] + _dot(dsc.T.astype(BF16), qms[hh])
                    dvp[g] = dvp[g] + _dot(prs[hh].T.astype(BF16), dom)
                dq = _unrotate(jnp.where(left, dqs[0], dqs[1]), c_ref[rows, :], s_ref[rows, :])
                dp_ref[rows, cols] = dq.astype(dp_ref.dtype)
            dk_acc[r0:r0 + 2 * BLOCK, :] += _fold_heads(dkp[0], dkp[1])
            dv_acc[r0:r0 + 2 * BLOCK, :] += _fold_heads(dvp[0], dvp[1])
        dk_acc[bq:bq + BLOCK, :] += k_carry[...]
        dv_acc[bq:bq + BLOCK, :] += v_carry[...]
        dk = _unrotate(dk_acc[BLOCK:bq + BLOCK, :], c_ref[...], s_ref[...])
        dp_ref[:, 512:640] = dk.astype(dp_ref.dtype)
        dp_ref[:, 640:768] = dv_acc[BLOCK:bq + BLOCK, :].astype(dp_ref.dtype)
        dp_ref[:, 768:1024] = jnp.zeros((bq, 256), dp_ref.dtype)
        k_carry[...] = dk_acc[0:BLOCK, :]
        v_carry[...] = dv_acc[0:BLOCK, :]

    rowblk = lambda w: pl.BlockSpec((bq, w), lambda i: (blk(i), 0))
    return _call(
        body, name="swa_bwd", grid=(T // bq,), in_specs=specs + [rowblk(512), rowblk(128), rowblk(128), _HBM],
        out_specs=[pl.BlockSpec((bq, 1536), lambda i: (blk(i), N_QB // 1536)), _full((SW_HEADS, 128))],
        out_shape=[jax.ShapeDtypeStruct(dp.shape, dp.dtype), jax.ShapeDtypeStruct((SW_HEADS, 128), F32)],
        scratch=[pltpu.VMEM((bq + BLOCK, 128), F32), pltpu.VMEM((bq + BLOCK, 128), F32),
                 pltpu.VMEM((BLOCK, 128), F32), pltpu.VMEM((BLOCK, 128), F32)],
        aliases={10: 0},
    )(sinks, qkr, qkr, qkr, p, p, p, dy, cos_t, sin_t, dp)


def _split_dot(x, tri):
    hi = x.astype(BF16)
    lo = (x - hi.astype(F32)).astype(BF16)
    return _dot(hi, tri) + _dot(lo, tri)


def _sb_block(qs, kj, qpos0, kpos0, r_in, tri_incl):
    tq, kb = qs.shape[0], kj.shape[0]
    z = _dot_nt(qs, kj)
    qpos = qpos0 + lax.broadcasted_iota(jnp.int32, (tq, kb), 0)
    kpos = kpos0 + lax.broadcasted_iota(jnp.int32, (tq, kb), 1)
    before = kpos < qpos
    e = jnp.exp(-jnp.abs(z))
    lf = jnp.where(before, -(jnp.maximum(z, 0.0) + jnp.log(1.0 + e)), 0.0)
    c = _split_dot(lf, tri_incl) + r_in
    a = jnp.where(before, jnp.exp(z + c), 0.0)
    return z, e, before, lf, a


def _sb_tris(kb):
    row = lax.broadcasted_iota(jnp.int32, (kb, kb), 0)
    col = lax.broadcasted_iota(jnp.int32, (kb, kb), 1)
    return jnp.where(row >= col, 1.0, 0.0).astype(BF16), jnp.where(row > col, 1.0, 0.0).astype(BF16)


def _sb_fwd(p):
    T = p.shape[0]
    tq = min(256, T)
    scale = SB_HEAD_DIM ** -0.5

    def body(q_ref, gt_ref, k_ref, v_ref, o_ref, y_ref, acc, r_ref):
        qi = pl.program_id(1)
        qs = (q_ref[...].astype(F32) * scale).astype(BF16)
        tri_incl, _ = _sb_tris(tq)
        acc[...] = jnp.zeros_like(acc)
        r_ref[...] = jnp.zeros_like(r_ref)

        def step(carry):
            j, _ = carry
            rows = pl.ds(pl.multiple_of(j * tq, tq), tq)
            _, _, _, lf, a = _sb_block(qs, k_ref[rows, :], qi * tq, j * tq, r_ref[...], tri_incl)
            acc[...] += _dot(a.astype(BF16), v_ref[rows, :])
            r_new = r_ref[...] + jnp.sum(lf, axis=-1, keepdims=True)
            r_ref[...] = r_new
            return j - 1, jnp.max(r_new) > SB_EXIT

        lax.while_loop(lambda c: jnp.logical_and(c[0] >= 0, c[1]), step, (qi, True))
        o = acc[...]
        o_ref[...] = o
        sg, _ = _silu_and_grad(gt_ref[...].astype(F32))
        y_ref[...] = (o * sg).astype(y_ref.dtype)

    col = lambda k: pl.BlockSpec((tq, 128), lambda h, i: (i, N_C // 128 + 4 * h + k))
    whole = lambda k: pl.BlockSpec((T, 128), lambda h, i: (0, N_C // 128 + 4 * h + k))
    out = pl.BlockSpec((tq, 128), lambda h, i: (i, h))
    return _call(
        body, name="sb_fwd", grid=(SB_HEADS, T // tq),
        in_specs=[col(0), col(1), whole(2), whole(3)],
        out_specs=[out, out],
        out_shape=[jax.ShapeDtypeStruct((T, 512), F32), jax.ShapeDtypeStruct((T, 512), BF16)],
        scratch=[pltpu.VMEM((tq, 128), F32), pltpu.VMEM((tq, 1), F32)],
    )(p, p, p, p)


def _sb_bwd(p, o, dy, dp):
    T = p.shape[0]
    tq = min(256, T)
    nq = T // tq
    scale = SB_HEAD_DIM ** -0.5

    def body(q_ref, gt_ref, k_ref, v_ref, o_ref, dy_ref, _, dqg_ref, dk_hbm, dv_hbm,
             dq_acc, r_ref, s_ref, dk_acc, dv_acc, sem):
        h, qi = pl.program_id(0), pl.program_id(1)

        @pl.when(qi == 0)
        def _():
            dk_acc[...] = jnp.zeros_like(dk_acc)
            dv_acc[...] = jnp.zeros_like(dv_acc)

        qs = (q_ref[...].astype(F32) * scale).astype(BF16)
        of = o_ref[...]
        dyf = dy_ref[...].astype(F32)
        sg, dsg = _silu_and_grad(gt_ref[...].astype(F32))
        do = dyf * sg
        dqg_ref[:, 128:256] = (dyf * of * dsg).astype(dqg_ref.dtype)
        delta = jnp.sum(do * of, axis=-1, keepdims=True)
        dob = do.astype(BF16)
        tri_incl, tri_strict = _sb_tris(tq)
        dq_acc[...] = jnp.zeros_like(dq_acc)
        r_ref[...] = jnp.zeros_like(r_ref)
        s_ref[...] = jnp.zeros_like(s_ref)

        def step(carry):
            j, _ = carry
            rows = pl.ds(pl.multiple_of(j * tq, tq), tq)
            kj, vj = k_ref[rows, :], v_ref[rows, :]
            z, e, before, lf, a = _sb_block(qs, kj, qi * tq, j * tq, r_ref[...], tri_incl)
            gz = a * _dot_nt(dob, vj)
            later = _split_dot(gz, tri_strict) + s_ref[...]
            sig = jnp.where(z >= 0.0, 1.0, e) / (1.0 + e)
            dz = jnp.where(before, gz - sig * (delta - later), 0.0)
            dq_acc[...] += _dot(dz.astype(BF16), kj)
            dk_acc[rows, :] += _dot(dz.T.astype(BF16), qs)
            dv_acc[rows, :] += _dot(a.T.astype(BF16), dob)
            s_ref[...] += jnp.sum(gz, axis=-1, keepdims=True)
            r_new = r_ref[...] + jnp.sum(lf, axis=-1, keepdims=True)
            r_ref[...] = r_new
            return j - 1, jnp.max(r_new) > SB_EXIT

        lax.while_loop(lambda c: jnp.logical_and(c[0] >= 0, c[1]), step, (qi, True))
        dqg_ref[:, 0:128] = (dq_acc[...] * scale).astype(dqg_ref.dtype)

        @pl.when(qi == nq - 1)
        def _():
            cols = pl.ds(pl.multiple_of(h * 128, 128), 128)
            ck = pltpu.make_async_copy(dk_acc, dk_hbm.at[:, cols], sem.at[0])
            cv = pltpu.make_async_copy(dv_acc, dv_hbm.at[:, cols], sem.at[1])
            ck.start()
            cv.start()
            ck.wait()
            cv.wait()

    col = lambda k: pl.BlockSpec((tq, 128), lambda h, i: (i, N_C // 128 + 4 * h + k))
    whole = lambda k: pl.BlockSpec((T, 128), lambda h, i: (0, N_C // 128 + 4 * h + k))
    blk = pl.BlockSpec((tq, 128), lambda h, i: (i, h))
    return _call(
        body, name="sb_bwd", grid=(SB_HEADS, nq),
        in_specs=[col(0), col(1), whole(2), whole(3), blk, blk, _HBM],
        out_specs=[pl.BlockSpec((tq, 256), lambda h, i: (i, N_C // 256 + 2 * h)), _HBM, _HBM],
        out_shape=[jax.ShapeDtypeStruct(dp.shape, dp.dtype),
                   jax.ShapeDtypeStruct((T, 512), F32), jax.ShapeDtypeStruct((T, 512), F32)],
        scratch=[pltpu.VMEM((tq, 128), F32), pltpu.VMEM((tq, 1), F32), pltpu.VMEM((tq, 1), F32),
                 pltpu.VMEM((T, 128), F32), pltpu.VMEM((T, 128), F32), pltpu.SemaphoreType.DMA((2,))],
        aliases={6: 0},
    )(p, p, p, p, o, dy, dp)


def _sb_kv_into_dp(dk, dv, dp):
    T = dk.shape[0]
    tm = min(1024, T)

    def body(dk_ref, dv_ref, _, o_ref):
        o_ref[:, 0:128] = dk_ref[...].astype(o_ref.dtype)
        o_ref[:, 128:256] = dv_ref[...].astype(o_ref.dtype)

    blk = pl.BlockSpec((tm, 128), lambda i, h: (i, h))
    return _call(
        body, name="sb_kv_into_dp", grid=(T // tm, SB_HEADS), in_specs=[blk, blk, _HBM],
        out_specs=pl.BlockSpec((tm, 256), lambda i, h: (i, N_C // 256 + 2 * h + 1)),
        out_shape=jax.ShapeDtypeStruct(dp.shape, dp.dtype), aliases={2: 0},
    )(dk, dv, dp)


def _post_fwd(ya, yb, yc, p, bm, wa, wb, wc, wo, g_post, x):
    T = x.shape[0]
    tm = min(512, T)

    def body(ya_ref, yb_ref, yc_ref, la_ref, lb_ref, lc_ref, bm_ref, wa_ref, wb_ref, wc_ref, wo_ref,
             g_ref, x_ref, m_ref, out_ref, xn_ref):
        merged = None
        for k, (y_ref, l_ref, w_ref) in enumerate(
                ((ya_ref, la_ref, wa_ref), (yb_ref, lb_ref, wb_ref), (yc_ref, lc_ref, wc_ref))):
            gate = _sigmoid(l_ref[...].astype(F32) + bm_ref[k:k + 1, :])
            term = gate * _dot(y_ref[...], w_ref[...])
            merged = term if merged is None else merged + term
        mb = merged.astype(BF16)
        m_ref[...] = mb
        out = _dot(mb, wo_ref[...])
        out_ref[...] = out
        r = lax.rsqrt(jnp.mean(out * out, axis=-1, keepdims=True) + EPS)
        xn_ref[...] = x_ref[...] + out * r * g_ref[...]

    yspec = pl.BlockSpec((tm, 512), lambda i: (i, 0))
    lspec = lambda k: pl.BlockSpec((tm, D), lambda i: (i, k))
    row = pl.BlockSpec((tm, D), lambda i: (i, 0))
    return _call(
        body, name="post_fwd", grid=(T // tm,),
        in_specs=[yspec, yspec, yspec, lspec(0), lspec(1), lspec(2), _full((3, D)),
                  _full((512, D)), _full((512, D)), _full((512, D)), _full((D, D)), _full((1, D)), row],
        out_specs=[row, row, row],
        out_shape=[jax.ShapeDtypeStruct((T, D), BF16), jax.ShapeDtypeStruct((T, D), F32),
                   jax.ShapeDtypeStruct((T, D), F32)],
    )(ya, yb, yc, p, p, p, bm, wa, wb, wc, wo, g_post, x)


def _post_bwd(dxn, out, ya, yb, yc, p, bm, wa, wb, wc, wo, g_post):
    T = dxn.shape[0]
    tm = min(256, T)

    def body(dxn_ref, out_ref, ya_ref, yb_ref, yc_ref, la_ref, lb_ref, lc_ref, bm_ref,
             wa_ref, wb_ref, wc_ref, wo_ref, g_ref,
             do_ref, dpa_ref, dpb_ref, dpc_ref, dl_ref, dya_ref, dyb_ref, dyc_ref, dg_ref, db_ref):
        @pl.when(pl.program_id(0) == 0)
        def _():
            dg_ref[...] = jnp.zeros_like(dg_ref)
            db_ref[...] = jnp.zeros_like(db_ref)

        out = out_ref[...]
        dxn_ = dxn_ref[...]
        r = lax.rsqrt(jnp.mean(out * out, axis=-1, keepdims=True) + EPS)
        a = dxn_ * g_ref[...]
        d_out = r * a - out * (r * r * r) * jnp.mean(a * out, axis=-1, keepdims=True)
        dg_ref[...] += jnp.sum(dxn_ * out * r, axis=0, keepdims=True)
        dob = d_out.astype(BF16)
        do_ref[...] = dob
        dmerged = _dot_nt(dob, wo_ref[...])
        for k, (y_ref, l_ref, w_ref, dp_ref, dy_ref) in enumerate((
                (ya_ref, la_ref, wa_ref, dpa_ref, dya_ref), (yb_ref, lb_ref, wb_ref, dpb_ref, dyb_ref),
                (yc_ref, lc_ref, wc_ref, dpc_ref, dyc_ref))):
            gate = _sigmoid(l_ref[...].astype(F32) + bm_ref[k:k + 1, :])
            proj = _dot(y_ref[...], w_ref[...])
            dproj = (dmerged * gate).astype(BF16)
            dp_ref[...] = dproj
            dlog = dmerged * proj * gate * (1.0 - gate)
            dl_ref[:, D * k:D * (k + 1)] = dlog.astype(dl_ref.dtype)
            db_ref[k:k + 1, :] += jnp.sum(dlog, axis=0, keepdims=True)
            dy_ref[...] = _dot_nt(dproj, w_ref[...]).astype(dy_ref.dtype)

    yspec = pl.BlockSpec((tm, 512), lambda i: (i, 0))
    lspec = lambda k: pl.BlockSpec((tm, D), lambda i: (i, k))
    row = pl.BlockSpec((tm, D), lambda i: (i, 0))
    sds = jax.ShapeDtypeStruct
    return _call(
        body, name="post_bwd", grid=(T // tm,),
        in_specs=[row, row, yspec, yspec, yspec, lspec(0), lspec(1), lspec(2), _full((3, D)),
                  _full((512, D)), _full((512, D)), _full((512, D)), _full((D, D)), _full((1, D))],
        out_specs=[row, row, row, row, pl.BlockSpec((tm, 3 * D), lambda i: (i, 0)), yspec, yspec, yspec,
                   _full((1, D)), _full((3, D))],
        out_shape=[sds((T, D), BF16), sds((T, D), BF16), sds((T, D), BF16), sds((T, D), BF16),
                   sds((T, IN_PAD), ACT), sds((T, 512), ACT), sds((T, 512), ACT), sds((T, 512), ACT),
                   sds((1, D), F32), sds((3, D), F32)],
    )(dxn, out, ya, yb, yc, p, p, p, bm, wa, wb, wc, wo, g_post)


def _loss_and_grad(y, target):
    T = y.shape[0]
    tm = min(1024, T)

    def body(y_ref, t_ref, l_ref, dy_ref):
        @pl.when(pl.program_id(0) == 0)
        def _():
            l_ref[...] = jnp.zeros_like(l_ref)
        e = y_ref[...] - t_ref[...]
        l_ref[...] += jnp.sum(e * e, axis=0, keepdims=True)
        dy_ref[...] = e * (1.0 / D)

    row = pl.BlockSpec((tm, D), lambda i: (i, 0))
    return _call(
        body, name="loss", grid=(T // tm,), in_specs=[row, row], out_specs=[_full((1, D)), row],
        out_shape=[jax.ShapeDtypeStruct((1, D), F32), jax.ShapeDtypeStruct((T, D), F32)],
    )(y, target)


_CHIP_STEPS = ((1, 0), (0, 1), (1, 1))


def _chip_exchange(srcs, per_target, name):
    n = len(srcs)

    def body(*refs):
        s_refs, r_refs = refs[:n], refs[n:2 * n]
        send_sems, recv_sems, local_sems = refs[2 * n:]
        x, y, c = lax.axis_index("x"), lax.axis_index("y"), lax.axis_index("c")
        me = 2 * x + y
        pick = lambda i, j: s_refs[i].at[j] if per_target[i] else s_refs[i]
        own = [pltpu.make_async_copy(pick(i, me), r_refs[i].at[me], local_sems.at[i]) for i in range(n)]
        for cp in own:
            cp.start()
        sent = []
        for k, (dx, dy) in enumerate(_CHIP_STEPS):
            tx, ty = (x + dx) % 2, (y + dy) % 2
            peer = 2 * tx + ty
            for i in range(n):
                cp = pltpu.make_async_remote_copy(
                    src_ref=pick(i, peer), dst_ref=r_refs[i].at[me], send_sem=send_sems.at[3 * i + k],
                    recv_sem=recv_sems.at[3 * i + k], device_id=(tx, ty, c), device_id_type=MESH)
                cp.start()
                sent.append((cp, i, k, peer))
        for _, i, k, peer in sent:
            pltpu.make_async_remote_copy(
                src_ref=pick(i, peer), dst_ref=r_refs[i].at[peer], send_sem=send_sems.at[3 * i + k],
                recv_sem=recv_sems.at[3 * i + k], device_id=(x, y, c), device_id_type=MESH).wait_recv()
        for cp, _, _, _ in sent:
            cp.wait_send()
        for cp in own:
            cp.wait()

    out_shape = [jax.ShapeDtypeStruct((4,) + (a.shape[1:] if pt else a.shape), a.dtype)
                 for a, pt in zip(srcs, per_target)]
    return pl.pallas_call(
        body, name=name, in_specs=[_HBM] * n, out_specs=[_HBM] * n, out_shape=out_shape,
        scratch_shapes=[pltpu.SemaphoreType.DMA((3 * n,)), pltpu.SemaphoreType.DMA((3 * n,)),
                        pltpu.SemaphoreType.DMA((n,))],
    )(*srcs)


def _sibling_swap(srcs, name):
    n = len(srcs)

    def body(*refs):
        s_refs, r_refs = refs[:n], refs[n:2 * n]
        send_sems, recv_sems = refs[2 * n:]
        x, y, c = lax.axis_index("x"), lax.axis_index("y"), lax.axis_index("c")
        copies = [pltpu.make_async_remote_copy(
            src_ref=s_refs[i], dst_ref=r_refs[i], send_sem=send_sems.at[i], recv_sem=recv_sems.at[i],
            device_id=(x, y, 1 - c), device_id_type=MESH) for i in range(n)]
        for cp in copies:
            cp.start()
        for cp in copies:
            cp.wait()

    return pl.pallas_call(
        body, name=name, in_specs=[_HBM] * n, out_specs=[_HBM] * n,
        out_shape=[jax.ShapeDtypeStruct(a.shape, a.dtype) for a in srcs],
        scratch_shapes=[pltpu.SemaphoreType.DMA((n,)), pltpu.SemaphoreType.DMA((n,))],
    )(*srcs)


_ELEMENTWISE_BLOCK_BYTES = 1 << 20


def _row_tile(rows, cols):
    if rows * cols * 4 <= 2 * _ELEMENTWISE_BLOCK_BYTES:
        return rows
    for tr in (2048, 1024, 512, 256, 128, 64, 32, 16, 8):
        if rows % tr == 0 and tr * cols * 4 <= _ELEMENTWISE_BLOCK_BYTES:
            return tr
    raise ValueError(f"no row tile for {(rows, cols)}")


def _sum_chips(r, name):
    _, R, C = r.shape
    tr = _row_tile(R, C)

    def body(r_ref, o_ref):
        f = lambda j: r_ref[j].astype(F32)
        o_ref[...] = ((f(0) + f(1)) + f(2)) + f(3)

    return _call(
        body, name=name, grid=(R // tr,),
        in_specs=[pl.BlockSpec((4, tr, C), lambda i: (0, i, 0))],
        out_specs=pl.BlockSpec((tr, C), lambda i: (i, 0)),
        out_shape=jax.ShapeDtypeStruct((R, C), F32),
    )(r)


def _adamw(w, m, v, g_mine, g_other, name):
    R, C = w.shape
    tr = _row_tile(R, C)

    def body(w_ref, m_ref, v_ref, a_ref, b_ref, g_ref, d_ref, nm_ref, nv_ref):
        g = a_ref[...] + b_ref[...]
        g_ref[...] = g
        m_new = ADAM_B1 * m_ref[...] + (1.0 - ADAM_B1) * g
        v_new = ADAM_B2 * v_ref[...] + (1.0 - ADAM_B2) * (g * g)
        nm_ref[...] = m_new
        nv_ref[...] = v_new
        m_hat = m_new / (1.0 - ADAM_B1 ** ADAM_STEP)
        v_hat = v_new / (1.0 - ADAM_B2 ** ADAM_STEP)
        d_ref[...] = -ADAM_LR * (m_hat / (jnp.sqrt(v_hat) + ADAM_EPS) + ADAM_WD * w_ref[...])

    blk = pl.BlockSpec((tr, C), lambda i: (i, 0))
    sds = jax.ShapeDtypeStruct((R, C), F32)
    return _call(body, name=name, grid=(R // tr,), in_specs=[blk] * 5, out_specs=[blk] * 4,
                 out_shape=[sds] * 4)(w, m, v, g_mine, g_other)


_NAMES = ('w_in', 'gm_w_s', 'gm_b_s', 'gm_norm_gain', 'sw_sinks', 'w_branch_a', 'w_branch_b',
          'w_branch_c', 'b_merge', 'w_out', 'g_pre', 'g_post')
_SMALL = ('gm_w_s', 'gm_b_s', 'gm_norm_gain', 'sw_sinks', 'g_pre', 'g_post')
_PACK_COLS = 1024


def _pack(arrays):
    flat = jnp.concatenate([a.reshape(-1).astype(F32) for a in arrays])
    rows = -(-flat.shape[0] // (8 * _PACK_COLS)) * 8
    return jnp.pad(flat, (0, rows * _PACK_COLS - flat.shape[0])).reshape(rows, _PACK_COLS)


def _unpack(buf, shapes):
    flat = buf.reshape(-1)
    out, off = [], 0
    for s in shapes:
        n = int(np.prod(s))
        out.append(flat[off:off + n].reshape(s))
        off += n
    return out


def _permute_cols(w):
    return jnp.concatenate(
        [jnp.zeros(w.shape[:-1] + (n,), w.dtype) if o is None else w[..., o:o + n] for o, n in _PERM], axis=-1)


def _unpermute_cols(w):
    new_off, off = {}, 0
    for o, n in _PERM:
        if o is not None:
            new_off[o] = (off, n)
        off += n
    return jnp.concatenate([w[..., new_off[o][0]:new_off[o][0] + new_off[o][1]] for o in sorted(new_off)], axis=-1)


def _rope_tables(T):
    half = HEAD_DIM // 2
    freqs = ROPE_THETA ** (-jnp.arange(half, dtype=F32) / half)
    ang = jnp.arange(T).astype(F32)[:, None] * freqs[None, :]
    cos, sin = jnp.cos(ang), jnp.sin(ang)
    return jnp.tile(jnp.concatenate([cos, cos], axis=1), (1, 2)), jnp.tile(jnp.concatenate([-sin, sin], axis=1), (1, 2))


def _layer_fwd(x, lw, tables):
    gpre, gpost = lw['g_pre'][None, :], lw['g_post'][None, :]
    bfull = jnp.broadcast_to(lw['bs'][:, :, None], (GM_GROUPS, BLOCK, BLOCK))
    gv = lw['gv'][None, :]
    p, h = _in_proj(x, gpre, lw['w_in'])
    ya = _gmlp_fwd(p, lw['ws'], bfull, gv)
    qkr = _rope_fwd(p, *tables)
    yb = _swa_fwd(qkr, p, lw['sinks'])
    oc, yc = _sb_fwd(p)
    merged, out, xn = _post_fwd(ya, yb, yc, p, lw['bm'], lw['wa'], lw['wb'], lw['wc'], lw['wo'], gpost, x)
    return xn, (lw, p, h, ya, qkr, yb, oc, yc, merged, out, bfull, gv, gpre, gpost)


def _layer_bwd(x, saved, dxn, tables):
    lw, p, h, ya, qkr, yb, oc, yc, merged, out, bfull, gv, gpre, gpost = saved
    g = {}
    (d_out, dpa, dpb, dpc, dp, dya, dyb, dyc, dgpost, dbm) = _post_bwd(
        dxn, out, ya, yb, yc, p, lw['bm'], lw['wa'], lw['wb'], lw['wc'], lw['wo'], gpost)
    g['w_out'] = _matmul_tn(merged, d_out, "grad_w_out")
    g['w_branch_a'] = _matmul_tn(ya, dpa, "grad_w_a")
    g['w_branch_b'] = _matmul_tn(yb, dpb, "grad_w_b")
    g['w_branch_c'] = _matmul_tn(yc, dpc, "grad_w_c")
    g['g_post'] = dgpost[0]
    g['b_merge'] = dbm
    dp, dws, dbacc, dgv = _gmlp_bwd(p, dya, lw['ws'], bfull, gv, dp)
    g['gm_w_s'] = dws
    g['gm_b_s'] = jnp.sum(dbacc.reshape(BLOCK, GM_GROUPS, BLOCK), axis=2).T
    g['gm_norm_gain'] = dgv[0]
    dp, dsink = _swa_bwd(qkr, p, dyb, lw['sinks'], *tables, dp)
    g['sw_sinks'] = dsink[:, 0]
    dp, dkc, dvc = _sb_bwd(p, oc, dyc, dp)
    dp = _sb_kv_into_dp(dkc, dvc, dp)
    g['w_in'] = _matmul_tn(h, dp, "grad_w_in")
    dx, dgpre = _in_bwd(dp, lw['w_in'], x, dxn, gpre)
    g['g_pre'] = dgpre[0]
    return dx, g


def kernel(x, w_in, gm_w_s, gm_b_s, gm_norm_gain, sw_sinks, w_branch_a, w_branch_b, w_branch_c, b_merge, w_out, g_pre, g_post, loss_target, m_w_in, m_gm_w_s, m_gm_b_s, m_gm_norm_gain, m_sw_sinks, m_w_branch_a, m_w_branch_b, m_w_branch_c, m_b_merge, m_w_out, m_g_pre, m_g_post, v_w_in, v_gm_w_s, v_gm_b_s, v_gm_norm_gain, v_sw_sinks, v_w_branch_a, v_w_branch_b, v_w_branch_c, v_b_merge, v_w_out, v_g_pre, v_g_post):
    T = x.shape[1]
    weights = dict(zip(_NAMES, (w_in, gm_w_s, gm_b_s, gm_norm_gain, sw_sinks, w_branch_a, w_branch_b,
                                w_branch_c, b_merge, w_out, g_pre, g_post)))
    mom_m = dict(zip(_NAMES, (m_w_in, m_gm_w_s, m_gm_b_s, m_gm_norm_gain, m_sw_sinks, m_w_branch_a,
                              m_w_branch_b, m_w_branch_c, m_b_merge, m_w_out, m_g_pre, m_g_post)))
    mom_v = dict(zip(_NAMES, (v_w_in, v_gm_w_s, v_gm_b_s, v_gm_norm_gain, v_sw_sinks, v_w_branch_a,
                              v_w_branch_b, v_w_branch_c, v_b_merge, v_w_out, v_g_pre, v_g_post)))
    abc = lambda d: jnp.stack([d['w_branch_a'], d['w_branch_b'], d['w_branch_c']])

    got = _chip_exchange([w_in.astype(BF16), abc(weights).astype(BF16), b_merge, w_out.astype(BF16)],
                         [False] * 4, "gather_weights")
    w_in_full = _permute_cols(jnp.concatenate([got[0][j] for j in range(4)], axis=2))
    wabc_full = jnp.concatenate([got[1][j] for j in range(4)], axis=3)
    bm_full = jnp.concatenate([got[2][j] for j in range(4)], axis=2)
    wo_full = jnp.concatenate([got[3][j] for j in range(4)], axis=1)

    tables = _rope_tables(T)
    xs = [x[0]]
    saved = []
    for l in range(DEPTH):
        lw = dict(w_in=w_in_full[l], wa=wabc_full[0, l], wb=wabc_full[1, l], wc=wabc_full[2, l], bm=bm_full[l],
                  wo=wo_full[l], ws=gm_w_s[l], bs=gm_b_s[l], gv=gm_norm_gain[l], sinks=sw_sinks[l],
                  g_pre=g_pre[l], g_post=g_post[l])
        xn, sv = _layer_fwd(xs[l], lw, tables)
        xs.append(xn)
        saved.append(sv)

    lsum, dxn = _loss_and_grad(xs[DEPTH], loss_target[0])
    loss = lax.psum(0.5 * jnp.sum(lsum) / D, ("x", "y", "c"))

    grads = {n: [None] * DEPTH for n in _NAMES}
    for l in reversed(range(DEPTH)):
        dxn, gl = _layer_bwd(xs[l], saved[l], dxn, tables)
        for n in _NAMES:
            grads[n][l] = gl[n]
    grad_x = dxn[None]
    full = {n: jnp.stack(grads[n]) for n in _NAMES}

    per_chip = lambda a, axis: jnp.stack(jnp.split(a, 4, axis=axis))
    g_in = per_chip(_unpermute_cols(full['w_in']), 2).astype(BF16)
    g_abc = per_chip(abc(full), 3).astype(BF16)
    g_bm = per_chip(full['b_merge'], 2)
    g_out = per_chip(full['w_out'], 1).astype(BF16)
    g_small = _pack([full[n] for n in _SMALL])
    got = _chip_exchange([g_in, g_abc, g_bm, g_out, g_small], [True, True, True, True, False], "scatter_grads")
    views = [(4 * D, IN_WIDTH // 4), (3 * 4 * 512, D // 4), (4 * 3, D // 4), (D, D), g_small.shape]
    sums = [_sum_chips(r.reshape((4,) + v), f"sum_chips_{i}") for i, (r, v) in enumerate(zip(got, views))]
    others = _sibling_swap(sums, "swap_core_sums")
    tensors = [lambda d: d['w_in'], abc, lambda d: d['b_merge'], lambda d: d['w_out'],
               lambda d: _pack([d[n] for n in _SMALL])]
    res = [_adamw(t(weights).reshape(v), t(mom_m).reshape(v), t(mom_v).reshape(v), s_, o_, f"adamw_{i}")
           for i, (t, v, s_, o_) in enumerate(zip(tensors, views, sums, others))]

    outs = []
    for kind in range(4):
        r = [res[i][kind] for i in range(5)]
        r_abc = r[1].reshape(3, 4, 512, D // 4)
        small = dict(zip(_SMALL, _unpack(r[4], [weights[n].shape for n in _SMALL])))
        big = dict(w_in=r[0].reshape(w_in.shape), w_branch_a=r_abc[0], w_branch_b=r_abc[1], w_branch_c=r_abc[2],
                   b_merge=r[2].reshape(b_merge.shape), w_out=r[3].reshape(w_out.shape))
        outs.extend(big[n] if n in big else small[n] for n in _NAMES)
    return (loss, grad_x, *outs)
```

```python
import functools

import numpy as np
import jax
import jax.numpy as jnp
from jax import lax
from jax.experimental import pallas as pl
from jax.experimental.pallas import tpu as pltpu

F32 = jnp.float32
BF16 = jnp.bfloat16
ACT = jnp.bfloat16

D = 1024
DEPTH = 4
BLOCK = 128
EPS = 1e-6
NEG = -1e30
GM_GROUPS = 4
GM_WIDTH = 512
HEAD_DIM = 64
SW_HEADS = 8
SB_HEADS = 4
SB_HEAD_DIM = 128
ROPE_THETA = 10000.0
IN_WIDTH = 7936
IN_PAD = 8192

O_UA, O_VA, O_GA, O_QB, O_KB, O_VB, O_GB = 0, 512, 1024, 1536, 2048, 2176, 2304
O_QC, O_KC, O_VC, O_GC, O_MG = 2816, 3328, 3840, 4352, 4864
_PERM = ((O_MG, 3072), (O_UA, 512), (O_VA, 512), (O_GA, 512),
         (O_QB, 512), (O_KB, 128), (O_VB, 128), (None, IN_PAD - IN_WIDTH), (O_GB, 512)) + tuple(
    (o + 128 * h, 128) for h in range(4) for o in (O_QC, O_GC, O_KC, O_VC))
N_MG, N_UA, N_VA, N_GA = 0, 3072, 3584, 4096
N_QB, N_KB, N_VB, N_GB = 4608, 5120, 5248, 5632
N_C = 6144

ADAM_LR, ADAM_B1, ADAM_B2, ADAM_EPS, ADAM_WD, ADAM_STEP = 0.001, 0.9, 0.999, 1e-08, 0.01, 10

SB_EXIT = -104.0
V7X_VMEM_LIMIT = 48 * 1024 * 1024
MESH = pl.DeviceIdType.MESH


_HBM = pl.BlockSpec(memory_space=pl.ANY)


def _call(body, *, name, grid, in_specs, out_specs, out_shape, scratch=(), aliases=None):
    return pl.pallas_call(
        body, name=name, grid=grid, in_specs=in_specs, out_specs=out_specs, out_shape=out_shape,
        scratch_shapes=list(scratch), input_output_aliases=aliases or {},
        compiler_params=pltpu.CompilerParams(
            dimension_semantics=("arbitrary",) * len(grid), vmem_limit_bytes=V7X_VMEM_LIMIT))


def _sigmoid(x):
    return 1.0 / (1.0 + jnp.exp(-x))


def _silu_and_grad(x):
    s = _sigmoid(x)
    return x * s, s * (1.0 + x * (1.0 - s))


def _dot(a, b):
    return jnp.dot(a, b, preferred_element_type=F32)


def _dot_nt(a, b):
    return lax.dot_general(a, b, (((1,), (1,)), ((), ())), preferred_element_type=F32)


def _dot_tn(a, b):
    return lax.dot_general(a, b, (((0,), (0,)), ((), ())), preferred_element_type=F32)


def _full(shape):
    return pl.BlockSpec(shape, lambda *_: (0,) * len(shape))


_CHIP_STEPS = ((1, 0), (0, 1), (1, 1))


def _exchange_ops(s_refs, r_refs, send_sems, recv_sems, local_sems, per_target):
    n = len(s_refs)

    def copies():
        x, y, c = lax.axis_index("x"), lax.axis_index("y"), lax.axis_index("c")
        me = 2 * x + y
        pick = lambda i, j: s_refs[i].at[j] if per_target[i] else s_refs[i]
        own = [pltpu.make_async_copy(pick(i, me), r_refs[i].at[me], local_sems.at[i]) for i in range(n)]
        sent, arriving = [], []
        for k, (dx, dy) in enumerate(_CHIP_STEPS):
            tx, ty = (x + dx) % 2, (y + dy) % 2
            peer = 2 * tx + ty
            for i in range(n):
                sems = dict(send_sem=send_sems.at[3 * i + k], recv_sem=recv_sems.at[3 * i + k])
                sent.append(pltpu.make_async_remote_copy(
                    src_ref=pick(i, peer), dst_ref=r_refs[i].at[me], device_id=(tx, ty, c),
                    device_id_type=MESH, **sems))
                arriving.append(pltpu.make_async_remote_copy(
                    src_ref=pick(i, peer), dst_ref=r_refs[i].at[peer], device_id=(x, y, c),
                    device_id_type=MESH, **sems))
        return own, sent, arriving

    def start():
        own, sent, _ = copies()
        for cp in own + sent:
            cp.start()

    def wait():
        own, sent, arriving = copies()
        for cp in arriving:
            cp.wait_recv()
        for cp in sent:
            cp.wait_send()
        for cp in own:
            cp.wait()

    return start, wait


def _exchange_shapes(srcs, per_target):
    return [jax.ShapeDtypeStruct((4,) + (a.shape[1:] if pt else a.shape), a.dtype)
            for a, pt in zip(srcs, per_target)]


def _exchange_sems(n):
    return [pltpu.SemaphoreType.DMA((3 * n,)), pltpu.SemaphoreType.DMA((3 * n,)), pltpu.SemaphoreType.DMA((n,))]


def _chip_exchange(srcs, per_target, name):
    n = len(srcs)

    def body(*refs):
        start, wait = _exchange_ops(refs[:n], refs[n:2 * n], *refs[2 * n:], per_target)
        start()
        wait()

    return pl.pallas_call(
        body, name=name, in_specs=[_HBM] * n, out_specs=[_HBM] * n, out_shape=_exchange_shapes(srcs, per_target),
        scratch_shapes=_exchange_sems(n),
    )(*srcs)


def _in_proj(x, g_pre, w_in, carry=()):
    T = x.shape[0]
    tm, tn = min(2048, T), 512
    ni, nj, n = T // tm, IN_PAD // tn, len(carry)

    def body(x_ref, g_ref, w_ref, *rest):
        p_ref, h_ref = rest[n], rest[n + 1]
        i, j = pl.program_id(0), pl.program_id(1)
        if n:
            start, wait = _exchange_ops(rest[:n], rest[n + 2:2 * n + 2], *rest[2 * n + 2:], (False,) * n)
            pl.when((i == 0) & (j == 0))(start)

        @pl.when(j == 0)
        def _():
            xf = x_ref[...]
            r = lax.rsqrt(jnp.mean(xf * xf, axis=-1, keepdims=True) + EPS)
            h_ref[...] = (xf * r * g_ref[...]).astype(BF16)
        p_ref[...] = _dot(h_ref[...], w_ref[...]).astype(p_ref.dtype)
        if n:
            pl.when((i == ni - 1) & (j == nj - 1))(wait)

    return _call(
        body, name="in_proj_gather" if n else "in_proj", grid=(ni, nj),
        in_specs=[pl.BlockSpec((tm, D), lambda i, j: (i, 0)), _full((1, D)),
                  pl.BlockSpec((D, tn), lambda i, j: (0, j))] + [_HBM] * n,
        out_specs=[pl.BlockSpec((tm, tn), lambda i, j: (i, j)), pl.BlockSpec((tm, D), lambda i, j: (i, 0))]
        + [_HBM] * n,
        out_shape=[jax.ShapeDtypeStruct((T, IN_PAD), ACT), jax.ShapeDtypeStruct((T, D), BF16)]
        + _exchange_shapes(carry, (False,) * n),
        scratch=_exchange_sems(n) if n else (),
    )(x, g_pre, w_in, *carry)


def _in_bwd(dp, w_in, x, dxn, g_pre, carry=()):
    T = x.shape[0]
    tm, tk = min(1024, T), 1024
    ni, nk, n = T // tm, IN_PAD // tk, len(carry)

    def body(dp_ref, w_ref, x_ref, dxn_ref, g_ref, *rest):
        dx_ref, dg_ref, acc = rest[n], rest[n + 1], rest[2 * n + 2]
        i, k = pl.program_id(0), pl.program_id(1)
        if n:
            start, wait = _exchange_ops(rest[:n], rest[n + 2:2 * n + 2], *rest[2 * n + 3:], (True,) * n)
            pl.when((i == 0) & (k == 0))(start)

        @pl.when(k == 0)
        def _():
            acc[...] = jnp.zeros_like(acc)

        @pl.when((i == 0) & (k == 0))
        def _():
            dg_ref[...] = jnp.zeros_like(dg_ref)

        acc[...] += _dot_nt(dp_ref[...], w_ref[...])

        @pl.when(k == nk - 1)
        def _():
            dh = acc[...]
            xf = x_ref[...]
            r = lax.rsqrt(jnp.mean(xf * xf, axis=-1, keepdims=True) + EPS)
            a = dh * g_ref[...]
            dx_ref[...] = dxn_ref[...] + r * a - xf * (r * r * r) * jnp.mean(a * xf, axis=-1, keepdims=True)
            dg_ref[...] += jnp.sum(dh * xf * r, axis=0, keepdims=True)

        if n:
            pl.when((i == ni - 1) & (k == nk - 1))(wait)

    return _call(
        body, name="in_bwd_scatter" if n else "in_bwd", grid=(ni, nk),
        in_specs=[pl.BlockSpec((tm, tk), lambda i, k: (i, k)), pl.BlockSpec((D, tk), lambda i, k: (0, k)),
                  pl.BlockSpec((tm, D), lambda i, k: (i, 0)), pl.BlockSpec((tm, D), lambda i, k: (i, 0)),
                  _full((1, D))] + [_HBM] * n,
        out_specs=[pl.BlockSpec((tm, D), lambda i, k: (i, 0)), _full((1, D))] + [_HBM] * n,
        out_shape=[jax.ShapeDtypeStruct((T, D), F32), jax.ShapeDtypeStruct((1, D), F32)]
        + _exchange_shapes(carry, (True,) * n),
        scratch=[pltpu.VMEM((tm, D), F32)] + (_exchange_sems(n) if n else []),
    )(dp, w_in, x, dxn, g_pre, *carry)


def _matmul_tn(a, b, name):
    T, K = a.shape
    N = b.shape[1]
    tk, tn, tt = min(K, 1024), min(N, 2048), min(T, 512)

    def body(a_ref, b_ref, o_ref):
        @pl.when(pl.program_id(2) == 0)
        def _():
            o_ref[...] = jnp.zeros_like(o_ref)
        o_ref[...] += _dot_tn(a_ref[...], b_ref[...])

    return _call(
        body, name=name, grid=(K // tk, N // tn, T // tt),
        in_specs=[pl.BlockSpec((tt, tk), lambda i, j, t: (t, i)), pl.BlockSpec((tt, tn), lambda i, j, t: (t, j))],
        out_specs=pl.BlockSpec((tk, tn), lambda i, j, t: (i, j)),
        out_shape=jax.ShapeDtypeStruct((K, N), F32),
    )(a, b)


def _gm_forward_parts(v_ref, gv_ref, ws_ref, bf_ref, nch):
    vf = v_ref[...].astype(F32)
    mu = jnp.mean(vf, axis=-1, keepdims=True)
    xc = vf - mu
    rstd = lax.rsqrt(jnp.mean(xc * xc, axis=-1, keepdims=True) + EPS)
    xhat = xc * rstd
    vnb = (xhat * gv_ref[...]).astype(BF16)
    row = lax.broadcasted_iota(jnp.int32, (BLOCK, BLOCK), 0)
    col = lax.broadcasted_iota(jnp.int32, (BLOCK, BLOCK), 1)
    mixed, vcats, wgs = [], [], []
    for g in range(GM_GROUPS):
        vg = vnb[:, BLOCK * g:BLOCK * (g + 1)]
        vcat = jnp.concatenate([vg[BLOCK * k:BLOCK * (k + 1), :] for k in range(nch)], axis=1)
        wg = jnp.where(row >= col, ws_ref[g], 0.0)
        m = _dot(wg.astype(BF16), vcat)
        mixed.append(jnp.concatenate(
            [m[:, BLOCK * k:BLOCK * (k + 1)] + bf_ref[g] for k in range(nch)], axis=0))
        vcats.append(vcat)
        wgs.append(wg)
    return xhat, rstd, jnp.concatenate(mixed, axis=1), vcats, wgs, row >= col


def _gmlp_fwd(p, ws, bfull, gv):
    T = p.shape[0]
    tm = min(512, T)
    nch = tm // BLOCK

    def body(u_ref, v_ref, gt_ref, ws_ref, bf_ref, gv_ref, y_ref):
        _, _, mixed, _, _, _ = _gm_forward_parts(v_ref, gv_ref, ws_ref, bf_ref, nch)
        sg, _ = _silu_and_grad(gt_ref[...].astype(F32))
        y_ref[...] = (u_ref[...].astype(F32) * mixed * sg).astype(y_ref.dtype)

    seg = lambda off: pl.BlockSpec((tm, 512), lambda i: (i, off // 512))
    return _call(
        body, name="gmlp_fwd", grid=(T // tm,),
        in_specs=[seg(N_UA), seg(N_VA), seg(N_GA), _full((GM_GROUPS, BLOCK, BLOCK)),
                  _full((GM_GROUPS, BLOCK, BLOCK)), _full((1, GM_WIDTH))],
        out_specs=pl.BlockSpec((tm, 512), lambda i: (i, 0)),
        out_shape=jax.ShapeDtypeStruct((T, GM_WIDTH), BF16),
    )(p, p, p, ws, bfull, gv)


def _gmlp_bwd(p, dy, ws, bfull, gv, dp):
    T = p.shape[0]
    tm = min(512, T)
    nch = tm // BLOCK

    def body(u_ref, v_ref, gt_ref, dy_ref, ws_ref, bf_ref, gv_ref, _, dp_ref, dws_ref, db_ref, dgv_ref):
        @pl.when(pl.program_id(0) == 0)
        def _():
            dws_ref[...] = jnp.zeros_like(dws_ref)
            db_ref[...] = jnp.zeros_like(db_ref)
            dgv_ref[...] = jnp.zeros_like(dgv_ref)

        xhat, rstd, mixed, vcats, wgs, tril = _gm_forward_parts(v_ref, gv_ref, ws_ref, bf_ref, nch)
        u = u_ref[...].astype(F32)
        gt = gt_ref[...].astype(F32)
        dyf = dy_ref[...].astype(F32)
        sg, dsg = _silu_and_grad(gt)
        du = dyf * mixed * sg
        dmixed = dyf * u * sg
        dgate = dyf * (u * mixed) * dsg
        dvn = []
        for g in range(GM_GROUPS):
            dmg = dmixed[:, BLOCK * g:BLOCK * (g + 1)]
            chunks = [dmg[BLOCK * k:BLOCK * (k + 1), :] for k in range(nch)]
            dmcat = jnp.concatenate(chunks, axis=1).astype(BF16)
            dws_ref[g] += jnp.where(tril, _dot_nt(dmcat, vcats[g]), 0.0)
            dvcat = _dot(wgs[g].T.astype(BF16), dmcat)
            dvn.append(jnp.concatenate([dvcat[:, BLOCK * k:BLOCK * (k + 1)] for k in range(nch)], axis=0))
            db_ref[:, BLOCK * g:BLOCK * (g + 1)] += functools.reduce(lambda a, b: a + b, chunks)
        dvn = jnp.concatenate(dvn, axis=1)
        dgv_ref[...] += jnp.sum(dvn * xhat, axis=0, keepdims=True)
        dxh = dvn * gv_ref[...]
        dv = rstd * (dxh - jnp.mean(dxh, axis=-1, keepdims=True)
                     - xhat * jnp.mean(dxh * xhat, axis=-1, keepdims=True))
        dp_ref[:, 0:512] = du.astype(dp_ref.dtype)
        dp_ref[:, 512:1024] = dv.astype(dp_ref.dtype)
        dp_ref[:, 1024:1536] = dgate.astype(dp_ref.dtype)

    seg = lambda off: pl.BlockSpec((tm, 512), lambda i: (i, off // 512))
    return _call(
        body, name="gmlp_bwd", grid=(T // tm,),
        in_specs=[seg(N_UA), seg(N_VA), seg(N_GA), pl.BlockSpec((tm, 512), lambda i: (i, 0)),
                  _full((GM_GROUPS, BLOCK, BLOCK)), _full((GM_GROUPS, BLOCK, BLOCK)), _full((1, GM_WIDTH)), _HBM],
        out_specs=[pl.BlockSpec((tm, 1536), lambda i: (i, N_UA // 1536)), _full((GM_GROUPS, BLOCK, BLOCK)),
                   _full((BLOCK, GM_WIDTH)), _full((1, GM_WIDTH))],
        out_shape=[jax.ShapeDtypeStruct(dp.shape, dp.dtype), jax.ShapeDtypeStruct((GM_GROUPS, BLOCK, BLOCK), F32),
                   jax.ShapeDtypeStruct((BLOCK, GM_WIDTH), F32), jax.ShapeDtypeStruct((1, GM_WIDTH), F32)],
        aliases={7: 0},
    )(p, p, p, dy, ws, bfull, gv, dp)


def _swap_halves(x):
    lane = lax.broadcasted_iota(jnp.int32, x.shape, 1) % HEAD_DIM
    return jnp.where(lane < HEAD_DIM // 2, pltpu.roll(x, 96, 1), pltpu.roll(x, 32, 1))


def _rope_fwd(p, cos_t, sin_t):
    T = p.shape[0]
    tm = min(512, T)

    def body(q_ref, k_ref, c_ref, s_ref, o_ref):
        c, s = c_ref[...], s_ref[...]
        for G in range(5):
            xg = (q_ref[:, 128 * G:128 * (G + 1)] if G < 4 else k_ref[...]).astype(F32)
            o_ref[:, 128 * G:128 * (G + 1)] = (xg * c + _swap_halves(xg) * s).astype(o_ref.dtype)

    return _call(
        body, name="rope_fwd", grid=(T // tm,),
        in_specs=[pl.BlockSpec((tm, 512), lambda i: (i, N_QB // 512)),
                  pl.BlockSpec((tm, 128), lambda i: (i, N_KB // 128)),
                  pl.BlockSpec((tm, 128), lambda i: (i, 0)), pl.BlockSpec((tm, 128), lambda i: (i, 0))],
        out_specs=pl.BlockSpec((tm, 640), lambda i: (i, 0)),
        out_shape=jax.ShapeDtypeStruct((T, 640), BF16),
    )(p, p, cos_t, sin_t)


def _unrotate(d, c, s):
    return d * c + _swap_halves(d * s)


def _dup_heads(x):
    left = lax.broadcasted_iota(jnp.int32, x.shape, 1) < HEAD_DIM
    r = pltpu.roll(x, HEAD_DIM, 1)
    return jnp.where(left, x, r).astype(BF16), jnp.where(left, r, x).astype(BF16)


def _fold_heads(acc0, acc1):
    left = lax.broadcasted_iota(jnp.int32, acc0.shape, 1) < HEAD_DIM
    t0 = acc0 + pltpu.roll(acc0, HEAD_DIM, 1)
    t1 = acc1 + pltpu.roll(acc1, HEAD_DIM, 1)
    return jnp.where(left, t0, t1)


def _swa_valid(base):
    qpos = base + lax.broadcasted_iota(jnp.int32, (BLOCK, 2 * BLOCK), 0)
    kpos = base - BLOCK + lax.broadcasted_iota(jnp.int32, (BLOCK, 2 * BLOCK), 1)
    return jnp.logical_and(kpos >= 0, jnp.logical_and(kpos <= qpos, kpos > qpos - BLOCK))


def _swa_probs(qm, kk, valid, sink):
    s = jnp.where(valid, _dot_nt(qm, kk) * (HEAD_DIM ** -0.5), NEG)
    m = jnp.maximum(jnp.max(s, axis=-1, keepdims=True), sink)
    e = jnp.exp(s - m)
    es = jnp.exp(sink - m)
    inv = 1.0 / (jnp.sum(e, axis=-1, keepdims=True) + es)
    return e * inv, es * inv


def _swa_specs(T, bq, rev):
    n = T // bq
    blk = (lambda i: n - 1 - i) if rev else (lambda i: i)
    halo = lambda i: jnp.maximum(blk(i) * (bq // BLOCK) - 1, 0)
    return blk, [
        pl.BlockSpec(memory_space=pltpu.SMEM),
        pl.BlockSpec((bq, 512), lambda i: (blk(i), 0)),
        pl.BlockSpec((bq, 128), lambda i: (blk(i), 4)),
        pl.BlockSpec((BLOCK, 128), lambda i: (halo(i), 4)),
        pl.BlockSpec((bq, 128), lambda i: (blk(i), N_VB // 128)),
        pl.BlockSpec((BLOCK, 128), lambda i: (halo(i), N_VB // 128)),
        pl.BlockSpec((bq, 512), lambda i: (blk(i), N_GB // 512)),
    ]


def _swa_fwd(qkr, p, sinks):
    T = p.shape[0]
    bq = min(512, T)
    nsb = bq // BLOCK
    blk, specs = _swa_specs(T, bq, False)

    def body(sink_ref, q_ref, kc_ref, kh_ref, vc_ref, vh_ref, gt_ref, y_ref):
        base = blk(pl.program_id(0)) * bq
        kd = _dup_heads(jnp.concatenate([kh_ref[...], kc_ref[...]], axis=0).astype(F32))
        vd = _dup_heads(jnp.concatenate([vh_ref[...], vc_ref[...]], axis=0).astype(F32))
        left = lax.broadcasted_iota(jnp.int32, (BLOCK, BLOCK), 1) < HEAD_DIM
        for sb in range(nsb):
            r0 = sb * BLOCK
            valid = _swa_valid(base + r0)
            for G in range(4):
                kk, vv = kd[G // 2][r0:r0 + 2 * BLOCK], vd[G // 2][r0:r0 + 2 * BLOCK]
                q128 = q_ref[r0:r0 + BLOCK, 128 * G:128 * (G + 1)]
                outs = []
                for hh in range(2):
                    qm = jnp.where(left if hh == 0 else jnp.logical_not(left), q128, jnp.zeros_like(q128))
                    pr, _ = _swa_probs(qm, kk, valid, sink_ref[2 * G + hh])
                    outs.append(_dot(pr.astype(BF16), vv))
                o = jnp.where(left, outs[0], outs[1])
                sg, _ = _silu_and_grad(gt_ref[r0:r0 + BLOCK, 128 * G:128 * (G + 1)].astype(F32))
                y_ref[r0:r0 + BLOCK, 128 * G:128 * (G + 1)] = (o * sg).astype(y_ref.dtype)

    return _call(
        body, name="swa_fwd", grid=(T // bq,), in_specs=specs,
        out_specs=pl.BlockSpec((bq, 512), lambda i: (i, 0)),
        out_shape=jax.ShapeDtypeStruct((T, 512), BF16),
    )(sinks, qkr, qkr, qkr, p, p, p)


def _swa_bwd(qkr, p, dy, sinks, cos_t, sin_t, dp):
    T = p.shape[0]
    bq = min(512, T)
    nsb = bq // BLOCK
    blk, specs = _swa_specs(T, bq, True)

    def body(sink_ref, q_ref, kc_ref, kh_ref, vc_ref, vh_ref, gt_ref, dy_ref, c_ref, s_ref, _,
             dp_ref, ds_ref, dk_acc, dv_acc, k_carry, v_carry):
        @pl.when(pl.program_id(0) == 0)
        def _():
            k_carry[...] = jnp.zeros_like(k_carry)
            v_carry[...] = jnp.zeros_like(v_carry)
            ds_ref[...] = jnp.zeros_like(ds_ref)

        base = blk(pl.program_id(0)) * bq
        dk_acc[...] = jnp.zeros_like(dk_acc)
        dv_acc[...] = jnp.zeros_like(dv_acc)
        kd = _dup_heads(jnp.concatenate([kh_ref[...], kc_ref[...]], axis=0).astype(F32))
        vd = _dup_heads(jnp.concatenate([vh_ref[...], vc_ref[...]], axis=0).astype(F32))
        left = lax.broadcasted_iota(jnp.int32, (BLOCK, BLOCK), 1) < HEAD_DIM
        for sb in range(nsb):
            r0 = sb * BLOCK
            rows = slice(r0, r0 + BLOCK)
            valid = _swa_valid(base + r0)
            dkp = [jnp.zeros((2 * BLOCK, BLOCK), F32)] * 2
            dvp = [jnp.zeros((2 * BLOCK, BLOCK), F32)] * 2
            for G in range(4):
                g = G // 2
                kk, vv = kd[g][r0:r0 + 2 * BLOCK], vd[g][r0:r0 + 2 * BLOCK]
                cols = slice(128 * G, 128 * (G + 1))
                q128 = q_ref[rows, cols]
                qms, prs, pss, outs = [], [], [], []
                for hh in range(2):
                    qm = jnp.where(left if hh == 0 else jnp.logical_not(left), q128, jnp.zeros_like(q128))
                    pr, ps = _swa_probs(qm, kk, valid, sink_ref[2 * G + hh])
                    qms.append(qm)
                    prs.append(pr)
                    pss.append(ps)
                    outs.append(_dot(pr.astype(BF16), vv))
                o = jnp.where(left, outs[0], outs[1])
                sg, dsg = _silu_and_grad(gt_ref[rows, cols].astype(F32))
                dyf = dy_ref[rows, cols].astype(F32)
                do = dyf * sg
                dp_ref[rows, 1024 + 128 * G:1024 + 128 * (G + 1)] = (dyf * o * dsg).astype(dp_ref.dtype)
                dqs = []
                for hh in range(2):
                    dom = jnp.where(left if hh == 0 else jnp.logical_not(left), do, 0.0).astype(BF16)
                    dpv = _dot_nt(dom, vv)
                    delta = jnp.sum(prs[hh] * dpv, axis=-1, keepdims=True)
                    dsc = prs[hh] * (dpv - delta) * (HEAD_DIM ** -0.5)
                    h = 2 * G + hh
                    ds_ref[h:h + 1, :] += jnp.broadcast_to(
                        -jnp.sum(pss[hh] * delta, axis=0, keepdims=True), (1, 128))
                    dqs.append(_dot(dsc.astype(BF16), kk))
                    dkp[g] = dkp[g] + _dot(dsc.T.astype(BF16), qms[hh])
                    dvp[g] = dvp[g] + _dot(prs[hh].T.astype(BF16), dom)
                dq = _unrotate(jnp.where(left, dqs[0], dqs[1]), c_ref[rows, :], s_ref[rows, :])
                dp_ref[rows, cols] = dq.astype(dp_ref.dtype)
            dk_acc[r0:r0 + 2 * BLOCK, :] += _fold_heads(dkp[0], dkp[1])
            dv_acc[r0:r0 + 2 * BLOCK, :] += _fold_heads(dvp[0], dvp[1])
        dk_acc[bq:bq + BLOCK, :] += k_carry[...]
        dv_acc[bq:bq + BLOCK, :] += v_carry[...]
        dk = _unrotate(dk_acc[BLOCK:bq + BLOCK, :], c_ref[...], s_ref[...])
        dp_ref[:, 512:640] = dk.astype(dp_ref.dtype)
        dp_ref[:, 640:768] = dv_acc[BLOCK:bq + BLOCK, :].astype(dp_ref.dtype)
        dp_ref[:, 768:1024] = jnp.zeros((bq, 256), dp_ref.dtype)
        k_carry[...] = dk_acc[0:BLOCK, :]
        v_carry[...] = dv_acc[0:BLOCK, :]

    rowblk = lambda w: pl.BlockSpec((bq, w), lambda i: (blk(i), 0))
    return _call(
        body, name="swa_bwd", grid=(T // bq,), in_specs=specs + [rowblk(512), rowblk(128), rowblk(128), _HBM],
        out_specs=[pl.BlockSpec((bq, 1536), lambda i: (blk(i), N_QB // 1536)), _full((SW_HEADS, 128))],
        out_shape=[jax.ShapeDtypeStruct(dp.shape, dp.dtype), jax.ShapeDtypeStruct((SW_HEADS, 128), F32)],
        scratch=[pltpu.VMEM((bq + BLOCK, 128), F32), pltpu.VMEM((bq + BLOCK, 128), F32),
                 pltpu.VMEM((BLOCK, 128), F32), pltpu.VMEM((BLOCK, 128), F32)],
        aliases={10: 0},
    )(sinks, qkr, qkr, qkr, p, p, p, dy, cos_t, sin_t, dp)


def _split_dot(x, tri):
    hi = x.astype(BF16)
    lo = (x - hi.astype(F32)).astype(BF16)
    return _dot(hi, tri) + _dot(lo, tri)


def _sb_logits(qs, kj):
    z = _dot_nt(qs, kj)
    e = jnp.exp(-jnp.abs(z))
    return z, e, -(jnp.maximum(z, 0.0) + jnp.log(1.0 + e))


def _sb_before(tq, qpos0, kpos0):
    qpos = qpos0 + lax.broadcasted_iota(jnp.int32, (tq, tq), 0)
    kpos = kpos0 + lax.broadcasted_iota(jnp.int32, (tq, tq), 1)
    return kpos < qpos


def _sb_tris(kb):
    row = lax.broadcasted_iota(jnp.int32, (kb, kb), 0)
    col = lax.broadcasted_iota(jnp.int32, (kb, kb), 1)
    return jnp.stack([jnp.where(row >= col, 1.0, 0.0), jnp.where(row > col, 1.0, 0.0)]).astype(BF16)


def _sb_window(qi, w, tq):
    jl = qi - 2 * w
    live = jl >= 1
    rows_l = pl.ds(pl.multiple_of(jl * tq, tq), tq)
    rows_e = pl.ds(pl.multiple_of(jnp.maximum(jl - 1, 0) * tq, tq), tq)
    return jl, live, rows_l, rows_e


def _sb_fwd(p, tris):
    T = p.shape[0]
    tq = min(256, T)
    scale = SB_HEAD_DIM ** -0.5

    def body(q_ref, gt_ref, k_ref, v_ref, tri_ref, o_ref, y_ref, acc, r_ref):
        qi = pl.program_id(1)
        qs = (q_ref[...].astype(F32) * scale).astype(BF16)
        acc[...] = jnp.zeros_like(acc)
        r_ref[...] = jnp.zeros_like(r_ref)

        def step(carry):
            w, _ = carry
            jl, live, rows_l, rows_e = _sb_window(qi, w, tq)
            r0 = r_ref[...]
            before = _sb_before(tq, qi * tq, jl * tq)
            z1, _, lf1 = _sb_logits(qs, k_ref[rows_l, :])
            lf1 = jnp.where(before, lf1, 0.0)
            t1 = jnp.sum(lf1, axis=-1, keepdims=True)
            a1 = jnp.where(before, jnp.exp(z1 + _split_dot(lf1, tri_ref[0]) + r0), 0.0)
            z2, _, lf2 = _sb_logits(qs, k_ref[rows_e, :])
            lf2 = jnp.where(live, lf2, 0.0)
            a2 = jnp.where(live, jnp.exp(z2 + _split_dot(lf2, tri_ref[0]) + (r0 + t1)), 0.0)
            acc[...] += _dot(a1.astype(BF16), v_ref[rows_l, :]) + _dot(a2.astype(BF16), v_ref[rows_e, :])
            r_new = r0 + t1 + jnp.sum(lf2, axis=-1, keepdims=True)
            r_ref[...] = r_new
            return w + 1, jnp.max(r_new) > SB_EXIT

        lax.while_loop(lambda c: jnp.logical_and(qi - 2 * c[0] >= 0, c[1]), step, (0, True))
        o = acc[...]
        o_ref[...] = o
        sg, _ = _silu_and_grad(gt_ref[...].astype(F32))
        y_ref[...] = (o * sg).astype(y_ref.dtype)

    col = lambda k: pl.BlockSpec((tq, 128), lambda h, i: (i, N_C // 128 + 4 * h + k))
    whole = lambda k: pl.BlockSpec((T, 128), lambda h, i: (0, N_C // 128 + 4 * h + k))
    out = pl.BlockSpec((tq, 128), lambda h, i: (i, h))
    return _call(
        body, name="sb_fwd", grid=(SB_HEADS, T // tq),
        in_specs=[col(0), col(1), whole(2), whole(3), _full((2, tq, tq))],
        out_specs=[out, out],
        out_shape=[jax.ShapeDtypeStruct((T, 512), F32), jax.ShapeDtypeStruct((T, 512), BF16)],
        scratch=[pltpu.VMEM((tq, 128), F32), pltpu.VMEM((tq, 1), F32)],
    )(p, p, p, p, tris)


def _sb_bwd(p, o, dy, tris, dp):
    T = p.shape[0]
    tq = min(256, T)
    nq = T // tq
    scale = SB_HEAD_DIM ** -0.5

    def body(q_ref, gt_ref, k_ref, v_ref, o_ref, dy_ref, tri_ref, _, dqg_ref, dk_hbm, dv_hbm,
             dq_acc, r_ref, s_ref, dk_acc, dv_acc, sem):
        h, qi = pl.program_id(0), pl.program_id(1)

        @pl.when(qi == 0)
        def _():
            dk_acc[...] = jnp.zeros_like(dk_acc)
            dv_acc[...] = jnp.zeros_like(dv_acc)

        qs = (q_ref[...].astype(F32) * scale).astype(BF16)
        of = o_ref[...]
        dyf = dy_ref[...].astype(F32)
        sg, dsg = _silu_and_grad(gt_ref[...].astype(F32))
        do = dyf * sg
        dqg_ref[:, 128:256] = (dyf * of * dsg).astype(dqg_ref.dtype)
        delta = jnp.sum(do * of, axis=-1, keepdims=True)
        dob = do.astype(BF16)
        dq_acc[...] = jnp.zeros_like(dq_acc)
        r_ref[...] = jnp.zeros_like(r_ref)
        s_ref[...] = jnp.zeros_like(s_ref)

        def tile(rows, keep, r_in, s_in):
            kj, vj = k_ref[rows, :], v_ref[rows, :]
            z, e, lf = _sb_logits(qs, kj)
            lf = jnp.where(keep, lf, 0.0)
            a = jnp.where(keep, jnp.exp(z + _split_dot(lf, tri_ref[0]) + r_in), 0.0)
            gz = a * _dot_nt(dob, vj)
            later = _split_dot(gz, tri_ref[1]) + s_in
            sig = jnp.where(z >= 0.0, 1.0, e) / (1.0 + e)
            dz = jnp.where(keep, gz - sig * (delta - later), 0.0)
            dk_acc[rows, :] += _dot(dz.T.astype(BF16), qs)
            dv_acc[rows, :] += _dot(a.T.astype(BF16), dob)
            return (_dot(dz.astype(BF16), kj), jnp.sum(lf, axis=-1, keepdims=True),
                    jnp.sum(gz, axis=-1, keepdims=True))

        def step(carry):
            w, _ = carry
            jl, live, rows_l, rows_e = _sb_window(qi, w, tq)
            r0, s0 = r_ref[...], s_ref[...]
            dq1, t1, g1 = tile(rows_l, _sb_before(tq, qi * tq, jl * tq), r0, s0)
            dq2, t2, g2 = tile(rows_e, live, r0 + t1, s0 + g1)
            dq_acc[...] += dq1 + dq2
            s_ref[...] = s0 + g1 + g2
            r_new = r0 + t1 + t2
            r_ref[...] = r_new
            return w + 1, jnp.max(r_new) > SB_EXIT

        lax.while_loop(lambda c: jnp.logical_and(qi - 2 * c[0] >= 0, c[1]), step, (0, True))
        dqg_ref[:, 0:128] = (dq_acc[...] * scale).astype(dqg_ref.dtype)

        @pl.when(qi == nq - 1)
        def _():
            cols = pl.ds(pl.multiple_of(h * 128, 128), 128)
            ck = pltpu.make_async_copy(dk_acc, dk_hbm.at[:, cols], sem.at[0])
            cv = pltpu.make_async_copy(dv_acc, dv_hbm.at[:, cols], sem.at[1])
            ck.start()
            cv.start()
            ck.wait()
            cv.wait()

    col = lambda k: pl.BlockSpec((tq, 128), lambda h, i: (i, N_C // 128 + 4 * h + k))
    whole = lambda k: pl.BlockSpec((T, 128), lambda h, i: (0, N_C // 128 + 4 * h + k))
    blk = pl.BlockSpec((tq, 128), lambda h, i: (i, h))
    return _call(
        body, name="sb_bwd", grid=(SB_HEADS, nq),
        in_specs=[col(0), col(1), whole(2), whole(3), blk, blk, _full((2, tq, tq)), _HBM],
        out_specs=[pl.BlockSpec((tq, 256), lambda h, i: (i, N_C // 256 + 2 * h)), _HBM, _HBM],
        out_shape=[jax.ShapeDtypeStruct(dp.shape, dp.dtype),
                   jax.ShapeDtypeStruct((T, 512), F32), jax.ShapeDtypeStruct((T, 512), F32)],
        scratch=[pltpu.VMEM((tq, 128), F32), pltpu.VMEM((tq, 1), F32), pltpu.VMEM((tq, 1), F32),
                 pltpu.VMEM((T, 128), F32), pltpu.VMEM((T, 128), F32), pltpu.SemaphoreType.DMA((2,))],
        aliases={7: 0},
    )(p, p, p, p, o, dy, tris, dp)


def _sb_kv_into_dp(dk, dv, dp):
    T = dk.shape[0]
    tm = min(1024, T)

    def body(dk_ref, dv_ref, _, o_ref):
        o_ref[:, 0:128] = dk_ref[...].astype(o_ref.dtype)
        o_ref[:, 128:256] = dv_ref[...].astype(o_ref.dtype)

    blk = pl.BlockSpec((tm, 128), lambda i, h: (i, h))
    return _call(
        body, name="sb_kv_into_dp", grid=(T // tm, SB_HEADS), in_specs=[blk, blk, _HBM],
        out_specs=pl.BlockSpec((tm, 256), lambda i, h: (i, N_C // 256 + 2 * h + 1)),
        out_shape=jax.ShapeDtypeStruct(dp.shape, dp.dtype), aliases={2: 0},
    )(dk, dv, dp)


def _post_fwd(ya, yb, yc, p, bm, wa, wb, wc, wo, g_post, x):
    T = x.shape[0]
    tm = min(512, T)

    def body(ya_ref, yb_ref, yc_ref, la_ref, lb_ref, lc_ref, bm_ref, wa_ref, wb_ref, wc_ref, wo_ref,
             g_ref, x_ref, m_ref, out_ref, xn_ref):
        merged = None
        for k, (y_ref, l_ref, w_ref) in enumerate(
                ((ya_ref, la_ref, wa_ref), (yb_ref, lb_ref, wb_ref), (yc_ref, lc_ref, wc_ref))):
            gate = _sigmoid(l_ref[...].astype(F32) + bm_ref[k:k + 1, :])
            term = gate * _dot(y_ref[...], w_ref[...])
            merged = term if merged is None else merged + term
        mb = merged.astype(BF16)
        m_ref[...] = mb
        out = _dot(mb, wo_ref[...])
        out_ref[...] = out
        r = lax.rsqrt(jnp.mean(out * out, axis=-1, keepdims=True) + EPS)
        xn_ref[...] = x_ref[...] + out * r * g_ref[...]

    yspec = pl.BlockSpec((tm, 512), lambda i: (i, 0))
    lspec = lambda k: pl.BlockSpec((tm, D), lambda i: (i, k))
    row = pl.BlockSpec((tm, D), lambda i: (i, 0))
    return _call(
        body, name="post_fwd", grid=(T // tm,),
        in_specs=[yspec, yspec, yspec, lspec(0), lspec(1), lspec(2), _full((3, D)),
                  _full((512, D)), _full((512, D)), _full((512, D)), _full((D, D)), _full((1, D)), row],
        out_specs=[row, row, row],
        out_shape=[jax.ShapeDtypeStruct((T, D), BF16), jax.ShapeDtypeStruct((T, D), F32),
                   jax.ShapeDtypeStruct((T, D), F32)],
    )(ya, yb, yc, p, p, p, bm, wa, wb, wc, wo, g_post, x)


def _post_bwd(dxn, out, ya, yb, yc, p, bm, wa, wb, wc, wo, g_post):
    T = dxn.shape[0]
    tm = min(256, T)

    def body(dxn_ref, out_ref, ya_ref, yb_ref, yc_ref, la_ref, lb_ref, lc_ref, bm_ref,
             wa_ref, wb_ref, wc_ref, wo_ref, g_ref,
             do_ref, dpa_ref, dpb_ref, dpc_ref, dl_ref, dya_ref, dyb_ref, dyc_ref, dg_ref, db_ref):
        @pl.when(pl.program_id(0) == 0)
        def _():
            dg_ref[...] = jnp.zeros_like(dg_ref)
            db_ref[...] = jnp.zeros_like(db_ref)

        out = out_ref[...]
        dxn_ = dxn_ref[...]
        r = lax.rsqrt(jnp.mean(out * out, axis=-1, keepdims=True) + EPS)
        a = dxn_ * g_ref[...]
        d_out = r * a - out * (r * r * r) * jnp.mean(a * out, axis=-1, keepdims=True)
        dg_ref[...] += jnp.sum(dxn_ * out * r, axis=0, keepdims=True)
        dob = d_out.astype(BF16)
        do_ref[...] = dob
        dmerged = _dot_nt(dob, wo_ref[...])
        for k, (y_ref, l_ref, w_ref, dp_ref, dy_ref) in enumerate((
                (ya_ref, la_ref, wa_ref, dpa_ref, dya_ref), (yb_ref, lb_ref, wb_ref, dpb_ref, dyb_ref),
                (yc_ref, lc_ref, wc_ref, dpc_ref, dyc_ref))):
            gate = _sigmoid(l_ref[...].astype(F32) + bm_ref[k:k + 1, :])
            proj = _dot(y_ref[...], w_ref[...])
            dproj = (dmerged * gate).astype(BF16)
            dp_ref[...] = dproj
            dlog = dmerged * proj * gate * (1.0 - gate)
            dl_ref[:, D * k:D * (k + 1)] = dlog.astype(dl_ref.dtype)
            db_ref[k:k + 1, :] += jnp.sum(dlog, axis=0, keepdims=True)
            dy_ref[...] = _dot_nt(dproj, w_ref[...]).astype(dy_ref.dtype)

    yspec = pl.BlockSpec((tm, 512), lambda i: (i, 0))
    lspec = lambda k: pl.BlockSpec((tm, D), lambda i: (i, k))
    row = pl.BlockSpec((tm, D), lambda i: (i, 0))
    sds = jax.ShapeDtypeStruct
    return _call(
        body, name="post_bwd", grid=(T // tm,),
        in_specs=[row, row, yspec, yspec, yspec, lspec(0), lspec(1), lspec(2), _full((3, D)),
                  _full((512, D)), _full((512, D)), _full((512, D)), _full((D, D)), _full((1, D))],
        out_specs=[row, row, row, row, pl.BlockSpec((tm, 3 * D), lambda i: (i, 0)), yspec, yspec, yspec,
                   _full((1, D)), _full((3, D))],
        out_shape=[sds((T, D), BF16), sds((T, D), BF16), sds((T, D), BF16), sds((T, D), BF16),
                   sds((T, IN_PAD), ACT), sds((T, 512), ACT), sds((T, 512), ACT), sds((T, 512), ACT),
                   sds((1, D), F32), sds((3, D), F32)],
    )(dxn, out, ya, yb, yc, p, p, p, bm, wa, wb, wc, wo, g_post)


def _loss_and_grad(y, target):
    T = y.shape[0]
    tm = min(1024, T)

    def body(y_ref, t_ref, l_ref, dy_ref):
        @pl.when(pl.program_id(0) == 0)
        def _():
            l_ref[...] = jnp.zeros_like(l_ref)
        e = y_ref[...] - t_ref[...]
        l_ref[...] += jnp.sum(e * e, axis=0, keepdims=True)
        dy_ref[...] = e * (1.0 / D)

    row = pl.BlockSpec((tm, D), lambda i: (i, 0))
    return _call(
        body, name="loss", grid=(T // tm,), in_specs=[row, row], out_specs=[_full((1, D)), row],
        out_shape=[jax.ShapeDtypeStruct((1, D), F32), jax.ShapeDtypeStruct((T, D), F32)],
    )(y, target)


def _sibling_swap(srcs, name):
    n = len(srcs)

    def body(*refs):
        s_refs, r_refs = refs[:n], refs[n:2 * n]
        send_sems, recv_sems = refs[2 * n:]
        x, y, c = lax.axis_index("x"), lax.axis_index("y"), lax.axis_index("c")
        copies = [pltpu.make_async_remote_copy(
            src_ref=s_refs[i], dst_ref=r_refs[i], send_sem=send_sems.at[i], recv_sem=recv_sems.at[i],
            device_id=(x, y, 1 - c), device_id_type=MESH) for i in range(n)]
        for cp in copies:
            cp.start()
        for cp in copies:
            cp.wait()

    return pl.pallas_call(
        body, name=name, in_specs=[_HBM] * n, out_specs=[_HBM] * n,
        out_shape=[jax.ShapeDtypeStruct(a.shape, a.dtype) for a in srcs],
        scratch_shapes=[pltpu.SemaphoreType.DMA((n,)), pltpu.SemaphoreType.DMA((n,))],
    )(*srcs)


_ELEMENTWISE_BLOCK_BYTES = 1 << 20


def _row_tile(rows, cols):
    if rows * cols * 4 <= 2 * _ELEMENTWISE_BLOCK_BYTES:
        return rows
    for tr in (2048, 1024, 512, 256, 128, 64, 32, 16, 8):
        if rows % tr == 0 and tr * cols * 4 <= _ELEMENTWISE_BLOCK_BYTES:
            return tr
    raise ValueError(f"no row tile for {(rows, cols)}")


def _sum_chips(r, name):
    _, R, C = r.shape
    tr = _row_tile(R, C)

    def body(r_ref, o_ref):
        f = lambda j: r_ref[j].astype(F32)
        o_ref[...] = ((f(0) + f(1)) + f(2)) + f(3)

    return _call(
        body, name=name, grid=(R // tr,),
        in_specs=[pl.BlockSpec((4, tr, C), lambda i: (0, i, 0))],
        out_specs=pl.BlockSpec((tr, C), lambda i: (i, 0)),
        out_shape=jax.ShapeDtypeStruct((R, C), F32),
    )(r)


def _adamw(w, m, v, g_mine, g_other, name):
    R, C = w.shape
    tr = _row_tile(R, C)

    def body(w_ref, m_ref, v_ref, a_ref, b_ref, g_ref, d_ref, nm_ref, nv_ref):
        g = a_ref[...] + b_ref[...]
        g_ref[...] = g
        m_new = ADAM_B1 * m_ref[...] + (1.0 - ADAM_B1) * g
        v_new = ADAM_B2 * v_ref[...] + (1.0 - ADAM_B2) * (g * g)
        nm_ref[...] = m_new
        nv_ref[...] = v_new
        m_hat = m_new / (1.0 - ADAM_B1 ** ADAM_STEP)
        v_hat = v_new / (1.0 - ADAM_B2 ** ADAM_STEP)
        d_ref[...] = -ADAM_LR * (m_hat / (jnp.sqrt(v_hat) + ADAM_EPS) + ADAM_WD * w_ref[...])

    blk = pl.BlockSpec((tr, C), lambda i: (i, 0))
    sds = jax.ShapeDtypeStruct((R, C), F32)
    return _call(body, name=name, grid=(R // tr,), in_specs=[blk] * 5, out_specs=[blk] * 4,
                 out_shape=[sds] * 4)(w, m, v, g_mine, g_other)


_NAMES = ('w_in', 'gm_w_s', 'gm_b_s', 'gm_norm_gain', 'sw_sinks', 'w_branch_a', 'w_branch_b',
          'w_branch_c', 'b_merge', 'w_out', 'g_pre', 'g_post')
_SMALL = ('gm_w_s', 'gm_b_s', 'gm_norm_gain', 'sw_sinks', 'g_pre', 'g_post')
_PACK_COLS = 1024


def _pack(arrays):
    flat = jnp.concatenate([a.reshape(-1).astype(F32) for a in arrays])
    rows = -(-flat.shape[0] // (8 * _PACK_COLS)) * 8
    return jnp.pad(flat, (0, rows * _PACK_COLS - flat.shape[0])).reshape(rows, _PACK_COLS)


def _unpack(buf, shapes):
    flat = buf.reshape(-1)
    out, off = [], 0
    for s in shapes:
        n = int(np.prod(s))
        out.append(flat[off:off + n].reshape(s))
        off += n
    return out


def _permute_cols(w):
    return jnp.concatenate(
        [jnp.zeros(w.shape[:-1] + (n,), w.dtype) if o is None else w[..., o:o + n] for o, n in _PERM], axis=-1)


def _unpermute_cols(w):
    new_off, off = {}, 0
    for o, n in _PERM:
        if o is not None:
            new_off[o] = (off, n)
        off += n
    return jnp.concatenate([w[..., new_off[o][0]:new_off[o][0] + new_off[o][1]] for o in sorted(new_off)], axis=-1)


def _tables(T):
    half = HEAD_DIM // 2
    freqs = ROPE_THETA ** (-jnp.arange(half, dtype=F32) / half)
    ang = jnp.arange(T).astype(F32)[:, None] * freqs[None, :]
    cos, sin = jnp.cos(ang), jnp.sin(ang)
    return (jnp.tile(jnp.concatenate([cos, cos], axis=1), (1, 2)),
            jnp.tile(jnp.concatenate([-sin, sin], axis=1), (1, 2)), _sb_tris(min(256, T)))


def _layer_fwd(x, lw, tables, carry=()):
    gpre, gpost = lw['g_pre'][None, :], lw['g_post'][None, :]
    bfull = jnp.broadcast_to(lw['bs'][:, :, None], (GM_GROUPS, BLOCK, BLOCK))
    gv = lw['gv'][None, :]
    p, h, *got = _in_proj(x, gpre, lw['w_in'], carry)
    ya = _gmlp_fwd(p, lw['ws'], bfull, gv)
    qkr = _rope_fwd(p, *tables[:2])
    yb = _swa_fwd(qkr, p, lw['sinks'])
    oc, yc = _sb_fwd(p, tables[2])
    merged, out, xn = _post_fwd(ya, yb, yc, p, lw['bm'], lw['wa'], lw['wb'], lw['wc'], lw['wo'], gpost, x)
    return xn, (lw, p, h, ya, qkr, yb, oc, yc, merged, out, bfull, gv, gpre, gpost), got


def _layer_bwd(x, saved, dxn, tables, carry=()):
    lw, p, h, ya, qkr, yb, oc, yc, merged, out, bfull, gv, gpre, gpost = saved
    g = {}
    (d_out, dpa, dpb, dpc, dp, dya, dyb, dyc, dgpost, dbm) = _post_bwd(
        dxn, out, ya, yb, yc, p, lw['bm'], lw['wa'], lw['wb'], lw['wc'], lw['wo'], gpost)
    g['w_out'] = _matmul_tn(merged, d_out, "grad_w_out")
    g['w_branch_a'] = _matmul_tn(ya, dpa, "grad_w_a")
    g['w_branch_b'] = _matmul_tn(yb, dpb, "grad_w_b")
    g['w_branch_c'] = _matmul_tn(yc, dpc, "grad_w_c")
    g['g_post'] = dgpost[0]
    g['b_merge'] = dbm
    dp, dws, dbacc, dgv = _gmlp_bwd(p, dya, lw['ws'], bfull, gv, dp)
    g['gm_w_s'] = dws
    g['gm_b_s'] = jnp.sum(dbacc.reshape(BLOCK, GM_GROUPS, BLOCK), axis=2).T
    g['gm_norm_gain'] = dgv[0]
    dp, dsink = _swa_bwd(qkr, p, dyb, lw['sinks'], *tables[:2], dp)
    g['sw_sinks'] = dsink[:, 0]
    dp, dkc, dvc = _sb_bwd(p, oc, dyc, tables[2], dp)
    dp = _sb_kv_into_dp(dkc, dvc, dp)
    g['w_in'] = _matmul_tn(h, dp, "grad_w_in")
    dx, dgpre, *got = _in_bwd(dp, lw['w_in'], x, dxn, gpre, carry)
    g['g_pre'] = dgpre[0]
    return dx, g, got


def kernel(x, w_in, gm_w_s, gm_b_s, gm_norm_gain, sw_sinks, w_branch_a, w_branch_b, w_branch_c, b_merge, w_out, g_pre, g_post, loss_target, m_w_in, m_gm_w_s, m_gm_b_s, m_gm_norm_gain, m_sw_sinks, m_w_branch_a, m_w_branch_b, m_w_branch_c, m_b_merge, m_w_out, m_g_pre, m_g_post, v_w_in, v_gm_w_s, v_gm_b_s, v_gm_norm_gain, v_sw_sinks, v_w_branch_a, v_w_branch_b, v_w_branch_c, v_b_merge, v_w_out, v_g_pre, v_g_post):
    T = x.shape[1]
    weights = dict(zip(_NAMES, (w_in, gm_w_s, gm_b_s, gm_norm_gain, sw_sinks, w_branch_a, w_branch_b,
                                w_branch_c, b_merge, w_out, g_pre, g_post)))
    mom_m = dict(zip(_NAMES, (m_w_in, m_gm_w_s, m_gm_b_s, m_gm_norm_gain, m_sw_sinks, m_w_branch_a,
                              m_w_branch_b, m_w_branch_c, m_b_merge, m_w_out, m_g_pre, m_g_post)))
    mom_v = dict(zip(_NAMES, (v_w_in, v_gm_w_s, v_gm_b_s, v_gm_norm_gain, v_sw_sinks, v_w_branch_a,
                              v_w_branch_b, v_w_branch_c, v_b_merge, v_w_out, v_g_pre, v_g_post)))
    abc = lambda d: jnp.stack([d['w_branch_a'], d['w_branch_b'], d['w_branch_c']], axis=1)

    local = [w_in.astype(BF16), abc(weights).astype(BF16), b_merge, w_out.astype(BF16)]
    shard = lambda l: [a[l] for a in local]

    def assemble(got):
        cat = lambda i, axis: jnp.concatenate([got[i][j] for j in range(4)], axis=axis)
        wabc = cat(1, 2)
        return dict(w_in=_permute_cols(cat(0, 1)), wa=wabc[0], wb=wabc[1], wc=wabc[2], bm=cat(2, 1), wo=cat(3, 0))

    tables = _tables(T)
    xs = [x[0]]
    saved = []
    got = _chip_exchange(shard(0), (False,) * 4, "gather_weights")
    for l in range(DEPTH):
        lw = dict(assemble(got), ws=gm_w_s[l], bs=gm_b_s[l], gv=gm_norm_gain[l], sinks=sw_sinks[l],
                  g_pre=g_pre[l], g_post=g_post[l])
        xn, sv, got = _layer_fwd(xs[l], lw, tables, shard(l + 1) if l + 1 < DEPTH else ())
        xs.append(xn)
        saved.append(sv)

    lsum, dxn = _loss_and_grad(xs[DEPTH], loss_target[0])
    loss = lax.psum(0.5 * jnp.sum(lsum) / D, ("x", "y", "c"))

    def partials(g):
        per_chip = lambda a, axis: jnp.stack(jnp.split(a, 4, axis=axis))
        g_abc = jnp.stack([g['w_branch_a'], g['w_branch_b'], g['w_branch_c']])
        return [per_chip(_unpermute_cols(g['w_in']), 1).astype(BF16),
                per_chip(g_abc, 2).astype(BF16),
                per_chip(g['b_merge'], 1),
                per_chip(g['w_out'], 0).astype(BF16)]

    small = {n: [None] * DEPTH for n in _SMALL}
    received = [None] * DEPTH
    pending = ()
    for l in reversed(range(DEPTH)):
        dxn, g, got = _layer_bwd(xs[l], saved[l], dxn, tables, pending)
        if pending:
            received[l + 1] = got
        pending = partials(g)
        for n in _SMALL:
            small[n][l] = g[n]
    grad_x = dxn[None]
    g_small = _pack([jnp.stack(small[n]) for n in _SMALL])
    got = _chip_exchange(pending + [g_small], (True,) * 4 + (False,), "scatter_grads")
    received[0] = got[:4]

    views = [(4 * D, IN_WIDTH // 4), (4 * 3 * 512, D // 4), (4 * 3, D // 4), (D, D), g_small.shape]
    stacks = [jnp.stack([received[l][i] for l in range(DEPTH)], axis=1) for i in range(4)] + [got[4]]
    sums = [_sum_chips(r.reshape((4,) + v), f"sum_chips_{i}") for i, (r, v) in enumerate(zip(stacks, views))]
    others = _sibling_swap(sums, "swap_core_sums")
    tensors = [lambda d: d['w_in'], abc, lambda d: d['b_merge'], lambda d: d['w_out'],
               lambda d: _pack([d[n] for n in _SMALL])]
    res = [_adamw(t(weights).reshape(v), t(mom_m).reshape(v), t(mom_v).reshape(v), s_, o_, f"adamw_{i}")
           for i, (t, v, s_, o_) in enumerate(zip(tensors, views, sums, others))]

    outs = []
    for kind in range(4):
        r = [res[i][kind] for i in range(5)]
        r_abc = r[1].reshape(4, 3, 512, D // 4)
        tiny = dict(zip(_SMALL, _unpack(r[4], [weights[n].shape for n in _SMALL])))
        big = dict(w_in=r[0].reshape(w_in.shape), w_branch_a=r_abc[:, 0], w_branch_b=r_abc[:, 1],
                   w_branch_c=r_abc[:, 2], b_merge=r[2].reshape(b_merge.shape), w_out=r[3].reshape(w_out.shape))
        outs.extend(big[n] if n in big else tiny[n] for n in _NAMES)
    return (loss, grad_x, *outs)
```

```python
import functools

import numpy as np
import jax
import jax.numpy as jnp
from jax import lax
from jax.experimental import pallas as pl
from jax.experimental.pallas import tpu as pltpu

F32 = jnp.float32
BF16 = jnp.bfloat16
ACT = jnp.bfloat16

D = 1024
DEPTH = 4
BLOCK = 128
EPS = 1e-6
NEG = -1e30
GM_GROUPS = 4
GM_WIDTH = 512
HEAD_DIM = 64
SW_HEADS = 8
SB_HEADS = 4
SB_HEAD_DIM = 128
ROPE_THETA = 10000.0
IN_WIDTH = 7936
IN_PAD = 8192

O_UA, O_VA, O_GA, O_QB, O_KB, O_VB, O_GB = 0, 512, 1024, 1536, 2048, 2176, 2304
O_QC, O_KC, O_VC, O_GC, O_MG = 2816, 3328, 3840, 4352, 4864
_PERM = ((O_MG, 3072), (O_UA, 512), (O_VA, 512), (O_GA, 512),
         (O_QB, 512), (O_KB, 128), (O_VB, 128), (None, IN_PAD - IN_WIDTH), (O_GB, 512)) + tuple(
    (o + 128 * h, 128) for h in range(4) for o in (O_QC, O_GC, O_KC, O_VC))
N_MG, N_UA, N_VA, N_GA = 0, 3072, 3584, 4096
N_QB, N_KB, N_VB, N_GB = 4608, 5120, 5248, 5632
N_C = 6144

ADAM_LR, ADAM_B1, ADAM_B2, ADAM_EPS, ADAM_WD, ADAM_STEP = 0.001, 0.9, 0.999, 1e-08, 0.01, 10

SB_EXIT = -104.0
V7X_VMEM_LIMIT = 48 * 1024 * 1024
MESH = pl.DeviceIdType.MESH


_HBM = pl.BlockSpec(memory_space=pl.ANY)


def _call(body, *, name, grid, in_specs, out_specs, out_shape, scratch=(), aliases=None):
    return pl.pallas_call(
        body, name=name, grid=grid, in_specs=in_specs, out_specs=out_specs, out_shape=out_shape,
        scratch_shapes=list(scratch), input_output_aliases=aliases or {},
        compiler_params=pltpu.CompilerParams(
            dimension_semantics=("arbitrary",) * len(grid), vmem_limit_bytes=V7X_VMEM_LIMIT))


def _sigmoid(x):
    return 1.0 / (1.0 + jnp.exp(-x))


def _silu_and_grad(x):
    s = _sigmoid(x)
    return x * s, s * (1.0 + x * (1.0 - s))


def _dot(a, b):
    return jnp.dot(a, b, preferred_element_type=F32)


def _dot_nt(a, b):
    return lax.dot_general(a, b, (((1,), (1,)), ((), ())), preferred_element_type=F32)


def _dot_tn(a, b):
    return lax.dot_general(a, b, (((0,), (0,)), ((), ())), preferred_element_type=F32)


def _full(shape):
    return pl.BlockSpec(shape, lambda *_: (0,) * len(shape))


_CHIP_STEPS = ((1, 0), (0, 1), (1, 1))


def _exchange_ops(s_refs, r_refs, send_sems, recv_sems, local_sems, per_target):
    n = len(s_refs)

    def copies():
        x, y, c = lax.axis_index("x"), lax.axis_index("y"), lax.axis_index("c")
        me = 2 * x + y
        pick = lambda i, j: s_refs[i].at[j] if per_target[i] else s_refs[i]
        own = [pltpu.make_async_copy(pick(i, me), r_refs[i].at[me], local_sems.at[i]) for i in range(n)]
        sent, arriving = [], []
        for k, (dx, dy) in enumerate(_CHIP_STEPS):
            tx, ty = (x + dx) % 2, (y + dy) % 2
            peer = 2 * tx + ty
            for i in range(n):
                sems = dict(send_sem=send_sems.at[3 * i + k], recv_sem=recv_sems.at[3 * i + k])
                sent.append(pltpu.make_async_remote_copy(
                    src_ref=pick(i, peer), dst_ref=r_refs[i].at[me], device_id=(tx, ty, c),
                    device_id_type=MESH, **sems))
                arriving.append(pltpu.make_async_remote_copy(
                    src_ref=pick(i, peer), dst_ref=r_refs[i].at[peer], device_id=(x, y, c),
                    device_id_type=MESH, **sems))
        return own, sent, arriving

    def start():
        own, sent, _ = copies()
        for cp in own + sent:
            cp.start()

    def wait():
        own, sent, arriving = copies()
        for cp in arriving:
            cp.wait_recv()
        for cp in sent:
            cp.wait_send()
        for cp in own:
            cp.wait()

    return start, wait


def _exchange_shapes(srcs, per_target):
    return [jax.ShapeDtypeStruct((4,) + (a.shape[1:] if pt else a.shape), a.dtype)
            for a, pt in zip(srcs, per_target)]


def _exchange_sems(n):
    return [pltpu.SemaphoreType.DMA((3 * n,)), pltpu.SemaphoreType.DMA((3 * n,)), pltpu.SemaphoreType.DMA((n,))]


def _chip_exchange(srcs, per_target, name):
    n = len(srcs)

    def body(*refs):
        start, wait = _exchange_ops(refs[:n], refs[n:2 * n], *refs[2 * n:], per_target)
        start()
        wait()

    return pl.pallas_call(
        body, name=name, in_specs=[_HBM] * n, out_specs=[_HBM] * n, out_shape=_exchange_shapes(srcs, per_target),
        scratch_shapes=_exchange_sems(n),
    )(*srcs)


def _in_proj(x, g_pre, w_in, carry=()):
    T = x.shape[0]
    tm, tn = min(2048, T), 512
    ni, nj, n = T // tm, IN_PAD // tn, len(carry)

    def body(x_ref, g_ref, w_ref, *rest):
        p_ref, h_ref = rest[n], rest[n + 1]
        i, j = pl.program_id(0), pl.program_id(1)
        if n:
            start, wait = _exchange_ops(rest[:n], rest[n + 2:2 * n + 2], *rest[2 * n + 2:], (False,) * n)
            pl.when((i == 0) & (j == 0))(start)

        @pl.when(j == 0)
        def _():
            xf = x_ref[...]
            r = lax.rsqrt(jnp.mean(xf * xf, axis=-1, keepdims=True) + EPS)
            h_ref[...] = (xf * r * g_ref[...]).astype(BF16)
        p_ref[...] = _dot(h_ref[...], w_ref[...]).astype(p_ref.dtype)
        if n:
            pl.when((i == ni - 1) & (j == nj - 1))(wait)

    return _call(
        body, name="in_proj_gather" if n else "in_proj", grid=(ni, nj),
        in_specs=[pl.BlockSpec((tm, D), lambda i, j: (i, 0)), _full((1, D)),
                  pl.BlockSpec((D, tn), lambda i, j: (0, j))] + [_HBM] * n,
        out_specs=[pl.BlockSpec((tm, tn), lambda i, j: (i, j)), pl.BlockSpec((tm, D), lambda i, j: (i, 0))]
        + [_HBM] * n,
        out_shape=[jax.ShapeDtypeStruct((T, IN_PAD), ACT), jax.ShapeDtypeStruct((T, D), BF16)]
        + _exchange_shapes(carry, (False,) * n),
        scratch=_exchange_sems(n) if n else (),
    )(x, g_pre, w_in, *carry)


def _in_bwd(dp, w_in, x, dxn, g_pre, carry=()):
    T = x.shape[0]
    tm, tk = min(1024, T), 1024
    ni, nk, n = T // tm, IN_PAD // tk, len(carry)

    def body(dp_ref, w_ref, x_ref, dxn_ref, g_ref, *rest):
        dx_ref, dg_ref, acc = rest[n], rest[n + 1], rest[2 * n + 2]
        i, k = pl.program_id(0), pl.program_id(1)
        if n:
            start, wait = _exchange_ops(rest[:n], rest[n + 2:2 * n + 2], *rest[2 * n + 3:], (True,) * n)
            pl.when((i == 0) & (k == 0))(start)

        @pl.when(k == 0)
        def _():
            acc[...] = jnp.zeros_like(acc)

        @pl.when((i == 0) & (k == 0))
        def _():
            dg_ref[...] = jnp.zeros_like(dg_ref)

        acc[...] += _dot_nt(dp_ref[...], w_ref[...])

        @pl.when(k == nk - 1)
        def _():
            dh = acc[...]
            xf = x_ref[...]
            r = lax.rsqrt(jnp.mean(xf * xf, axis=-1, keepdims=True) + EPS)
            a = dh * g_ref[...]
            dx_ref[...] = dxn_ref[...] + r * a - xf * (r * r * r) * jnp.mean(a * xf, axis=-1, keepdims=True)
            dg_ref[...] += jnp.sum(dh * xf * r, axis=0, keepdims=True)

        if n:
            pl.when((i == ni - 1) & (k == nk - 1))(wait)

    return _call(
        body, name="in_bwd_scatter" if n else "in_bwd", grid=(ni, nk),
        in_specs=[pl.BlockSpec((tm, tk), lambda i, k: (i, k)), pl.BlockSpec((D, tk), lambda i, k: (0, k)),
                  pl.BlockSpec((tm, D), lambda i, k: (i, 0)), pl.BlockSpec((tm, D), lambda i, k: (i, 0)),
                  _full((1, D))] + [_HBM] * n,
        out_specs=[pl.BlockSpec((tm, D), lambda i, k: (i, 0)), _full((1, D))] + [_HBM] * n,
        out_shape=[jax.ShapeDtypeStruct((T, D), F32), jax.ShapeDtypeStruct((1, D), F32)]
        + _exchange_shapes(carry, (True,) * n),
        scratch=[pltpu.VMEM((tm, D), F32)] + (_exchange_sems(n) if n else []),
    )(dp, w_in, x, dxn, g_pre, *carry)


def _matmul_tn(a, b, name):
    T, K = a.shape
    N = b.shape[1]
    tk, tn, tt = min(K, 1024), min(N, 2048), min(T, 512)

    def body(a_ref, b_ref, o_ref):
        @pl.when(pl.program_id(2) == 0)
        def _():
            o_ref[...] = jnp.zeros_like(o_ref)
        o_ref[...] += _dot_tn(a_ref[...], b_ref[...])

    return _call(
        body, name=name, grid=(K // tk, N // tn, T // tt),
        in_specs=[pl.BlockSpec((tt, tk), lambda i, j, t: (t, i)), pl.BlockSpec((tt, tn), lambda i, j, t: (t, j))],
        out_specs=pl.BlockSpec((tk, tn), lambda i, j, t: (i, j)),
        out_shape=jax.ShapeDtypeStruct((K, N), F32),
    )(a, b)


def _gm_forward_parts(v_ref, gv_ref, ws_ref, bf_ref, nch):
    vf = v_ref[...].astype(F32)
    mu = jnp.mean(vf, axis=-1, keepdims=True)
    xc = vf - mu
    rstd = lax.rsqrt(jnp.mean(xc * xc, axis=-1, keepdims=True) + EPS)
    xhat = xc * rstd
    vnb = (xhat * gv_ref[...]).astype(BF16)
    row = lax.broadcasted_iota(jnp.int32, (BLOCK, BLOCK), 0)
    col = lax.broadcasted_iota(jnp.int32, (BLOCK, BLOCK), 1)
    mixed, vcats, wgs = [], [], []
    for g in range(GM_GROUPS):
        vg = vnb[:, BLOCK * g:BLOCK * (g + 1)]
        vcat = jnp.concatenate([vg[BLOCK * k:BLOCK * (k + 1), :] for k in range(nch)], axis=1)
        wg = jnp.where(row >= col, ws_ref[g], 0.0)
        m = _dot(wg.astype(BF16), vcat)
        mixed.append(jnp.concatenate(
            [m[:, BLOCK * k:BLOCK * (k + 1)] + bf_ref[g] for k in range(nch)], axis=0))
        vcats.append(vcat)
        wgs.append(wg)
    return xhat, rstd, jnp.concatenate(mixed, axis=1), vcats, wgs, row >= col


def _gmlp_fwd(p, ws, bfull, gv):
    T = p.shape[0]
    tm = min(512, T)
    nch = tm // BLOCK

    def body(u_ref, v_ref, gt_ref, ws_ref, bf_ref, gv_ref, y_ref):
        _, _, mixed, _, _, _ = _gm_forward_parts(v_ref, gv_ref, ws_ref, bf_ref, nch)
        sg, _ = _silu_and_grad(gt_ref[...].astype(F32))
        y_ref[...] = (u_ref[...].astype(F32) * mixed * sg).astype(y_ref.dtype)

    seg = lambda off: pl.BlockSpec((tm, 512), lambda i: (i, off // 512))
    return _call(
        body, name="gmlp_fwd", grid=(T // tm,),
        in_specs=[seg(N_UA), seg(N_VA), seg(N_GA), _full((GM_GROUPS, BLOCK, BLOCK)),
                  _full((GM_GROUPS, BLOCK, BLOCK)), _full((1, GM_WIDTH))],
        out_specs=pl.BlockSpec((tm, 512), lambda i: (i, 0)),
        out_shape=jax.ShapeDtypeStruct((T, GM_WIDTH), BF16),
    )(p, p, p, ws, bfull, gv)


def _gmlp_bwd(p, dy, ws, bfull, gv, dp):
    T = p.shape[0]
    tm = min(512, T)
    nch = tm // BLOCK

    def body(u_ref, v_ref, gt_ref, dy_ref, ws_ref, bf_ref, gv_ref, _, dp_ref, dws_ref, db_ref, dgv_ref):
        @pl.when(pl.program_id(0) == 0)
        def _():
            dws_ref[...] = jnp.zeros_like(dws_ref)
            db_ref[...] = jnp.zeros_like(db_ref)
            dgv_ref[...] = jnp.zeros_like(dgv_ref)

        xhat, rstd, mixed, vcats, wgs, tril = _gm_forward_parts(v_ref, gv_ref, ws_ref, bf_ref, nch)
        u = u_ref[...].astype(F32)
        gt = gt_ref[...].astype(F32)
        dyf = dy_ref[...].astype(F32)
        sg, dsg = _silu_and_grad(gt)
        du = dyf * mixed * sg
        dmixed = dyf * u * sg
        dgate = dyf * (u * mixed) * dsg
        dvn = []
        for g in range(GM_GROUPS):
            dmg = dmixed[:, BLOCK * g:BLOCK * (g + 1)]
            chunks = [dmg[BLOCK * k:BLOCK * (k + 1), :] for k in range(nch)]
            dmcat = jnp.concatenate(chunks, axis=1).astype(BF16)
            dws_ref[g] += jnp.where(tril, _dot_nt(dmcat, vcats[g]), 0.0)
            dvcat = _dot(wgs[g].T.astype(BF16), dmcat)
            dvn.append(jnp.concatenate([dvcat[:, BLOCK * k:BLOCK * (k + 1)] for k in range(nch)], axis=0))
            db_ref[:, BLOCK * g:BLOCK * (g + 1)] += functools.reduce(lambda a, b: a + b, chunks)
        dvn = jnp.concatenate(dvn, axis=1)
        dgv_ref[...] += jnp.sum(dvn * xhat, axis=0, keepdims=True)
        dxh = dvn * gv_ref[...]
        dv = rstd * (dxh - jnp.mean(dxh, axis=-1, keepdims=True)
                     - xhat * jnp.mean(dxh * xhat, axis=-1, keepdims=True))
        dp_ref[:, 0:512] = du.astype(dp_ref.dtype)
        dp_ref[:, 512:1024] = dv.astype(dp_ref.dtype)
        dp_ref[:, 1024:1536] = dgate.astype(dp_ref.dtype)

    seg = lambda off: pl.BlockSpec((tm, 512), lambda i: (i, off // 512))
    return _call(
        body, name="gmlp_bwd", grid=(T // tm,),
        in_specs=[seg(N_UA), seg(N_VA), seg(N_GA), pl.BlockSpec((tm, 512), lambda i: (i, 0)),
                  _full((GM_GROUPS, BLOCK, BLOCK)), _full((GM_GROUPS, BLOCK, BLOCK)), _full((1, GM_WIDTH)), _HBM],
        out_specs=[pl.BlockSpec((tm, 1536), lambda i: (i, N_UA // 1536)), _full((GM_GROUPS, BLOCK, BLOCK)),
                   _full((BLOCK, GM_WIDTH)), _full((1, GM_WIDTH))],
        out_shape=[jax.ShapeDtypeStruct(dp.shape, dp.dtype), jax.ShapeDtypeStruct((GM_GROUPS, BLOCK, BLOCK), F32),
                   jax.ShapeDtypeStruct((BLOCK, GM_WIDTH), F32), jax.ShapeDtypeStruct((1, GM_WIDTH), F32)],
        aliases={7: 0},
    )(p, p, p, dy, ws, bfull, gv, dp)


def _swap_halves(x):
    lane = lax.broadcasted_iota(jnp.int32, x.shape, 1) % HEAD_DIM
    return jnp.where(lane < HEAD_DIM // 2, pltpu.roll(x, 96, 1), pltpu.roll(x, 32, 1))


def _rope_fwd(p, cos_t, sin_t):
    T = p.shape[0]
    tm = min(512, T)

    def body(q_ref, k_ref, c_ref, s_ref, o_ref):
        c, s = c_ref[...], s_ref[...]
        for G in range(5):
            xg = (q_ref[:, 128 * G:128 * (G + 1)] if G < 4 else k_ref[...]).astype(F32)
            o_ref[:, 128 * G:128 * (G + 1)] = (xg * c + _swap_halves(xg) * s).astype(o_ref.dtype)

    return _call(
        body, name="rope_fwd", grid=(T // tm,),
        in_specs=[pl.BlockSpec((tm, 512), lambda i: (i, N_QB // 512)),
                  pl.BlockSpec((tm, 128), lambda i: (i, N_KB // 128)),
                  pl.BlockSpec((tm, 128), lambda i: (i, 0)), pl.BlockSpec((tm, 128), lambda i: (i, 0))],
        out_specs=pl.BlockSpec((tm, 640), lambda i: (i, 0)),
        out_shape=jax.ShapeDtypeStruct((T, 640), BF16),
    )(p, p, cos_t, sin_t)


def _unrotate(d, c, s):
    return d * c + _swap_halves(d * s)


def _dup_heads(x):
    left = lax.broadcasted_iota(jnp.int32, x.shape, 1) < HEAD_DIM
    r = pltpu.roll(x, HEAD_DIM, 1)
    return jnp.where(left, x, r), jnp.where(left, r, x)


def _fold_heads(acc0, acc1):
    left = lax.broadcasted_iota(jnp.int32, acc0.shape, 1) < HEAD_DIM
    t0 = acc0 + pltpu.roll(acc0, HEAD_DIM, 1)
    t1 = acc1 + pltpu.roll(acc1, HEAD_DIM, 1)
    return jnp.where(left, t0, t1)


def _swa_valid_t(base):
    kpos = base - BLOCK + lax.broadcasted_iota(jnp.int32, (2 * BLOCK, BLOCK), 0)
    qpos = base + lax.broadcasted_iota(jnp.int32, (2 * BLOCK, BLOCK), 1)
    return jnp.logical_and(kpos >= 0, jnp.logical_and(kpos <= qpos, kpos > qpos - BLOCK))


def _swa_probs_t(qm, kk, valid_t, sink):
    s = jnp.where(valid_t, _dot_nt(kk, qm) * (HEAD_DIM ** -0.5), NEG)
    m = jnp.maximum(jnp.max(s, axis=0, keepdims=True), sink)
    e = jnp.exp(s - m)
    es = jnp.exp(sink - m)
    inv = 1.0 / (jnp.sum(e, axis=0, keepdims=True) + es)
    return e * inv, es * inv


def _swa_operands(kh_ref, kc_ref, vh_ref, vc_ref, nsb):
    out = []
    for h_ref, c_ref in ((kh_ref, kc_ref), (vh_ref, vc_ref)):
        dup = _dup_heads(jnp.concatenate([h_ref[...], c_ref[...]], axis=0).astype(F32))
        out.append([d.astype(BF16) for d in dup])
        out.append([[d[BLOCK * c:BLOCK * (c + 1), :].T.astype(BF16) for c in range(nsb + 1)] for d in dup])
    return out


def _halves(x):
    left = lax.broadcasted_iota(jnp.int32, x.shape, 1) < HEAD_DIM
    zero = jnp.zeros_like(x)
    return jnp.where(left, x, zero), jnp.where(left, zero, x)


def _swa_specs(T, bq, rev):
    n = T // bq
    blk = (lambda i: n - 1 - i) if rev else (lambda i: i)
    halo = lambda i: jnp.maximum(blk(i) * (bq // BLOCK) - 1, 0)
    return blk, [
        pl.BlockSpec(memory_space=pltpu.SMEM),
        pl.BlockSpec((bq, 512), lambda i: (blk(i), 0)),
        pl.BlockSpec((bq, 128), lambda i: (blk(i), 4)),
        pl.BlockSpec((BLOCK, 128), lambda i: (halo(i), 4)),
        pl.BlockSpec((bq, 128), lambda i: (blk(i), N_VB // 128)),
        pl.BlockSpec((BLOCK, 128), lambda i: (halo(i), N_VB // 128)),
        pl.BlockSpec((bq, 512), lambda i: (blk(i), N_GB // 512)),
    ]


def _swa_fwd(qkr, p, sinks):
    T = p.shape[0]
    bq = min(512, T)
    nsb = bq // BLOCK
    blk, specs = _swa_specs(T, bq, False)

    def body(sink_ref, q_ref, kc_ref, kh_ref, vc_ref, vh_ref, gt_ref, y_ref):
        base = blk(pl.program_id(0)) * bq
        kk, _, _, vT = _swa_operands(kh_ref, kc_ref, vh_ref, vc_ref, nsb)
        top = lax.broadcasted_iota(jnp.int32, (BLOCK, BLOCK), 0) < HEAD_DIM
        for sb in range(nsb):
            rows = slice(sb * BLOCK, (sb + 1) * BLOCK)
            keys = slice(sb * BLOCK, (sb + 2) * BLOCK)
            valid_t = _swa_valid_t(base + sb * BLOCK)
            for G in range(4):
                g = G // 2
                cols = slice(128 * G, 128 * (G + 1))
                vvt = jnp.concatenate([vT[g][sb], vT[g][sb + 1]], axis=1)
                o_t = []
                for hh, qm in enumerate(_halves(q_ref[rows, cols])):
                    pr, _ = _swa_probs_t(qm, kk[g][keys], valid_t, sink_ref[2 * G + hh])
                    o_t.append(_dot(vvt, pr.astype(BF16)))
                o = jnp.where(top, o_t[0], o_t[1]).T
                sg, _ = _silu_and_grad(gt_ref[rows, cols].astype(F32))
                y_ref[rows, cols] = (o * sg).astype(y_ref.dtype)

    return _call(
        body, name="swa_fwd", grid=(T // bq,), in_specs=specs,
        out_specs=pl.BlockSpec((bq, 512), lambda i: (i, 0)),
        out_shape=jax.ShapeDtypeStruct((T, 512), BF16),
    )(sinks, qkr, qkr, qkr, p, p, p)


def _swa_bwd(qkr, p, dy, sinks, cos_t, sin_t, dp):
    T = p.shape[0]
    bq = min(512, T)
    nsb = bq // BLOCK
    blk, specs = _swa_specs(T, bq, True)

    def body(sink_ref, q_ref, kc_ref, kh_ref, vc_ref, vh_ref, gt_ref, dy_ref, c_ref, s_ref, _,
             dp_ref, ds_ref, dk_acc, dv_acc, k_carry, v_carry):
        @pl.when(pl.program_id(0) == 0)
        def _():
            k_carry[...] = jnp.zeros_like(k_carry)
            v_carry[...] = jnp.zeros_like(v_carry)
            ds_ref[...] = jnp.zeros_like(ds_ref)

        base = blk(pl.program_id(0)) * bq
        dk_acc[...] = jnp.zeros_like(dk_acc)
        dv_acc[...] = jnp.zeros_like(dv_acc)
        kk, kT, vv, vT = _swa_operands(kh_ref, kc_ref, vh_ref, vc_ref, nsb)
        top = lax.broadcasted_iota(jnp.int32, (BLOCK, BLOCK), 0) < HEAD_DIM
        for sb in range(nsb):
            rows = slice(sb * BLOCK, (sb + 1) * BLOCK)
            keys = slice(sb * BLOCK, (sb + 2) * BLOCK)
            valid_t = _swa_valid_t(base + sb * BLOCK)
            dkp = [jnp.zeros((2 * BLOCK, BLOCK), F32)] * 2
            dvp = [jnp.zeros((2 * BLOCK, BLOCK), F32)] * 2
            for G in range(4):
                g = G // 2
                cols = slice(128 * G, 128 * (G + 1))
                kkt = jnp.concatenate([kT[g][sb], kT[g][sb + 1]], axis=1)
                vvt = jnp.concatenate([vT[g][sb], vT[g][sb + 1]], axis=1)
                qms = _halves(q_ref[rows, cols])
                prs, pss, o_t = [], [], []
                for hh in range(2):
                    pr, ps = _swa_probs_t(qms[hh], kk[g][keys], valid_t, sink_ref[2 * G + hh])
                    prs.append(pr)
                    pss.append(ps)
                    o_t.append(_dot(vvt, pr.astype(BF16)))
                o = jnp.where(top, o_t[0], o_t[1]).T
                sg, dsg = _silu_and_grad(gt_ref[rows, cols].astype(F32))
                dyf = dy_ref[rows, cols].astype(F32)
                do = dyf * sg
                dp_ref[rows, 1024 + 128 * G:1024 + 128 * (G + 1)] = (dyf * o * dsg).astype(dp_ref.dtype)
                do_t = do.T
                doms = _halves(do.astype(BF16))
                dq_t = []
                for hh in range(2):
                    dom_t = jnp.where(top if hh == 0 else jnp.logical_not(top), do_t, 0.0).astype(BF16)
                    dpv = _dot(vv[g][keys], dom_t)
                    delta = jnp.sum(prs[hh] * dpv, axis=0, keepdims=True)
                    dsb = (prs[hh] * (dpv - delta) * (HEAD_DIM ** -0.5)).astype(BF16)
                    h = 2 * G + hh
                    ds_ref[h:h + 1, :] += jnp.broadcast_to(
                        -jnp.sum(pss[hh] * delta, axis=1, keepdims=True), (1, 128))
                    dq_t.append(_dot(kkt, dsb))
                    dkp[g] = dkp[g] + _dot(dsb, qms[hh])
                    dvp[g] = dvp[g] + _dot(prs[hh].astype(BF16), doms[hh])
                dq = jnp.where(top, dq_t[0], dq_t[1]).T
                dp_ref[rows, cols] = _unrotate(dq, c_ref[rows, :], s_ref[rows, :]).astype(dp_ref.dtype)
            dk_acc[keys, :] += _fold_heads(dkp[0], dkp[1])
            dv_acc[keys, :] += _fold_heads(dvp[0], dvp[1])
        dk_acc[bq:bq + BLOCK, :] += k_carry[...]
        dv_acc[bq:bq + BLOCK, :] += v_carry[...]
        dk = _unrotate(dk_acc[BLOCK:bq + BLOCK, :], c_ref[...], s_ref[...])
        dp_ref[:, 512:640] = dk.astype(dp_ref.dtype)
        dp_ref[:, 640:768] = dv_acc[BLOCK:bq + BLOCK, :].astype(dp_ref.dtype)
        dp_ref[:, 768:1024] = jnp.zeros((bq, 256), dp_ref.dtype)
        k_carry[...] = dk_acc[0:BLOCK, :]
        v_carry[...] = dv_acc[0:BLOCK, :]

    rowblk = lambda w: pl.BlockSpec((bq, w), lambda i: (blk(i), 0))
    return _call(
        body, name="swa_bwd", grid=(T // bq,), in_specs=specs + [rowblk(512), rowblk(128), rowblk(128), _HBM],
        out_specs=[pl.BlockSpec((bq, 1536), lambda i: (blk(i), N_QB // 1536)), _full((SW_HEADS, 128))],
        out_shape=[jax.ShapeDtypeStruct(dp.shape, dp.dtype), jax.ShapeDtypeStruct((SW_HEADS, 128), F32)],
        scratch=[pltpu.VMEM((bq + BLOCK, 128), F32), pltpu.VMEM((bq + BLOCK, 128), F32),
                 pltpu.VMEM((BLOCK, 128), F32), pltpu.VMEM((BLOCK, 128), F32)],
        aliases={10: 0},
    )(sinks, qkr, qkr, qkr, p, p, p, dy, cos_t, sin_t, dp)


def _split_dot(x, tri):
    hi = x.astype(BF16)
    lo = (x - hi.astype(F32)).astype(BF16)
    return _dot(hi, tri) + _dot(lo, tri)


def _sb_logits(qs, kj):
    z = _dot_nt(qs, kj)
    e = jnp.exp(-jnp.abs(z))
    return z, e, -(jnp.maximum(z, 0.0) + jnp.log(1.0 + e))


def _sb_before(tq, qpos0, kpos0):
    qpos = qpos0 + lax.broadcasted_iota(jnp.int32, (tq, tq), 0)
    kpos = kpos0 + lax.broadcasted_iota(jnp.int32, (tq, tq), 1)
    return kpos < qpos


def _sb_tris(kb):
    row = lax.broadcasted_iota(jnp.int32, (kb, kb), 0)
    col = lax.broadcasted_iota(jnp.int32, (kb, kb), 1)
    return jnp.stack([jnp.where(row >= col, 1.0, 0.0), jnp.where(row > col, 1.0, 0.0)]).astype(BF16)


def _sb_window(qi, w, tq):
    jl = qi - 2 * w
    live = jl >= 1
    rows_l = pl.ds(pl.multiple_of(jl * tq, tq), tq)
    rows_e = pl.ds(pl.multiple_of(jnp.maximum(jl - 1, 0) * tq, tq), tq)
    return jl, live, rows_l, rows_e


def _sb_fwd(p, tris):
    T = p.shape[0]
    tq = min(256, T)
    scale = SB_HEAD_DIM ** -0.5

    def body(q_ref, gt_ref, k_ref, v_ref, tri_ref, o_ref, y_ref, acc, r_ref):
        qi = pl.program_id(1)
        qs = (q_ref[...].astype(F32) * scale).astype(BF16)
        acc[...] = jnp.zeros_like(acc)
        r_ref[...] = jnp.zeros_like(r_ref)

        def step(carry):
            w, _ = carry
            jl, live, rows_l, rows_e = _sb_window(qi, w, tq)
            r0 = r_ref[...]
            before = _sb_before(tq, qi * tq, jl * tq)
            z1, _, lf1 = _sb_logits(qs, k_ref[rows_l, :])
            lf1 = jnp.where(before, lf1, 0.0)
            t1 = jnp.sum(lf1, axis=-1, keepdims=True)
            a1 = jnp.where(before, jnp.exp(z1 + _split_dot(lf1, tri_ref[0]) + r0), 0.0)
            z2, _, lf2 = _sb_logits(qs, k_ref[rows_e, :])
            lf2 = jnp.where(live, lf2, 0.0)
            a2 = jnp.where(live, jnp.exp(z2 + _split_dot(lf2, tri_ref[0]) + (r0 + t1)), 0.0)
            acc[...] += _dot(a1.astype(BF16), v_ref[rows_l, :]) + _dot(a2.astype(BF16), v_ref[rows_e, :])
            r_new = r0 + t1 + jnp.sum(lf2, axis=-1, keepdims=True)
            r_ref[...] = r_new
            return w + 1, jnp.max(r_new) > SB_EXIT

        lax.while_loop(lambda c: jnp.logical_and(qi - 2 * c[0] >= 0, c[1]), step, (0, True))
        o = acc[...]
        o_ref[...] = o
        sg, _ = _silu_and_grad(gt_ref[...].astype(F32))
        y_ref[...] = (o * sg).astype(y_ref.dtype)

    col = lambda k: pl.BlockSpec((tq, 128), lambda h, i: (i, N_C // 128 + 4 * h + k))
    whole = lambda k: pl.BlockSpec((T, 128), lambda h, i: (0, N_C // 128 + 4 * h + k))
    out = pl.BlockSpec((tq, 128), lambda h, i: (i, h))
    return _call(
        body, name="sb_fwd", grid=(SB_HEADS, T // tq),
        in_specs=[col(0), col(1), whole(2), whole(3), _full((2, tq, tq))],
        out_specs=[out, out],
        out_shape=[jax.ShapeDtypeStruct((T, 512), F32), jax.ShapeDtypeStruct((T, 512), BF16)],
        scratch=[pltpu.VMEM((tq, 128), F32), pltpu.VMEM((tq, 1), F32)],
    )(p, p, p, p, tris)


def _sb_bwd(p, o, dy, tris, dp):
    T = p.shape[0]
    tq = min(256, T)
    nq = T // tq
    scale = SB_HEAD_DIM ** -0.5

    def body(q_ref, gt_ref, k_ref, v_ref, o_ref, dy_ref, tri_ref, _, dqg_ref, dk_hbm, dv_hbm,
             dq_acc, r_ref, s_ref, dk_acc, dv_acc, sem):
        h, qi = pl.program_id(0), pl.program_id(1)

        @pl.when(qi == 0)
        def _():
            dk_acc[...] = jnp.zeros_like(dk_acc)
            dv_acc[...] = jnp.zeros_like(dv_acc)

        qs = (q_ref[...].astype(F32) * scale).astype(BF16)
        of = o_ref[...]
        dyf = dy_ref[...].astype(F32)
        sg, dsg = _silu_and_grad(gt_ref[...].astype(F32))
        do = dyf * sg
        dqg_ref[:, 128:256] = (dyf * of * dsg).astype(dqg_ref.dtype)
        delta = jnp.sum(do * of, axis=-1, keepdims=True)
        dob = do.astype(BF16)
        dq_acc[...] = jnp.zeros_like(dq_acc)
        r_ref[...] = jnp.zeros_like(r_ref)
        s_ref[...] = jnp.zeros_like(s_ref)

        def tile(rows, keep, r_in, s_in):
            kj, vj = k_ref[rows, :], v_ref[rows, :]
            z, e, lf = _sb_logits(qs, kj)
            lf = jnp.where(keep, lf, 0.0)
            a = jnp.where(keep, jnp.exp(z + _split_dot(lf, tri_ref[0]) + r_in), 0.0)
            gz = a * _dot_nt(dob, vj)
            later = _split_dot(gz, tri_ref[1]) + s_in
            sig = jnp.where(z >= 0.0, 1.0, e) / (1.0 + e)
            dz = jnp.where(keep, gz - sig * (delta - later), 0.0)
            dk_acc[rows, :] += _dot(dz.T.astype(BF16), qs)
            dv_acc[rows, :] += _dot(a.T.astype(BF16), dob)
            return (_dot(dz.astype(BF16), kj), jnp.sum(lf, axis=-1, keepdims=True),
                    jnp.sum(gz, axis=-1, keepdims=True))

        def step(carry):
            w, _ = carry
            jl, live, rows_l, rows_e = _sb_window(qi, w, tq)
            r0, s0 = r_ref[...], s_ref[...]
            dq1, t1, g1 = tile(rows_l, _sb_before(tq, qi * tq, jl * tq), r0, s0)
            dq2, t2, g2 = tile(rows_e, live, r0 + t1, s0 + g1)
            dq_acc[...] += dq1 + dq2
            s_ref[...] = s0 + g1 + g2
            r_new = r0 + t1 + t2
            r_ref[...] = r_new
            return w + 1, jnp.max(r_new) > SB_EXIT

        lax.while_loop(lambda c: jnp.logical_and(qi - 2 * c[0] >= 0, c[1]), step, (0, True))
        dqg_ref[:, 0:128] = (dq_acc[...] * scale).astype(dqg_ref.dtype)

        @pl.when(qi == nq - 1)
        def _():
            cols = pl.ds(pl.multiple_of(h * 128, 128), 128)
            ck = pltpu.make_async_copy(dk_acc, dk_hbm.at[:, cols], sem.at[0])
            cv = pltpu.make_async_copy(dv_acc, dv_hbm.at[:, cols], sem.at[1])
            ck.start()
            cv.start()
            ck.wait()
            cv.wait()

    col = lambda k: pl.BlockSpec((tq, 128), lambda h, i: (i, N_C // 128 + 4 * h + k))
    whole = lambda k: pl.BlockSpec((T, 128), lambda h, i: (0, N_C // 128 + 4 * h + k))
    blk = pl.BlockSpec((tq, 128), lambda h, i: (i, h))
    return _call(
        body, name="sb_bwd", grid=(SB_HEADS, nq),
        in_specs=[col(0), col(1), whole(2), whole(3), blk, blk, _full((2, tq, tq)), _HBM],
        out_specs=[pl.BlockSpec((tq, 256), lambda h, i: (i, N_C // 256 + 2 * h)), _HBM, _HBM],
        out_shape=[jax.ShapeDtypeStruct(dp.shape, dp.dtype),
                   jax.ShapeDtypeStruct((T, 512), F32), jax.ShapeDtypeStruct((T, 512), F32)],
        scratch=[pltpu.VMEM((tq, 128), F32), pltpu.VMEM((tq, 1), F32), pltpu.VMEM((tq, 1), F32),
                 pltpu.VMEM((T, 128), F32), pltpu.VMEM((T, 128), F32), pltpu.SemaphoreType.DMA((2,))],
        aliases={7: 0},
    )(p, p, p, p, o, dy, tris, dp)


def _sb_kv_into_dp(dk, dv, dp):
    T = dk.shape[0]
    tm = min(1024, T)

    def body(dk_ref, dv_ref, _, o_ref):
        o_ref[:, 0:128] = dk_ref[...].astype(o_ref.dtype)
        o_ref[:, 128:256] = dv_ref[...].astype(o_ref.dtype)

    blk = pl.BlockSpec((tm, 128), lambda i, h: (i, h))
    return _call(
        body, name="sb_kv_into_dp", grid=(T // tm, SB_HEADS), in_specs=[blk, blk, _HBM],
        out_specs=pl.BlockSpec((tm, 256), lambda i, h: (i, N_C // 256 + 2 * h + 1)),
        out_shape=jax.ShapeDtypeStruct(dp.shape, dp.dtype), aliases={2: 0},
    )(dk, dv, dp)


def _post_fwd(ya, yb, yc, p, bm, wa, wb, wc, wo, g_post, x):
    T = x.shape[0]
    tm = min(512, T)

    def body(ya_ref, yb_ref, yc_ref, la_ref, lb_ref, lc_ref, bm_ref, wa_ref, wb_ref, wc_ref, wo_ref,
             g_ref, x_ref, m_ref, out_ref, xn_ref):
        merged = None
        for k, (y_ref, l_ref, w_ref) in enumerate(
                ((ya_ref, la_ref, wa_ref), (yb_ref, lb_ref, wb_ref), (yc_ref, lc_ref, wc_ref))):
            gate = _sigmoid(l_ref[...].astype(F32) + bm_ref[k:k + 1, :])
            term = gate * _dot(y_ref[...], w_ref[...])
            merged = term if merged is None else merged + term
        mb = merged.astype(BF16)
        m_ref[...] = mb
        out = _dot(mb, wo_ref[...])
        out_ref[...] = out
        r = lax.rsqrt(jnp.mean(out * out, axis=-1, keepdims=True) + EPS)
        xn_ref[...] = x_ref[...] + out * r * g_ref[...]

    yspec = pl.BlockSpec((tm, 512), lambda i: (i, 0))
    lspec = lambda k: pl.BlockSpec((tm, D), lambda i: (i, k))
    row = pl.BlockSpec((tm, D), lambda i: (i, 0))
    return _call(
        body, name="post_fwd", grid=(T // tm,),
        in_specs=[yspec, yspec, yspec, lspec(0), lspec(1), lspec(2), _full((3, D)),
                  _full((512, D)), _full((512, D)), _full((512, D)), _full((D, D)), _full((1, D)), row],
        out_specs=[row, row, row],
        out_shape=[jax.ShapeDtypeStruct((T, D), BF16), jax.ShapeDtypeStruct((T, D), F32),
                   jax.ShapeDtypeStruct((T, D), F32)],
    )(ya, yb, yc, p, p, p, bm, wa, wb, wc, wo, g_post, x)


def _post_bwd(dxn, out, ya, yb, yc, p, bm, wa, wb, wc, wo, g_post):
    T = dxn.shape[0]
    tm = min(256, T)

    def body(dxn_ref, out_ref, ya_ref, yb_ref, yc_ref, la_ref, lb_ref, lc_ref, bm_ref,
             wa_ref, wb_ref, wc_ref, wo_ref, g_ref,
             do_ref, dpa_ref, dpb_ref, dpc_ref, dl_ref, dya_ref, dyb_ref, dyc_ref, dg_ref, db_ref):
        @pl.when(pl.program_id(0) == 0)
        def _():
            dg_ref[...] = jnp.zeros_like(dg_ref)
            db_ref[...] = jnp.zeros_like(db_ref)

        out = out_ref[...]
        dxn_ = dxn_ref[...]
        r = lax.rsqrt(jnp.mean(out * out, axis=-1, keepdims=True) + EPS)
        a = dxn_ * g_ref[...]
        d_out = r * a - out * (r * r * r) * jnp.mean(a * out, axis=-1, keepdims=True)
        dg_ref[...] += jnp.sum(dxn_ * out * r, axis=0, keepdims=True)
        dob = d_out.astype(BF16)
        do_ref[...] = dob
        dmerged = _dot_nt(dob, wo_ref[...])
        for k, (y_ref, l_ref, w_ref, dp_ref, dy_ref) in enumerate((
                (ya_ref, la_ref, wa_ref, dpa_ref, dya_ref), (yb_ref, lb_ref, wb_ref, dpb_ref, dyb_ref),
                (yc_ref, lc_ref, wc_ref, dpc_ref, dyc_ref))):
            gate = _sigmoid(l_ref[...].astype(F32) + bm_ref[k:k + 1, :])
            proj = _dot(y_ref[...], w_ref[...])
            dproj = (dmerged * gate).astype(BF16)
            dp_ref[...] = dproj
            dlog = dmerged * proj * gate * (1.0 - gate)
            dl_ref[:, D * k:D * (k + 1)] = dlog.astype(dl_ref.dtype)
            db_ref[k:k + 1, :] += jnp.sum(dlog, axis=0, keepdims=True)
            dy_ref[...] = _dot_nt(dproj, w_ref[...]).astype(dy_ref.dtype)

    yspec = pl.BlockSpec((tm, 512), lambda i: (i, 0))
    lspec = lambda k: pl.BlockSpec((tm, D), lambda i: (i, k))
    row = pl.BlockSpec((tm, D), lambda i: (i, 0))
    sds = jax.ShapeDtypeStruct
    return _call(
        body, name="post_bwd", grid=(T // tm,),
        in_specs=[row, row, yspec, yspec, yspec, lspec(0), lspec(1), lspec(2), _full((3, D)),
                  _full((512, D)), _full((512, D)), _full((512, D)), _full((D, D)), _full((1, D))],
        out_specs=[row, row, row, row, pl.BlockSpec((tm, 3 * D), lambda i: (i, 0)), yspec, yspec, yspec,
                   _full((1, D)), _full((3, D))],
        out_shape=[sds((T, D), BF16), sds((T, D), BF16), sds((T, D), BF16), sds((T, D), BF16),
                   sds((T, IN_PAD), ACT), sds((T, 512), ACT), sds((T, 512), ACT), sds((T, 512), ACT),
                   sds((1, D), F32), sds((3, D), F32)],
    )(dxn, out, ya, yb, yc, p, p, p, bm, wa, wb, wc, wo, g_post)


def _loss_and_grad(y, target):
    T = y.shape[0]
    tm = min(1024, T)

    def body(y_ref, t_ref, l_ref, dy_ref):
        @pl.when(pl.program_id(0) == 0)
        def _():
            l_ref[...] = jnp.zeros_like(l_ref)
        e = y_ref[...] - t_ref[...]
        l_ref[...] += jnp.sum(e * e, axis=0, keepdims=True)
        dy_ref[...] = e * (1.0 / D)

    row = pl.BlockSpec((tm, D), lambda i: (i, 0))
    return _call(
        body, name="loss", grid=(T // tm,), in_specs=[row, row], out_specs=[_full((1, D)), row],
        out_shape=[jax.ShapeDtypeStruct((1, D), F32), jax.ShapeDtypeStruct((T, D), F32)],
    )(y, target)


def _sibling_swap(srcs, name):
    n = len(srcs)

    def body(*refs):
        s_refs, r_refs = refs[:n], refs[n:2 * n]
        send_sems, recv_sems = refs[2 * n:]
        x, y, c = lax.axis_index("x"), lax.axis_index("y"), lax.axis_index("c")
        copies = [pltpu.make_async_remote_copy(
            src_ref=s_refs[i], dst_ref=r_refs[i], send_sem=send_sems.at[i], recv_sem=recv_sems.at[i],
            device_id=(x, y, 1 - c), device_id_type=MESH) for i in range(n)]
        for cp in copies:
            cp.start()
        for cp in copies:
            cp.wait()

    return pl.pallas_call(
        body, name=name, in_specs=[_HBM] * n, out_specs=[_HBM] * n,
        out_shape=[jax.ShapeDtypeStruct(a.shape, a.dtype) for a in srcs],
        scratch_shapes=[pltpu.SemaphoreType.DMA((n,)), pltpu.SemaphoreType.DMA((n,))],
    )(*srcs)


_ELEMENTWISE_BLOCK_BYTES = 1 << 20


def _row_tile(rows, cols):
    if rows * cols * 4 <= 2 * _ELEMENTWISE_BLOCK_BYTES:
        return rows
    for tr in (2048, 1024, 512, 256, 128, 64, 32, 16, 8):
        if rows % tr == 0 and tr * cols * 4 <= _ELEMENTWISE_BLOCK_BYTES:
            return tr
    raise ValueError(f"no row tile for {(rows, cols)}")


def _sum_chips(r, name):
    _, R, C = r.shape
    tr = _row_tile(R, C)

    def body(r_ref, o_ref):
        f = lambda j: r_ref[j].astype(F32)
        o_ref[...] = ((f(0) + f(1)) + f(2)) + f(3)

    return _call(
        body, name=name, grid=(R // tr,),
        in_specs=[pl.BlockSpec((4, tr, C), lambda i: (0, i, 0))],
        out_specs=pl.BlockSpec((tr, C), lambda i: (i, 0)),
        out_shape=jax.ShapeDtypeStruct((R, C), F32),
    )(r)


def _adamw(w, m, v, g_mine, g_other, name):
    R, C = w.shape
    tr = _row_tile(R, C)

    def body(w_ref, m_ref, v_ref, a_ref, b_ref, g_ref, d_ref, nm_ref, nv_ref):
        g = a_ref[...] + b_ref[...]
        g_ref[...] = g
        m_new = ADAM_B1 * m_ref[...] + (1.0 - ADAM_B1) * g
        v_new = ADAM_B2 * v_ref[...] + (1.0 - ADAM_B2) * (g * g)
        nm_ref[...] = m_new
        nv_ref[...] = v_new
        m_hat = m_new / (1.0 - ADAM_B1 ** ADAM_STEP)
        v_hat = v_new / (1.0 - ADAM_B2 ** ADAM_STEP)
        d_ref[...] = -ADAM_LR * (m_hat / (jnp.sqrt(v_hat) + ADAM_EPS) + ADAM_WD * w_ref[...])

    blk = pl.BlockSpec((tr, C), lambda i: (i, 0))
    sds = jax.ShapeDtypeStruct((R, C), F32)
    return _call(body, name=name, grid=(R // tr,), in_specs=[blk] * 5, out_specs=[blk] * 4,
                 out_shape=[sds] * 4)(w, m, v, g_mine, g_other)


_NAMES = ('w_in', 'gm_w_s', 'gm_b_s', 'gm_norm_gain', 'sw_sinks', 'w_branch_a', 'w_branch_b',
          'w_branch_c', 'b_merge', 'w_out', 'g_pre', 'g_post')
_SMALL = ('gm_w_s', 'gm_b_s', 'gm_norm_gain', 'sw_sinks', 'g_pre', 'g_post')
_PACK_COLS = 1024


def _pack(arrays):
    flat = jnp.concatenate([a.reshape(-1).astype(F32) for a in arrays])
    rows = -(-flat.shape[0] // (8 * _PACK_COLS)) * 8
    return jnp.pad(flat, (0, rows * _PACK_COLS - flat.shape[0])).reshape(rows, _PACK_COLS)


def _unpack(buf, shapes):
    flat = buf.reshape(-1)
    out, off = [], 0
    for s in shapes:
        n = int(np.prod(s))
        out.append(flat[off:off + n].reshape(s))
        off += n
    return out


def _permute_cols(w):
    return jnp.concatenate(
        [jnp.zeros(w.shape[:-1] + (n,), w.dtype) if o is None else w[..., o:o + n] for o, n in _PERM], axis=-1)


def _unpermute_cols(w):
    new_off, off = {}, 0
    for o, n in _PERM:
        if o is not None:
            new_off[o] = (off, n)
        off += n
    return jnp.concatenate([w[..., new_off[o][0]:new_off[o][0] + new_off[o][1]] for o in sorted(new_off)], axis=-1)


def _tables(T):
    half = HEAD_DIM // 2
    freqs = ROPE_THETA ** (-jnp.arange(half, dtype=F32) / half)
    ang = jnp.arange(T).astype(F32)[:, None] * freqs[None, :]
    cos, sin = jnp.cos(ang), jnp.sin(ang)
    return (jnp.tile(jnp.concatenate([cos, cos], axis=1), (1, 2)),
            jnp.tile(jnp.concatenate([-sin, sin], axis=1), (1, 2)), _sb_tris(min(256, T)))


def _layer_fwd(x, lw, tables, carry=()):
    gpre, gpost = lw['g_pre'][None, :], lw['g_post'][None, :]
    bfull = jnp.broadcast_to(lw['bs'][:, :, None], (GM_GROUPS, BLOCK, BLOCK))
    gv = lw['gv'][None, :]
    p, h, *got = _in_proj(x, gpre, lw['w_in'], carry)
    ya = _gmlp_fwd(p, lw['ws'], bfull, gv)
    qkr = _rope_fwd(p, *tables[:2])
    yb = _swa_fwd(qkr, p, lw['sinks'])
    oc, yc = _sb_fwd(p, tables[2])
    merged, out, xn = _post_fwd(ya, yb, yc, p, lw['bm'], lw['wa'], lw['wb'], lw['wc'], lw['wo'], gpost, x)
    return xn, (lw, p, h, ya, qkr, yb, oc, yc, merged, out, bfull, gv, gpre, gpost), got


def _layer_bwd(x, saved, dxn, tables, carry=()):
    lw, p, h, ya, qkr, yb, oc, yc, merged, out, bfull, gv, gpre, gpost = saved
    g = {}
    (d_out, dpa, dpb, dpc, dp, dya, dyb, dyc, dgpost, dbm) = _post_bwd(
        dxn, out, ya, yb, yc, p, lw['bm'], lw['wa'], lw['wb'], lw['wc'], lw['wo'], gpost)
    g['w_out'] = _matmul_tn(merged, d_out, "grad_w_out")
    g['w_branch_a'] = _matmul_tn(ya, dpa, "grad_w_a")
    g['w_branch_b'] = _matmul_tn(yb, dpb, "grad_w_b")
    g['w_branch_c'] = _matmul_tn(yc, dpc, "grad_w_c")
    g['g_post'] = dgpost[0]
    g['b_merge'] = dbm
    dp, dws, dbacc, dgv = _gmlp_bwd(p, dya, lw['ws'], bfull, gv, dp)
    g['gm_w_s'] = dws
    g['gm_b_s'] = jnp.sum(dbacc.reshape(BLOCK, GM_GROUPS, BLOCK), axis=2).T
    g['gm_norm_gain'] = dgv[0]
    dp, dsink = _swa_bwd(qkr, p, dyb, lw['sinks'], *tables[:2], dp)
    g['sw_sinks'] = dsink[:, 0]
    dp, dkc, dvc = _sb_bwd(p, oc, dyc, tables[2], dp)
    dp = _sb_kv_into_dp(dkc, dvc, dp)
    g['w_in'] = _matmul_tn(h, dp, "grad_w_in")
    dx, dgpre, *got = _in_bwd(dp, lw['w_in'], x, dxn, gpre, carry)
    g['g_pre'] = dgpre[0]
    return dx, g, got


def kernel(x, w_in, gm_w_s, gm_b_s, gm_norm_gain, sw_sinks, w_branch_a, w_branch_b, w_branch_c, b_merge, w_out, g_pre, g_post, loss_target, m_w_in, m_gm_w_s, m_gm_b_s, m_gm_norm_gain, m_sw_sinks, m_w_branch_a, m_w_branch_b, m_w_branch_c, m_b_merge, m_w_out, m_g_pre, m_g_post, v_w_in, v_gm_w_s, v_gm_b_s, v_gm_norm_gain, v_sw_sinks, v_w_branch_a, v_w_branch_b, v_w_branch_c, v_b_merge, v_w_out, v_g_pre, v_g_post):
    T = x.shape[1]
    weights = dict(zip(_NAMES, (w_in, gm_w_s, gm_b_s, gm_norm_gain, sw_sinks, w_branch_a, w_branch_b,
                                w_branch_c, b_merge, w_out, g_pre, g_post)))
    mom_m = dict(zip(_NAMES, (m_w_in, m_gm_w_s, m_gm_b_s, m_gm_norm_gain, m_sw_sinks, m_w_branch_a,
                              m_w_branch_b, m_w_branch_c, m_b_merge, m_w_out, m_g_pre, m_g_post)))
    mom_v = dict(zip(_NAMES, (v_w_in, v_gm_w_s, v_gm_b_s, v_gm_norm_gain, v_sw_sinks, v_w_branch_a,
                              v_w_branch_b, v_w_branch_c, v_b_merge, v_w_out, v_g_pre, v_g_post)))
    abc = lambda d: jnp.stack([d['w_branch_a'], d['w_branch_b'], d['w_branch_c']], axis=1)

    local = [w_in.astype(BF16), abc(weights).astype(BF16), b_merge, w_out.astype(BF16)]
    shard = lambda l: [a[l] for a in local]

    def assemble(got):
        cat = lambda i, axis: jnp.concatenate([got[i][j] for j in range(4)], axis=axis)
        wabc = cat(1, 2)
        return dict(w_in=_permute_cols(cat(0, 1)), wa=wabc[0], wb=wabc[1], wc=wabc[2], bm=cat(2, 1), wo=cat(3, 0))

    tables = _tables(T)
    xs = [x[0]]
    saved = []
    got = _chip_exchange(shard(0), (False,) * 4, "gather_weights")
    for l in range(DEPTH):
        lw = dict(assemble(got), ws=gm_w_s[l], bs=gm_b_s[l], gv=gm_norm_gain[l], sinks=sw_sinks[l],
                  g_pre=g_pre[l], g_post=g_post[l])
        xn, sv, got = _layer_fwd(xs[l], lw, tables, shard(l + 1) if l + 1 < DEPTH else ())
        xs.append(xn)
        saved.append(sv)

    lsum, dxn = _loss_and_grad(xs[DEPTH], loss_target[0])
    loss = lax.psum(0.5 * jnp.sum(lsum) / D, ("x", "y", "c"))

    def partials(g):
        per_chip = lambda a, axis: jnp.stack(jnp.split(a, 4, axis=axis))
        g_abc = jnp.stack([g['w_branch_a'], g['w_branch_b'], g['w_branch_c']])
        return [per_chip(_unpermute_cols(g['w_in']), 1).astype(BF16),
                per_chip(g_abc, 2).astype(BF16),
                per_chip(g['b_merge'], 1),
                per_chip(g['w_out'], 0).astype(BF16)]

    small = {n: [None] * DEPTH for n in _SMALL}
    received = [None] * DEPTH
    pending = ()
    for l in reversed(range(DEPTH)):
        dxn, g, got = _layer_bwd(xs[l], saved[l], dxn, tables, pending)
        if pending:
            received[l + 1] = got
        pending = partials(g)
        for n in _SMALL:
            small[n][l] = g[n]
    grad_x = dxn[None]
    g_small = _pack([jnp.stack(small[n]) for n in _SMALL])
    got = _chip_exchange(pending + [g_small], (True,) * 4 + (False,), "scatter_grads")
    received[0] = got[:4]

    views = [(4 * D, IN_WIDTH // 4), (4 * 3 * 512, D // 4), (4 * 3, D // 4), (D, D), g_small.shape]
    stacks = [jnp.stack([received[l][i] for l in range(DEPTH)], axis=1) for i in range(4)] + [got[4]]
    sums = [_sum_chips(r.reshape((4,) + v), f"sum_chips_{i}") for i, (r, v) in enumerate(zip(stacks, views))]
    others = _sibling_swap(sums, "swap_core_sums")
    tensors = [lambda d: d['w_in'], abc, lambda d: d['b_merge'], lambda d: d['w_out'],
               lambda d: _pack([d[n] for n in _SMALL])]
    res = [_adamw(t(weights).reshape(v), t(mom_m).reshape(v), t(mom_v).reshape(v), s_, o_, f"adamw_{i}")
           for i, (t, v, s_, o_) in enumerate(zip(tensors, views, sums, others))]

    outs = []
    for kind in range(4):
        r = [res[i][kind] for i in range(5)]
        r_abc = r[1].reshape(4, 3, 512, D // 4)
        tiny = dict(zip(_SMALL, _unpack(r[4], [weights[n].shape for n in _SMALL])))
        big = dict(w_in=r[0].reshape(w_in.shape), w_branch_a=r_abc[:, 0], w_branch_b=r_abc[:, 1],
                   w_branch_c=r_abc[:, 2], b_merge=r[2].reshape(b_merge.shape), w_out=r[3].reshape(w_out.shape))
        outs.extend(big[n] if n in big else tiny[n] for n in _NAMES)
    return (loss, grad_x, *outs)
```

```python
import functools

import numpy as np
import jax
import jax.numpy as jnp
from jax import lax
from jax.experimental import pallas as pl
from jax.experimental.pallas import tpu as pltpu

F32 = jnp.float32
BF16 = jnp.bfloat16
ACT = jnp.bfloat16

D = 1024
DEPTH = 4
BLOCK = 128
EPS = 1e-6
NEG = -1e30
GM_GROUPS = 4
GM_WIDTH = 512
HEAD_DIM = 64
SW_HEADS = 8
SB_HEADS = 4
SB_HEAD_DIM = 128
ROPE_THETA = 10000.0
IN_WIDTH = 7936
IN_PAD = 8192

O_UA, O_VA, O_GA, O_QB, O_KB, O_VB, O_GB = 0, 512, 1024, 1536, 2048, 2176, 2304
O_QC, O_KC, O_VC, O_GC, O_MG = 2816, 3328, 3840, 4352, 4864
_PERM = ((O_MG, 3072), (O_UA, 512), (O_VA, 512), (O_GA, 512),
         (O_QB, 512), (O_KB, 128), (O_VB, 128), (None, IN_PAD - IN_WIDTH), (O_GB, 512)) + tuple(
    (o + 128 * h, 128) for h in range(4) for o in (O_QC, O_GC, O_KC, O_VC))
N_MG, N_UA, N_VA, N_GA = 0, 3072, 3584, 4096
N_QB, N_KB, N_VB, N_GB = 4608, 5120, 5248, 5632
N_C = 6144

ADAM_LR, ADAM_B1, ADAM_B2, ADAM_EPS, ADAM_WD, ADAM_STEP = 0.001, 0.9, 0.999, 1e-08, 0.01, 10

SB_EXIT = -104.0
V7X_VMEM_LIMIT = 48 * 1024 * 1024
MESH = pl.DeviceIdType.MESH


_HBM = pl.BlockSpec(memory_space=pl.ANY)


def _call(body, *, name, grid, in_specs, out_specs, out_shape, scratch=(), aliases=None):
    return pl.pallas_call(
        body, name=name, grid=grid, in_specs=in_specs, out_specs=out_specs, out_shape=out_shape,
        scratch_shapes=list(scratch), input_output_aliases=aliases or {},
        compiler_params=pltpu.CompilerParams(
            dimension_semantics=("arbitrary",) * len(grid), vmem_limit_bytes=V7X_VMEM_LIMIT))


def _sigmoid(x):
    return 1.0 / (1.0 + jnp.exp(-x))


def _silu_and_grad(x):
    s = _sigmoid(x)
    return x * s, s * (1.0 + x * (1.0 - s))


def _dot(a, b):
    return jnp.dot(a, b, preferred_element_type=F32)


def _dot_nt(a, b):
    return lax.dot_general(a, b, (((1,), (1,)), ((), ())), preferred_element_type=F32)


def _dot_tn(a, b):
    return lax.dot_general(a, b, (((0,), (0,)), ((), ())), preferred_element_type=F32)


def _full(shape):
    return pl.BlockSpec(shape, lambda *_: (0,) * len(shape))


_CHIP_STEPS = ((1, 0), (0, 1), (1, 1))


def _exchange_ops(s_refs, r_refs, send_sems, recv_sems, local_sems, per_target):
    n = len(s_refs)

    def copies():
        x, y, c = lax.axis_index("x"), lax.axis_index("y"), lax.axis_index("c")
        me = 2 * x + y
        pick = lambda i, j: s_refs[i].at[j] if per_target[i] else s_refs[i]
        own = [pltpu.make_async_copy(pick(i, me), r_refs[i].at[me], local_sems.at[i]) for i in range(n)]
        sent, arriving = [], []
        for k, (dx, dy) in enumerate(_CHIP_STEPS):
            tx, ty = (x + dx) % 2, (y + dy) % 2
            peer = 2 * tx + ty
            for i in range(n):
                sems = dict(send_sem=send_sems.at[3 * i + k], recv_sem=recv_sems.at[3 * i + k])
                sent.append(pltpu.make_async_remote_copy(
                    src_ref=pick(i, peer), dst_ref=r_refs[i].at[me], device_id=(tx, ty, c),
                    device_id_type=MESH, **sems))
                arriving.append(pltpu.make_async_remote_copy(
                    src_ref=pick(i, peer), dst_ref=r_refs[i].at[peer], device_id=(x, y, c),
                    device_id_type=MESH, **sems))
        return own, sent, arriving

    def start():
        own, sent, _ = copies()
        for cp in own + sent:
            cp.start()

    def wait():
        own, sent, arriving = copies()
        for cp in arriving:
            cp.wait_recv()
        for cp in sent:
            cp.wait_send()
        for cp in own:
            cp.wait()

    return start, wait


def _exchange_shapes(srcs, per_target):
    return [jax.ShapeDtypeStruct((4,) + (a.shape[1:] if pt else a.shape), a.dtype)
            for a, pt in zip(srcs, per_target)]


def _exchange_sems(n):
    return [pltpu.SemaphoreType.DMA((3 * n,)), pltpu.SemaphoreType.DMA((3 * n,)), pltpu.SemaphoreType.DMA((n,))]


def _chip_exchange(srcs, per_target, name):
    n = len(srcs)

    def body(*refs):
        start, wait = _exchange_ops(refs[:n], refs[n:2 * n], *refs[2 * n:], per_target)
        start()
        wait()

    return pl.pallas_call(
        body, name=name, in_specs=[_HBM] * n, out_specs=[_HBM] * n, out_shape=_exchange_shapes(srcs, per_target),
        scratch_shapes=_exchange_sems(n),
    )(*srcs)


def _in_proj(x, g_pre, w_in, carry=()):
    T = x.shape[0]
    tm, tn = min(2048, T), 512
    ni, nj, n = T // tm, IN_PAD // tn, len(carry)

    def body(x_ref, g_ref, w_ref, *rest):
        p_ref, h_ref = rest[n], rest[n + 1]
        i, j = pl.program_id(0), pl.program_id(1)
        if n:
            start, wait = _exchange_ops(rest[:n], rest[n + 2:2 * n + 2], *rest[2 * n + 2:], (False,) * n)
            pl.when((i == 0) & (j == 0))(start)

        @pl.when(j == 0)
        def _():
            xf = x_ref[...]
            r = lax.rsqrt(jnp.mean(xf * xf, axis=-1, keepdims=True) + EPS)
            h_ref[...] = (xf * r * g_ref[...]).astype(BF16)
        p_ref[...] = _dot(h_ref[...], w_ref[...]).astype(p_ref.dtype)
        if n:
            pl.when((i == ni - 1) & (j == nj - 1))(wait)

    return _call(
        body, name="in_proj_gather" if n else "in_proj", grid=(ni, nj),
        in_specs=[pl.BlockSpec((tm, D), lambda i, j: (i, 0)), _full((1, D)),
                  pl.BlockSpec((D, tn), lambda i, j: (0, j))] + [_HBM] * n,
        out_specs=[pl.BlockSpec((tm, tn), lambda i, j: (i, j)), pl.BlockSpec((tm, D), lambda i, j: (i, 0))]
        + [_HBM] * n,
        out_shape=[jax.ShapeDtypeStruct((T, IN_PAD), ACT), jax.ShapeDtypeStruct((T, D), BF16)]
        + _exchange_shapes(carry, (False,) * n),
        scratch=_exchange_sems(n) if n else (),
    )(x, g_pre, w_in, *carry)


def _in_bwd(dp, w_in, x, dxn, g_pre, carry=()):
    T = x.shape[0]
    tm, tk = min(1024, T), 1024
    ni, nk, n = T // tm, IN_PAD // tk, len(carry)

    def body(dp_ref, w_ref, x_ref, dxn_ref, g_ref, *rest):
        dx_ref, dg_ref, acc = rest[n], rest[n + 1], rest[2 * n + 2]
        i, k = pl.program_id(0), pl.program_id(1)
        if n:
            start, wait = _exchange_ops(rest[:n], rest[n + 2:2 * n + 2], *rest[2 * n + 3:], (True,) * n)
            pl.when((i == 0) & (k == 0))(start)

        @pl.when(k == 0)
        def _():
            acc[...] = jnp.zeros_like(acc)

        @pl.when((i == 0) & (k == 0))
        def _():
            dg_ref[...] = jnp.zeros_like(dg_ref)

        acc[...] += _dot_nt(dp_ref[...], w_ref[...])

        @pl.when(k == nk - 1)
        def _():
            dh = acc[...]
            xf = x_ref[...]
            r = lax.rsqrt(jnp.mean(xf * xf, axis=-1, keepdims=True) + EPS)
            a = dh * g_ref[...]
            dx_ref[...] = dxn_ref[...] + r * a - xf * (r * r * r) * jnp.mean(a * xf, axis=-1, keepdims=True)
            dg_ref[...] += jnp.sum(dh * xf * r, axis=0, keepdims=True)

        if n:
            pl.when((i == ni - 1) & (k == nk - 1))(wait)

    return _call(
        body, name="in_bwd_scatter" if n else "in_bwd", grid=(ni, nk),
        in_specs=[pl.BlockSpec((tm, tk), lambda i, k: (i, k)), pl.BlockSpec((D, tk), lambda i, k: (0, k)),
                  pl.BlockSpec((tm, D), lambda i, k: (i, 0)), pl.BlockSpec((tm, D), lambda i, k: (i, 0)),
                  _full((1, D))] + [_HBM] * n,
        out_specs=[pl.BlockSpec((tm, D), lambda i, k: (i, 0)), _full((1, D))] + [_HBM] * n,
        out_shape=[jax.ShapeDtypeStruct((T, D), F32), jax.ShapeDtypeStruct((1, D), F32)]
        + _exchange_shapes(carry, (True,) * n),
        scratch=[pltpu.VMEM((tm, D), F32)] + (_exchange_sems(n) if n else []),
    )(dp, w_in, x, dxn, g_pre, *carry)


def _matmul_tn(a, b, name):
    T, K = a.shape
    N = b.shape[1]
    tk, tn, tt = min(K, 1024), min(N, 2048), min(T, 512)

    def body(a_ref, b_ref, o_ref):
        @pl.when(pl.program_id(2) == 0)
        def _():
            o_ref[...] = jnp.zeros_like(o_ref)
        o_ref[...] += _dot_tn(a_ref[...], b_ref[...])

    return _call(
        body, name=name, grid=(K // tk, N // tn, T // tt),
        in_specs=[pl.BlockSpec((tt, tk), lambda i, j, t: (t, i)), pl.BlockSpec((tt, tn), lambda i, j, t: (t, j))],
        out_specs=pl.BlockSpec((tk, tn), lambda i, j, t: (i, j)),
        out_shape=jax.ShapeDtypeStruct((K, N), F32),
    )(a, b)


def _gm_forward_parts(v_ref, gv_ref, ws_ref, bf_ref, nch):
    vf = v_ref[...].astype(F32)
    mu = jnp.mean(vf, axis=-1, keepdims=True)
    xc = vf - mu
    rstd = lax.rsqrt(jnp.mean(xc * xc, axis=-1, keepdims=True) + EPS)
    xhat = xc * rstd
    vnb = (xhat * gv_ref[...]).astype(BF16)
    row = lax.broadcasted_iota(jnp.int32, (BLOCK, BLOCK), 0)
    col = lax.broadcasted_iota(jnp.int32, (BLOCK, BLOCK), 1)
    mixed, vcats, wgs = [], [], []
    for g in range(GM_GROUPS):
        vg = vnb[:, BLOCK * g:BLOCK * (g + 1)]
        vcat = jnp.concatenate([vg[BLOCK * k:BLOCK * (k + 1), :] for k in range(nch)], axis=1)
        wg = jnp.where(row >= col, ws_ref[g], 0.0)
        m = _dot(wg.astype(BF16), vcat)
        mixed.append(jnp.concatenate(
            [m[:, BLOCK * k:BLOCK * (k + 1)] + bf_ref[g] for k in range(nch)], axis=0))
        vcats.append(vcat)
        wgs.append(wg)
    return xhat, rstd, jnp.concatenate(mixed, axis=1), vcats, wgs, row >= col


def _gmlp_fwd(p, ws, bfull, gv):
    T = p.shape[0]
    tm = min(512, T)
    nch = tm // BLOCK

    def body(u_ref, v_ref, gt_ref, ws_ref, bf_ref, gv_ref, y_ref):
        _, _, mixed, _, _, _ = _gm_forward_parts(v_ref, gv_ref, ws_ref, bf_ref, nch)
        sg, _ = _silu_and_grad(gt_ref[...].astype(F32))
        y_ref[...] = (u_ref[...].astype(F32) * mixed * sg).astype(y_ref.dtype)

    seg = lambda off: pl.BlockSpec((tm, 512), lambda i: (i, off // 512))
    return _call(
        body, name="gmlp_fwd", grid=(T // tm,),
        in_specs=[seg(N_UA), seg(N_VA), seg(N_GA), _full((GM_GROUPS, BLOCK, BLOCK)),
                  _full((GM_GROUPS, BLOCK, BLOCK)), _full((1, GM_WIDTH))],
        out_specs=pl.BlockSpec((tm, 512), lambda i: (i, 0)),
        out_shape=jax.ShapeDtypeStruct((T, GM_WIDTH), BF16),
    )(p, p, p, ws, bfull, gv)


def _gmlp_bwd(p, dy, ws, bfull, gv, dp):
    T = p.shape[0]
    tm = min(512, T)
    nch = tm // BLOCK

    def body(u_ref, v_ref, gt_ref, dy_ref, ws_ref, bf_ref, gv_ref, _, dp_ref, dws_ref, db_ref, dgv_ref):
        @pl.when(pl.program_id(0) == 0)
        def _():
            dws_ref[...] = jnp.zeros_like(dws_ref)
            db_ref[...] = jnp.zeros_like(db_ref)
            dgv_ref[...] = jnp.zeros_like(dgv_ref)

        xhat, rstd, mixed, vcats, wgs, tril = _gm_forward_parts(v_ref, gv_ref, ws_ref, bf_ref, nch)
        u = u_ref[...].astype(F32)
        gt = gt_ref[...].astype(F32)
        dyf = dy_ref[...].astype(F32)
        sg, dsg = _silu_and_grad(gt)
        du = dyf * mixed * sg
        dmixed = dyf * u * sg
        dgate = dyf * (u * mixed) * dsg
        dvn = []
        for g in range(GM_GROUPS):
            dmg = dmixed[:, BLOCK * g:BLOCK * (g + 1)]
            chunks = [dmg[BLOCK * k:BLOCK * (k + 1), :] for k in range(nch)]
            dmcat = jnp.concatenate(chunks, axis=1).astype(BF16)
            dws_ref[g] += jnp.where(tril, _dot_nt(dmcat, vcats[g]), 0.0)
            dvcat = _dot(wgs[g].T.astype(BF16), dmcat)
            dvn.append(jnp.concatenate([dvcat[:, BLOCK * k:BLOCK * (k + 1)] for k in range(nch)], axis=0))
            db_ref[:, BLOCK * g:BLOCK * (g + 1)] += functools.reduce(lambda a, b: a + b, chunks)
        dvn = jnp.concatenate(dvn, axis=1)
        dgv_ref[...] += jnp.sum(dvn * xhat, axis=0, keepdims=True)
        dxh = dvn * gv_ref[...]
        dv = rstd * (dxh - jnp.mean(dxh, axis=-1, keepdims=True)
                     - xhat * jnp.mean(dxh * xhat, axis=-1, keepdims=True))
        dp_ref[:, 0:512] = du.astype(dp_ref.dtype)
        dp_ref[:, 512:1024] = dv.astype(dp_ref.dtype)
        dp_ref[:, 1024:1536] = dgate.astype(dp_ref.dtype)

    seg = lambda off: pl.BlockSpec((tm, 512), lambda i: (i, off // 512))
    return _call(
        body, name="gmlp_bwd", grid=(T // tm,),
        in_specs=[seg(N_UA), seg(N_VA), seg(N_GA), pl.BlockSpec((tm, 512), lambda i: (i, 0)),
                  _full((GM_GROUPS, BLOCK, BLOCK)), _full((GM_GROUPS, BLOCK, BLOCK)), _full((1, GM_WIDTH)), _HBM],
        out_specs=[pl.BlockSpec((tm, 1536), lambda i: (i, N_UA // 1536)), _full((GM_GROUPS, BLOCK, BLOCK)),
                   _full((BLOCK, GM_WIDTH)), _full((1, GM_WIDTH))],
        out_shape=[jax.ShapeDtypeStruct(dp.shape, dp.dtype), jax.ShapeDtypeStruct((GM_GROUPS, BLOCK, BLOCK), F32),
                   jax.ShapeDtypeStruct((BLOCK, GM_WIDTH), F32), jax.ShapeDtypeStruct((1, GM_WIDTH), F32)],
        aliases={7: 0},
    )(p, p, p, dy, ws, bfull, gv, dp)


def _swap_halves(x):
    lane = lax.broadcasted_iota(jnp.int32, x.shape, 1) % HEAD_DIM
    return jnp.where(lane < HEAD_DIM // 2, pltpu.roll(x, 96, 1), pltpu.roll(x, 32, 1))


def _rope_fwd(p, cos_t, sin_t):
    T = p.shape[0]
    tm = min(512, T)

    def body(q_ref, k_ref, c_ref, s_ref, o_ref):
        c, s = c_ref[...], s_ref[...]
        for G in range(5):
            xg = (q_ref[:, 128 * G:128 * (G + 1)] if G < 4 else k_ref[...]).astype(F32)
            o_ref[:, 128 * G:128 * (G + 1)] = (xg * c + _swap_halves(xg) * s).astype(o_ref.dtype)

    return _call(
        body, name="rope_fwd", grid=(T // tm,),
        in_specs=[pl.BlockSpec((tm, 512), lambda i: (i, N_QB // 512)),
                  pl.BlockSpec((tm, 128), lambda i: (i, N_KB // 128)),
                  pl.BlockSpec((tm, 128), lambda i: (i, 0)), pl.BlockSpec((tm, 128), lambda i: (i, 0))],
        out_specs=pl.BlockSpec((tm, 640), lambda i: (i, 0)),
        out_shape=jax.ShapeDtypeStruct((T, 640), BF16),
    )(p, p, cos_t, sin_t)


def _unrotate(d, c, s):
    return d * c + _swap_halves(d * s)


def _dup_heads(x):
    left = lax.broadcasted_iota(jnp.int32, x.shape, 1) < HEAD_DIM
    r = pltpu.roll(x, HEAD_DIM, 1)
    return jnp.where(left, x, r), jnp.where(left, r, x)


def _fold_heads(acc0, acc1):
    left = lax.broadcasted_iota(jnp.int32, acc0.shape, 1) < HEAD_DIM
    t0 = acc0 + pltpu.roll(acc0, HEAD_DIM, 1)
    t1 = acc1 + pltpu.roll(acc1, HEAD_DIM, 1)
    return jnp.where(left, t0, t1)


def _swa_valid_t(base):
    kpos = base - BLOCK + lax.broadcasted_iota(jnp.int32, (2 * BLOCK, BLOCK), 0)
    qpos = base + lax.broadcasted_iota(jnp.int32, (2 * BLOCK, BLOCK), 1)
    return jnp.logical_and(kpos >= 0, jnp.logical_and(kpos <= qpos, kpos > qpos - BLOCK))


def _swa_probs_t(qm, kk, valid_t, sink):
    s = jnp.where(valid_t, _dot_nt(kk, qm) * (HEAD_DIM ** -0.5), NEG)
    m = jnp.maximum(jnp.max(s, axis=0, keepdims=True), sink)
    e = jnp.exp(s - m)
    es = jnp.exp(sink - m)
    inv = 1.0 / (jnp.sum(e, axis=0, keepdims=True) + es)
    return e * inv, es * inv


def _swa_operands(kh_ref, kc_ref, vh_ref, vc_ref, nsb):
    out = []
    for h_ref, c_ref in ((kh_ref, kc_ref), (vh_ref, vc_ref)):
        dup = _dup_heads(jnp.concatenate([h_ref[...], c_ref[...]], axis=0).astype(F32))
        out.append([d.astype(BF16) for d in dup])
        out.append([[d[BLOCK * c:BLOCK * (c + 1), :].T.astype(BF16) for c in range(nsb + 1)] for d in dup])
    return out


def _halves(x):
    left = lax.broadcasted_iota(jnp.int32, x.shape, 1) < HEAD_DIM
    zero = jnp.zeros_like(x)
    return jnp.where(left, x, zero), jnp.where(left, zero, x)


def _swa_specs(T, bq, rev):
    n = T // bq
    blk = (lambda i: n - 1 - i) if rev else (lambda i: i)
    halo = lambda i: jnp.maximum(blk(i) * (bq // BLOCK) - 1, 0)
    return blk, [
        pl.BlockSpec(memory_space=pltpu.SMEM),
        pl.BlockSpec((bq, 512), lambda i: (blk(i), 0)),
        pl.BlockSpec((bq, 128), lambda i: (blk(i), 4)),
        pl.BlockSpec((BLOCK, 128), lambda i: (halo(i), 4)),
        pl.BlockSpec((bq, 128), lambda i: (blk(i), N_VB // 128)),
        pl.BlockSpec((BLOCK, 128), lambda i: (halo(i), N_VB // 128)),
        pl.BlockSpec((bq, 512), lambda i: (blk(i), N_GB // 512)),
    ]


def _swa_fwd(qkr, p, sinks):
    T = p.shape[0]
    bq = min(512, T)
    nsb = bq // BLOCK
    blk, specs = _swa_specs(T, bq, False)

    def body(sink_ref, q_ref, kc_ref, kh_ref, vc_ref, vh_ref, gt_ref, y_ref):
        base = blk(pl.program_id(0)) * bq
        kk, _, _, vT = _swa_operands(kh_ref, kc_ref, vh_ref, vc_ref, nsb)
        top = lax.broadcasted_iota(jnp.int32, (BLOCK, BLOCK), 0) < HEAD_DIM
        for sb in range(nsb):
            rows = slice(sb * BLOCK, (sb + 1) * BLOCK)
            keys = slice(sb * BLOCK, (sb + 2) * BLOCK)
            valid_t = _swa_valid_t(base + sb * BLOCK)
            for G in range(4):
                g = G // 2
                cols = slice(128 * G, 128 * (G + 1))
                vvt = jnp.concatenate([vT[g][sb], vT[g][sb + 1]], axis=1)
                o_t = []
                for hh, qm in enumerate(_halves(q_ref[rows, cols])):
                    pr, _ = _swa_probs_t(qm, kk[g][keys], valid_t, sink_ref[2 * G + hh])
                    o_t.append(_dot(vvt, pr.astype(BF16)))
                o = jnp.where(top, o_t[0], o_t[1]).T
                sg, _ = _silu_and_grad(gt_ref[rows, cols].astype(F32))
                y_ref[rows, cols] = (o * sg).astype(y_ref.dtype)

    return _call(
        body, name="swa_fwd", grid=(T // bq,), in_specs=specs,
        out_specs=pl.BlockSpec((bq, 512), lambda i: (i, 0)),
        out_shape=jax.ShapeDtypeStruct((T, 512), BF16),
    )(sinks, qkr, qkr, qkr, p, p, p)


def _swa_bwd(qkr, p, dy, sinks, cos_t, sin_t, dp):
    T = p.shape[0]
    bq = min(512, T)
    nsb = bq // BLOCK
    blk, specs = _swa_specs(T, bq, True)

    def body(sink_ref, q_ref, kc_ref, kh_ref, vc_ref, vh_ref, gt_ref, dy_ref, c_ref, s_ref, _,
             dp_ref, ds_ref, dk_acc, dv_acc, k_carry, v_carry):
        @pl.when(pl.program_id(0) == 0)
        def _():
            k_carry[...] = jnp.zeros_like(k_carry)
            v_carry[...] = jnp.zeros_like(v_carry)
            ds_ref[...] = jnp.zeros_like(ds_ref)

        base = blk(pl.program_id(0)) * bq
        dk_acc[...] = jnp.zeros_like(dk_acc)
        dv_acc[...] = jnp.zeros_like(dv_acc)
        kk, kT, vv, vT = _swa_operands(kh_ref, kc_ref, vh_ref, vc_ref, nsb)
        top = lax.broadcasted_iota(jnp.int32, (BLOCK, BLOCK), 0) < HEAD_DIM
        for sb in range(nsb):
            rows = slice(sb * BLOCK, (sb + 1) * BLOCK)
            keys = slice(sb * BLOCK, (sb + 2) * BLOCK)
            valid_t = _swa_valid_t(base + sb * BLOCK)
            dkp = [jnp.zeros((2 * BLOCK, BLOCK), F32)] * 2
            dvp = [jnp.zeros((2 * BLOCK, BLOCK), F32)] * 2
            for G in range(4):
                g = G // 2
                cols = slice(128 * G, 128 * (G + 1))
                kkt = jnp.concatenate([kT[g][sb], kT[g][sb + 1]], axis=1)
                vvt = jnp.concatenate([vT[g][sb], vT[g][sb + 1]], axis=1)
                qms = _halves(q_ref[rows, cols])
                prs, pss, o_t = [], [], []
                for hh in range(2):
                    pr, ps = _swa_probs_t(qms[hh], kk[g][keys], valid_t, sink_ref[2 * G + hh])
                    prs.append(pr)
                    pss.append(ps)
                    o_t.append(_dot(vvt, pr.astype(BF16)))
                o = jnp.where(top, o_t[0], o_t[1]).T
                sg, dsg = _silu_and_grad(gt_ref[rows, cols].astype(F32))
                dyf = dy_ref[rows, cols].astype(F32)
                do = dyf * sg
                dp_ref[rows, 1024 + 128 * G:1024 + 128 * (G + 1)] = (dyf * o * dsg).astype(dp_ref.dtype)
                do_t = do.T
                doms = _halves(do.astype(BF16))
                dq_t = []
                for hh in range(2):
                    dom_t = jnp.where(top if hh == 0 else jnp.logical_not(top), do_t, 0.0).astype(BF16)
                    dpv = _dot(vv[g][keys], dom_t)
                    delta = jnp.sum(prs[hh] * dpv, axis=0, keepdims=True)
                    dsb = (prs[hh] * (dpv - delta) * (HEAD_DIM ** -0.5)).astype(BF16)
                    h = 2 * G + hh
                    ds_ref[h:h + 1, :] += jnp.broadcast_to(
                        -jnp.sum(pss[hh] * delta, axis=1, keepdims=True), (1, 128))
                    dq_t.append(_dot(kkt, dsb))
                    dkp[g] = dkp[g] + _dot(dsb, qms[hh])
                    dvp[g] = dvp[g] + _dot(prs[hh].astype(BF16), doms[hh])
                dq = jnp.where(top, dq_t[0], dq_t[1]).T
                dp_ref[rows, cols] = _unrotate(dq, c_ref[rows, :], s_ref[rows, :]).astype(dp_ref.dtype)
            dk_acc[keys, :] += _fold_heads(dkp[0], dkp[1])
            dv_acc[keys, :] += _fold_heads(dvp[0], dvp[1])
        dk_acc[bq:bq + BLOCK, :] += k_carry[...]
        dv_acc[bq:bq + BLOCK, :] += v_carry[...]
        dk = _unrotate(dk_acc[BLOCK:bq + BLOCK, :], c_ref[...], s_ref[...])
        dp_ref[:, 512:640] = dk.astype(dp_ref.dtype)
        dp_ref[:, 640:768] = dv_acc[BLOCK:bq + BLOCK, :].astype(dp_ref.dtype)
        dp_ref[:, 768:1024] = jnp.zeros((bq, 256), dp_ref.dtype)
        k_carry[...] = dk_acc[0:BLOCK, :]
        v_carry[...] = dv_acc[0:BLOCK, :]

    rowblk = lambda w: pl.BlockSpec((bq, w), lambda i: (blk(i), 0))
    return _call(
        body, name="swa_bwd", grid=(T // bq,), in_specs=specs + [rowblk(512), rowblk(128), rowblk(128), _HBM],
        out_specs=[pl.BlockSpec((bq, 1536), lambda i: (blk(i), N_QB // 1536)), _full((SW_HEADS, 128))],
        out_shape=[jax.ShapeDtypeStruct(dp.shape, dp.dtype), jax.ShapeDtypeStruct((SW_HEADS, 128), F32)],
        scratch=[pltpu.VMEM((bq + BLOCK, 128), F32), pltpu.VMEM((bq + BLOCK, 128), F32),
                 pltpu.VMEM((BLOCK, 128), F32), pltpu.VMEM((BLOCK, 128), F32)],
        aliases={10: 0},
    )(sinks, qkr, qkr, qkr, p, p, p, dy, cos_t, sin_t, dp)


def _split_dot(x, tri):
    hi = x.astype(BF16)
    lo = (x - hi.astype(F32)).astype(BF16)
    return _dot(hi, tri) + _dot(lo, tri)


def _sb_logits(qs, kj):
    z = _dot_nt(qs, kj)
    e = jnp.exp(-jnp.abs(z))
    return z, e, -(jnp.maximum(z, 0.0) + jnp.log(1.0 + e))


def _sb_before(tq, qpos0, kpos0):
    qpos = qpos0 + lax.broadcasted_iota(jnp.int32, (tq, tq), 0)
    kpos = kpos0 + lax.broadcasted_iota(jnp.int32, (tq, tq), 1)
    return kpos < qpos


def _sb_tris(kb):
    row = lax.broadcasted_iota(jnp.int32, (kb, kb), 0)
    col = lax.broadcasted_iota(jnp.int32, (kb, kb), 1)
    return jnp.stack([jnp.where(row >= col, 1.0, 0.0), jnp.where(row > col, 1.0, 0.0)]).astype(BF16)


SB_TQ = 256
SB_NSUB = 2


def _sb_tiles(qi, w, tq):
    for t in range(SB_NSUB):
        jb = qi - SB_NSUB * w - t
        yield pl.ds(pl.multiple_of(jnp.maximum(jb, 0) * tq, tq), tq), jb, jb >= 0


def _sb_fwd(p, tris):
    T = p.shape[0]
    tq = min(SB_TQ, T)
    scale = SB_HEAD_DIM ** -0.5

    def body(q_ref, gt_ref, k_ref, v_ref, tri_ref, o_ref, y_ref, acc, r_ref):
        qi = pl.program_id(1)
        qs = (q_ref[...].astype(F32) * scale).astype(BF16)
        acc[...] = jnp.zeros_like(acc)
        r_ref[...] = jnp.zeros_like(r_ref)

        def step(carry):
            w, _ = carry
            r = r_ref[...]
            out = None
            for t, (rows, jb, live) in enumerate(_sb_tiles(qi, w, tq)):
                keep = _sb_before(tq, qi * tq, jb * tq) if t == 0 else live
                z, _, lf = _sb_logits(qs, k_ref[rows, :])
                lf = jnp.where(keep, lf, 0.0)
                a = jnp.where(keep, jnp.exp(z + _split_dot(lf, tri_ref[0]) + r), 0.0)
                term = _dot(a.astype(BF16), v_ref[rows, :])
                out = term if out is None else out + term
                r = r + jnp.sum(lf, axis=-1, keepdims=True)
            acc[...] += out
            r_ref[...] = r
            return w + 1, jnp.max(r) > SB_EXIT

        lax.while_loop(lambda c: jnp.logical_and(qi - SB_NSUB * c[0] >= 0, c[1]), step, (0, True))
        o = acc[...]
        o_ref[...] = o
        sg, _ = _silu_and_grad(gt_ref[...].astype(F32))
        y_ref[...] = (o * sg).astype(y_ref.dtype)

    col = lambda k: pl.BlockSpec((tq, 128), lambda h, i: (i, N_C // 128 + 4 * h + k))
    whole = lambda k: pl.BlockSpec((T, 128), lambda h, i: (0, N_C // 128 + 4 * h + k))
    out = pl.BlockSpec((tq, 128), lambda h, i: (i, h))
    return _call(
        body, name="sb_fwd", grid=(SB_HEADS, T // tq),
        in_specs=[col(0), col(1), whole(2), whole(3), _full((2, tq, tq))],
        out_specs=[out, out],
        out_shape=[jax.ShapeDtypeStruct((T, 512), F32), jax.ShapeDtypeStruct((T, 512), BF16)],
        scratch=[pltpu.VMEM((tq, 128), F32), pltpu.VMEM((tq, 1), F32)],
    )(p, p, p, p, tris)


def _sb_bwd(p, o, dy, tris, dp):
    T = p.shape[0]
    tq = min(SB_TQ, T)
    nq = T // tq
    scale = SB_HEAD_DIM ** -0.5

    def body(q_ref, gt_ref, k_ref, v_ref, o_ref, dy_ref, tri_ref, _, dqg_ref, dk_hbm, dv_hbm,
             dq_acc, r_ref, s_ref, dk_acc, dv_acc, sem):
        h, qi = pl.program_id(0), pl.program_id(1)

        @pl.when(qi == 0)
        def _():
            dk_acc[...] = jnp.zeros_like(dk_acc)
            dv_acc[...] = jnp.zeros_like(dv_acc)

        qs = (q_ref[...].astype(F32) * scale).astype(BF16)
        of = o_ref[...]
        dyf = dy_ref[...].astype(F32)
        sg, dsg = _silu_and_grad(gt_ref[...].astype(F32))
        do = dyf * sg
        dqg_ref[:, 128:256] = (dyf * of * dsg).astype(dqg_ref.dtype)
        delta = jnp.sum(do * of, axis=-1, keepdims=True)
        dob = do.astype(BF16)
        dq_acc[...] = jnp.zeros_like(dq_acc)
        r_ref[...] = jnp.zeros_like(r_ref)
        s_ref[...] = jnp.zeros_like(s_ref)

        def tile(rows, keep, r_in, s_in):
            kj, vj = k_ref[rows, :], v_ref[rows, :]
            z, e, lf = _sb_logits(qs, kj)
            lf = jnp.where(keep, lf, 0.0)
            a = jnp.where(keep, jnp.exp(z + _split_dot(lf, tri_ref[0]) + r_in), 0.0)
            gz = a * _dot_nt(dob, vj)
            later = _split_dot(gz, tri_ref[1]) + s_in
            sig = jnp.where(z >= 0.0, 1.0, e) / (1.0 + e)
            dz = jnp.where(keep, gz - sig * (delta - later), 0.0)
            dk_acc[rows, :] += _dot(dz.T.astype(BF16), qs)
            dv_acc[rows, :] += _dot(a.T.astype(BF16), dob)
            return (_dot(dz.astype(BF16), kj), jnp.sum(lf, axis=-1, keepdims=True),
                    jnp.sum(gz, axis=-1, keepdims=True))

        def step(carry):
            w, _ = carry
            r, sm, dq = r_ref[...], s_ref[...], None
            for t, (rows, jb, live) in enumerate(_sb_tiles(qi, w, tq)):
                dqt, lt, gt = tile(rows, _sb_before(tq, qi * tq, jb * tq) if t == 0 else live, r, sm)
                dq = dqt if dq is None else dq + dqt
                r, sm = r + lt, sm + gt
            dq_acc[...] += dq
            s_ref[...] = sm
            r_ref[...] = r
            return w + 1, jnp.max(r) > SB_EXIT

        lax.while_loop(lambda c: jnp.logical_and(qi - SB_NSUB * c[0] >= 0, c[1]), step, (0, True))
        dqg_ref[:, 0:128] = (dq_acc[...] * scale).astype(dqg_ref.dtype)

        @pl.when(qi == nq - 1)
        def _():
            cols = pl.ds(pl.multiple_of(h * 128, 128), 128)
            ck = pltpu.make_async_copy(dk_acc, dk_hbm.at[:, cols], sem.at[0])
            cv = pltpu.make_async_copy(dv_acc, dv_hbm.at[:, cols], sem.at[1])
            ck.start()
            cv.start()
            ck.wait()
            cv.wait()

    col = lambda k: pl.BlockSpec((tq, 128), lambda h, i: (i, N_C // 128 + 4 * h + k))
    whole = lambda k: pl.BlockSpec((T, 128), lambda h, i: (0, N_C // 128 + 4 * h + k))
    blk = pl.BlockSpec((tq, 128), lambda h, i: (i, h))
    return _call(
        body, name="sb_bwd", grid=(SB_HEADS, nq),
        in_specs=[col(0), col(1), whole(2), whole(3), blk, blk, _full((2, tq, tq)), _HBM],
        out_specs=[pl.BlockSpec((tq, 256), lambda h, i: (i, N_C // 256 + 2 * h)), _HBM, _HBM],
        out_shape=[jax.ShapeDtypeStruct(dp.shape, dp.dtype),
                   jax.ShapeDtypeStruct((T, 512), F32), jax.ShapeDtypeStruct((T, 512), F32)],
        scratch=[pltpu.VMEM((tq, 128), F32), pltpu.VMEM((tq, 1), F32), pltpu.VMEM((tq, 1), F32),
                 pltpu.VMEM((T, 128), F32), pltpu.VMEM((T, 128), F32), pltpu.SemaphoreType.DMA((2,))],
        aliases={7: 0},
    )(p, p, p, p, o, dy, tris, dp)


def _sb_kv_into_dp(dk, dv, dp):
    T = dk.shape[0]
    tm = min(1024, T)

    def body(dk_ref, dv_ref, _, o_ref):
        o_ref[:, 0:128] = dk_ref[...].astype(o_ref.dtype)
        o_ref[:, 128:256] = dv_ref[...].astype(o_ref.dtype)

    blk = pl.BlockSpec((tm, 128), lambda i, h: (i, h))
    return _call(
        body, name="sb_kv_into_dp", grid=(T // tm, SB_HEADS), in_specs=[blk, blk, _HBM],
        out_specs=pl.BlockSpec((tm, 256), lambda i, h: (i, N_C // 256 + 2 * h + 1)),
        out_shape=jax.ShapeDtypeStruct(dp.shape, dp.dtype), aliases={2: 0},
    )(dk, dv, dp)


def _post_fwd(ya, yb, yc, p, bm, wa, wb, wc, wo, g_post, x):
    T = x.shape[0]
    tm = min(512, T)

    def body(ya_ref, yb_ref, yc_ref, la_ref, lb_ref, lc_ref, bm_ref, wa_ref, wb_ref, wc_ref, wo_ref,
             g_ref, x_ref, m_ref, out_ref, xn_ref):
        merged = None
        for k, (y_ref, l_ref, w_ref) in enumerate(
                ((ya_ref, la_ref, wa_ref), (yb_ref, lb_ref, wb_ref), (yc_ref, lc_ref, wc_ref))):
            gate = _sigmoid(l_ref[...].astype(F32) + bm_ref[k:k + 1, :])
            term = gate * _dot(y_ref[...], w_ref[...])
            merged = term if merged is None else merged + term
        mb = merged.astype(BF16)
        m_ref[...] = mb
        out = _dot(mb, wo_ref[...])
        out_ref[...] = out
        r = lax.rsqrt(jnp.mean(out * out, axis=-1, keepdims=True) + EPS)
        xn_ref[...] = x_ref[...] + out * r * g_ref[...]

    yspec = pl.BlockSpec((tm, 512), lambda i: (i, 0))
    lspec = lambda k: pl.BlockSpec((tm, D), lambda i: (i, k))
    row = pl.BlockSpec((tm, D), lambda i: (i, 0))
    return _call(
        body, name="post_fwd", grid=(T // tm,),
        in_specs=[yspec, yspec, yspec, lspec(0), lspec(1), lspec(2), _full((3, D)),
                  _full((512, D)), _full((512, D)), _full((512, D)), _full((D, D)), _full((1, D)), row],
        out_specs=[row, row, row],
        out_shape=[jax.ShapeDtypeStruct((T, D), BF16), jax.ShapeDtypeStruct((T, D), F32),
                   jax.ShapeDtypeStruct((T, D), F32)],
    )(ya, yb, yc, p, p, p, bm, wa, wb, wc, wo, g_post, x)


def _post_bwd(dxn, out, ya, yb, yc, p, bm, wa, wb, wc, wo, g_post):
    T = dxn.shape[0]
    tm = min(256, T)

    def body(dxn_ref, out_ref, ya_ref, yb_ref, yc_ref, la_ref, lb_ref, lc_ref, bm_ref,
             wa_ref, wb_ref, wc_ref, wo_ref, g_ref,
             do_ref, dpa_ref, dpb_ref, dpc_ref, dl_ref, dya_ref, dyb_ref, dyc_ref, dg_ref, db_ref):
        @pl.when(pl.program_id(0) == 0)
        def _():
            dg_ref[...] = jnp.zeros_like(dg_ref)
            db_ref[...] = jnp.zeros_like(db_ref)

        out = out_ref[...]
        dxn_ = dxn_ref[...]
        r = lax.rsqrt(jnp.mean(out * out, axis=-1, keepdims=True) + EPS)
        a = dxn_ * g_ref[...]
        d_out = r * a - out * (r * r * r) * jnp.mean(a * out, axis=-1, keepdims=True)
        dg_ref[...] += jnp.sum(dxn_ * out * r, axis=0, keepdims=True)
        dob = d_out.astype(BF16)
        do_ref[...] = dob
        dmerged = _dot_nt(dob, wo_ref[...])
        for k, (y_ref, l_ref, w_ref, dp_ref, dy_ref) in enumerate((
                (ya_ref, la_ref, wa_ref, dpa_ref, dya_ref), (yb_ref, lb_ref, wb_ref, dpb_ref, dyb_ref),
                (yc_ref, lc_ref, wc_ref, dpc_ref, dyc_ref))):
            gate = _sigmoid(l_ref[...].astype(F32) + bm_ref[k:k + 1, :])
            proj = _dot(y_ref[...], w_ref[...])
            dproj = (dmerged * gate).astype(BF16)
            dp_ref[...] = dproj
            dlog = dmerged * proj * gate * (1.0 - gate)
            dl_ref[:, D * k:D * (k + 1)] = dlog.astype(dl_ref.dtype)
            db_ref[k:k + 1, :] += jnp.sum(dlog, axis=0, keepdims=True)
            dy_ref[...] = _dot_nt(dproj, w_ref[...]).astype(dy_ref.dtype)

    yspec = pl.BlockSpec((tm, 512), lambda i: (i, 0))
    lspec = lambda k: pl.BlockSpec((tm, D), lambda i: (i, k))
    row = pl.BlockSpec((tm, D), lambda i: (i, 0))
    sds = jax.ShapeDtypeStruct
    return _call(
        body, name="post_bwd", grid=(T // tm,),
        in_specs=[row, row, yspec, yspec, yspec, lspec(0), lspec(1), lspec(2), _full((3, D)),
                  _full((512, D)), _full((512, D)), _full((512, D)), _full((D, D)), _full((1, D))],
        out_specs=[row, row, row, row, pl.BlockSpec((tm, 3 * D), lambda i: (i, 0)), yspec, yspec, yspec,
                   _full((1, D)), _full((3, D))],
        out_shape=[sds((T, D), BF16), sds((T, D), BF16), sds((T, D), BF16), sds((T, D), BF16),
                   sds((T, IN_PAD), ACT), sds((T, 512), ACT), sds((T, 512), ACT), sds((T, 512), ACT),
                   sds((1, D), F32), sds((3, D), F32)],
    )(dxn, out, ya, yb, yc, p, p, p, bm, wa, wb, wc, wo, g_post)


def _loss_and_grad(y, target):
    T = y.shape[0]
    tm = min(1024, T)

    def body(y_ref, t_ref, l_ref, dy_ref):
        @pl.when(pl.program_id(0) == 0)
        def _():
            l_ref[...] = jnp.zeros_like(l_ref)
        e = y_ref[...] - t_ref[...]
        l_ref[...] += jnp.sum(e * e, axis=0, keepdims=True)
        dy_ref[...] = e * (1.0 / D)

    row = pl.BlockSpec((tm, D), lambda i: (i, 0))
    return _call(
        body, name="loss", grid=(T // tm,), in_specs=[row, row], out_specs=[_full((1, D)), row],
        out_shape=[jax.ShapeDtypeStruct((1, D), F32), jax.ShapeDtypeStruct((T, D), F32)],
    )(y, target)


def _sibling_swap(srcs, name):
    n = len(srcs)

    def body(*refs):
        s_refs, r_refs = refs[:n], refs[n:2 * n]
        send_sems, recv_sems = refs[2 * n:]
        x, y, c = lax.axis_index("x"), lax.axis_index("y"), lax.axis_index("c")
        copies = [pltpu.make_async_remote_copy(
            src_ref=s_refs[i], dst_ref=r_refs[i], send_sem=send_sems.at[i], recv_sem=recv_sems.at[i],
            device_id=(x, y, 1 - c), device_id_type=MESH) for i in range(n)]
        for cp in copies:
            cp.start()
        for cp in copies:
            cp.wait()

    return pl.pallas_call(
        body, name=name, in_specs=[_HBM] * n, out_specs=[_HBM] * n,
        out_shape=[jax.ShapeDtypeStruct(a.shape, a.dtype) for a in srcs],
        scratch_shapes=[pltpu.SemaphoreType.DMA((n,)), pltpu.SemaphoreType.DMA((n,))],
    )(*srcs)


_ELEMENTWISE_BLOCK_BYTES = 1 << 20


def _row_tile(rows, cols):
    if rows * cols * 4 <= 2 * _ELEMENTWISE_BLOCK_BYTES:
        return rows
    for tr in (2048, 1024, 512, 256, 128, 64, 32, 16, 8):
        if rows % tr == 0 and tr * cols * 4 <= _ELEMENTWISE_BLOCK_BYTES:
            return tr
    raise ValueError(f"no row tile for {(rows, cols)}")


def _sum_chips(r, name):
    _, R, C = r.shape
    tr = _row_tile(R, C)

    def body(r_ref, o_ref):
        f = lambda j: r_ref[j].astype(F32)
        o_ref[...] = ((f(0) + f(1)) + f(2)) + f(3)

    return _call(
        body, name=name, grid=(R // tr,),
        in_specs=[pl.BlockSpec((4, tr, C), lambda i: (0, i, 0))],
        out_specs=pl.BlockSpec((tr, C), lambda i: (i, 0)),
        out_shape=jax.ShapeDtypeStruct((R, C), F32),
    )(r)


def _adamw(w, m, v, g_mine, g_other, name):
    R, C = w.shape
    tr = _row_tile(R, C)

    def body(w_ref, m_ref, v_ref, a_ref, b_ref, g_ref, d_ref, nm_ref, nv_ref):
        g = a_ref[...] + b_ref[...]
        g_ref[...] = g
        m_new = ADAM_B1 * m_ref[...] + (1.0 - ADAM_B1) * g
        v_new = ADAM_B2 * v_ref[...] + (1.0 - ADAM_B2) * (g * g)
        nm_ref[...] = m_new
        nv_ref[...] = v_new
        m_hat = m_new / (1.0 - ADAM_B1 ** ADAM_STEP)
        v_hat = v_new / (1.0 - ADAM_B2 ** ADAM_STEP)
        d_ref[...] = -ADAM_LR * (m_hat / (jnp.sqrt(v_hat) + ADAM_EPS) + ADAM_WD * w_ref[...])

    blk = pl.BlockSpec((tr, C), lambda i: (i, 0))
    sds = jax.ShapeDtypeStruct((R, C), F32)
    return _call(body, name=name, grid=(R // tr,), in_specs=[blk] * 5, out_specs=[blk] * 4,
                 out_shape=[sds] * 4)(w, m, v, g_mine, g_other)


_NAMES = ('w_in', 'gm_w_s', 'gm_b_s', 'gm_norm_gain', 'sw_sinks', 'w_branch_a', 'w_branch_b',
          'w_branch_c', 'b_merge', 'w_out', 'g_pre', 'g_post')
_SMALL = ('gm_w_s', 'gm_b_s', 'gm_norm_gain', 'sw_sinks', 'g_pre', 'g_post')
_PACK_COLS = 1024


def _pack(arrays):
    flat = jnp.concatenate([a.reshape(-1).astype(F32) for a in arrays])
    rows = -(-flat.shape[0] // (8 * _PACK_COLS)) * 8
    return jnp.pad(flat, (0, rows * _PACK_COLS - flat.shape[0])).reshape(rows, _PACK_COLS)


def _unpack(buf, shapes):
    flat = buf.reshape(-1)
    out, off = [], 0
    for s in shapes:
        n = int(np.prod(s))
        out.append(flat[off:off + n].reshape(s))
        off += n
    return out


def _permute_cols(w):
    return jnp.concatenate(
        [jnp.zeros(w.shape[:-1] + (n,), w.dtype) if o is None else w[..., o:o + n] for o, n in _PERM], axis=-1)


def _unpermute_cols(w):
    new_off, off = {}, 0
    for o, n in _PERM:
        if o is not None:
            new_off[o] = (off, n)
        off += n
    return jnp.concatenate([w[..., new_off[o][0]:new_off[o][0] + new_off[o][1]] for o in sorted(new_off)], axis=-1)


def _tables(T):
    half = HEAD_DIM // 2
    freqs = ROPE_THETA ** (-jnp.arange(half, dtype=F32) / half)
    ang = jnp.arange(T).astype(F32)[:, None] * freqs[None, :]
    cos, sin = jnp.cos(ang), jnp.sin(ang)
    return (jnp.tile(jnp.concatenate([cos, cos], axis=1), (1, 2)),
            jnp.tile(jnp.concatenate([-sin, sin], axis=1), (1, 2)), _sb_tris(min(SB_TQ, T)))


def _cat_chips(got, axis):
    return jnp.concatenate([got[j] for j in range(4)], axis=axis)


def _layer_fwd(x, lw, tables, carry):
    gpre, gpost = lw['g_pre'][None, :], lw['g_post'][None, :]
    bfull = jnp.broadcast_to(lw['bs'][:, :, None], (GM_GROUPS, BLOCK, BLOCK))
    gv = lw['gv'][None, :]
    p, h, *got = _in_proj(x, gpre, lw['w_in'], carry)
    wabc = _cat_chips(got[0], 2)
    lw = dict(lw, wa=wabc[0], wb=wabc[1], wc=wabc[2], bm=_cat_chips(got[1], 1), wo=_cat_chips(got[2], 0))
    got = _permute_cols(_cat_chips(got[3], 1)) if len(got) > 3 else None
    ya = _gmlp_fwd(p, lw['ws'], bfull, gv)
    qkr = _rope_fwd(p, *tables[:2])
    yb = _swa_fwd(qkr, p, lw['sinks'])
    oc, yc = _sb_fwd(p, tables[2])
    merged, out, xn = _post_fwd(ya, yb, yc, p, lw['bm'], lw['wa'], lw['wb'], lw['wc'], lw['wo'], gpost, x)
    return xn, (lw, p, h, ya, qkr, yb, oc, yc, merged, out, bfull, gv, gpre, gpost), got


def _grad_partials(g):
    per_chip = lambda a, axis: jnp.stack(jnp.split(a, 4, axis=axis))
    g_abc = jnp.stack([g['w_branch_a'], g['w_branch_b'], g['w_branch_c']])
    return [per_chip(_unpermute_cols(g['w_in']), 1).astype(BF16),
            per_chip(g_abc, 2).astype(BF16),
            per_chip(g['b_merge'], 1),
            per_chip(g['w_out'], 0).astype(BF16)]


def _layer_bwd(x, saved, dxn, tables):
    lw, p, h, ya, qkr, yb, oc, yc, merged, out, bfull, gv, gpre, gpost = saved
    g = {}
    (d_out, dpa, dpb, dpc, dp, dya, dyb, dyc, dgpost, dbm) = _post_bwd(
        dxn, out, ya, yb, yc, p, lw['bm'], lw['wa'], lw['wb'], lw['wc'], lw['wo'], gpost)
    g['w_out'] = _matmul_tn(merged, d_out, "grad_w_out")
    g['w_branch_a'] = _matmul_tn(ya, dpa, "grad_w_a")
    g['w_branch_b'] = _matmul_tn(yb, dpb, "grad_w_b")
    g['w_branch_c'] = _matmul_tn(yc, dpc, "grad_w_c")
    g['g_post'] = dgpost[0]
    g['b_merge'] = dbm
    dp, dws, dbacc, dgv = _gmlp_bwd(p, dya, lw['ws'], bfull, gv, dp)
    g['gm_w_s'] = dws
    g['gm_b_s'] = jnp.sum(dbacc.reshape(BLOCK, GM_GROUPS, BLOCK), axis=2).T
    g['gm_norm_gain'] = dgv[0]
    dp, dsink = _swa_bwd(qkr, p, dyb, lw['sinks'], *tables[:2], dp)
    g['sw_sinks'] = dsink[:, 0]
    dp, dkc, dvc = _sb_bwd(p, oc, dyc, tables[2], dp)
    dp = _sb_kv_into_dp(dkc, dvc, dp)
    g['w_in'] = _matmul_tn(h, dp, "grad_w_in")
    dx, dgpre, *got = _in_bwd(dp, lw['w_in'], x, dxn, gpre, _grad_partials(g))
    g['g_pre'] = dgpre[0]
    return dx, g, got


def kernel(x, w_in, gm_w_s, gm_b_s, gm_norm_gain, sw_sinks, w_branch_a, w_branch_b, w_branch_c, b_merge, w_out, g_pre, g_post, loss_target, m_w_in, m_gm_w_s, m_gm_b_s, m_gm_norm_gain, m_sw_sinks, m_w_branch_a, m_w_branch_b, m_w_branch_c, m_b_merge, m_w_out, m_g_pre, m_g_post, v_w_in, v_gm_w_s, v_gm_b_s, v_gm_norm_gain, v_sw_sinks, v_w_branch_a, v_w_branch_b, v_w_branch_c, v_b_merge, v_w_out, v_g_pre, v_g_post):
    T = x.shape[1]
    weights = dict(zip(_NAMES, (w_in, gm_w_s, gm_b_s, gm_norm_gain, sw_sinks, w_branch_a, w_branch_b,
                                w_branch_c, b_merge, w_out, g_pre, g_post)))
    mom_m = dict(zip(_NAMES, (m_w_in, m_gm_w_s, m_gm_b_s, m_gm_norm_gain, m_sw_sinks, m_w_branch_a,
                              m_w_branch_b, m_w_branch_c, m_b_merge, m_w_out, m_g_pre, m_g_post)))
    mom_v = dict(zip(_NAMES, (v_w_in, v_gm_w_s, v_gm_b_s, v_gm_norm_gain, v_sw_sinks, v_w_branch_a,
                              v_w_branch_b, v_w_branch_c, v_b_merge, v_w_out, v_g_pre, v_g_post)))
    abc = lambda d: jnp.stack([d['w_branch_a'], d['w_branch_b'], d['w_branch_c']], axis=1)

    w_in_b, rest = w_in.astype(BF16), [abc(weights).astype(BF16), b_merge, w_out.astype(BF16)]
    tables = _tables(T)
    xs = [x[0]]
    saved = []
    got, = _chip_exchange([w_in_b[0]], (False,), "gather_w_in")
    w_in_l = _permute_cols(_cat_chips(got, 1))
    for l in range(DEPTH):
        lw = dict(w_in=w_in_l, ws=gm_w_s[l], bs=gm_b_s[l], gv=gm_norm_gain[l], sinks=sw_sinks[l],
                  g_pre=g_pre[l], g_post=g_post[l])
        carry = [a[l] for a in rest] + ([w_in_b[l + 1]] if l + 1 < DEPTH else [])
        xn, sv, w_in_l = _layer_fwd(xs[l], lw, tables, carry)
        xs.append(xn)
        saved.append(sv)

    lsum, dxn = _loss_and_grad(xs[DEPTH], loss_target[0])
    loss = lax.psum(0.5 * jnp.sum(lsum) / D, ("x", "y", "c"))

    small = {n: [None] * DEPTH for n in _SMALL}
    received = [None] * DEPTH
    for l in reversed(range(DEPTH)):
        dxn, g, received[l] = _layer_bwd(xs[l], saved[l], dxn, tables)
        for n in _SMALL:
            small[n][l] = g[n]
    grad_x = dxn[None]
    g_small = _pack([jnp.stack(small[n]) for n in _SMALL])
    got_small, = _chip_exchange([g_small], (False,), "gather_small_grads")

    views = [(4 * D, IN_WIDTH // 4), (4 * 3 * 512, D // 4), (4 * 3, D // 4), (D, D), g_small.shape]
    stacks = [jnp.stack([received[l][i] for l in range(DEPTH)], axis=1) for i in range(4)] + [got_small]
    sums = [_sum_chips(r.reshape((4,) + v), f"sum_chips_{i}") for i, (r, v) in enumerate(zip(stacks, views))]
    others = _sibling_swap(sums, "swap_core_sums")
    tensors = [lambda d: d['w_in'], abc, lambda d: d['b_merge'], lambda d: d['w_out'],
               lambda d: _pack([d[n] for n in _SMALL])]
    res = [_adamw(t(weights).reshape(v), t(mom_m).reshape(v), t(mom_v).reshape(v), s_, o_, f"adamw_{i}")
           for i, (t, v, s_, o_) in enumerate(zip(tensors, views, sums, others))]

    outs = []
    for kind in range(4):
        r = [res[i][kind] for i in range(5)]
        r_abc = r[1].reshape(4, 3, 512, D // 4)
        tiny = dict(zip(_SMALL, _unpack(r[4], [weights[n].shape for n in _SMALL])))
        big = dict(w_in=r[0].reshape(w_in.shape), w_branch_a=r_abc[:, 0], w_branch_b=r_abc[:, 1],
                   w_branch_c=r_abc[:, 2], b_merge=r[2].reshape(b_merge.shape), w_out=r[3].reshape(w_out.shape))
        outs.extend(big[n] if n in big else tiny[n] for n in _NAMES)
    return (loss, grad_x, *outs)
```

```python
import functools

import numpy as np
import jax
import jax.numpy as jnp
from jax import lax
from jax.experimental import pallas as pl
from jax.experimental.pallas import tpu as pltpu

F32 = jnp.float32
BF16 = jnp.bfloat16
ACT = jnp.bfloat16

D = 1024
DEPTH = 4
BLOCK = 128
EPS = 1e-6
NEG = -1e30
GM_GROUPS = 4
GM_WIDTH = 512
HEAD_DIM = 64
SW_HEADS = 8
SB_HEADS = 4
SB_HEAD_DIM = 128
ROPE_THETA = 10000.0
IN_WIDTH = 7936
IN_PAD = 8192

O_UA, O_VA, O_GA, O_QB, O_KB, O_VB, O_GB = 0, 512, 1024, 1536, 2048, 2176, 2304
O_QC, O_KC, O_VC, O_GC, O_MG = 2816, 3328, 3840, 4352, 4864
_PERM = ((O_MG, 3072), (O_UA, 512), (O_VA, 512), (O_GA, 512),
         (O_QB, 512), (O_KB, 128), (O_VB, 128), (None, IN_PAD - IN_WIDTH), (O_GB, 512)) + tuple(
    (o + 128 * h, 128) for h in range(4) for o in (O_QC, O_GC, O_KC, O_VC))
N_MG, N_UA, N_VA, N_GA = 0, 3072, 3584, 4096
N_QB, N_KB, N_VB, N_GB = 4608, 5120, 5248, 5632
N_C = 6144

ADAM_LR, ADAM_B1, ADAM_B2, ADAM_EPS, ADAM_WD, ADAM_STEP = 0.001, 0.9, 0.999, 1e-08, 0.01, 10

SB_EXIT = -104.0
V7X_VMEM_LIMIT = 56 * 1024 * 1024
MESH = pl.DeviceIdType.MESH


_HBM = pl.BlockSpec(memory_space=pl.ANY)


def _call(body, *, name, grid, in_specs, out_specs, out_shape, scratch=(), aliases=None):
    return pl.pallas_call(
        body, name=name, grid=grid, in_specs=in_specs, out_specs=out_specs, out_shape=out_shape,
        scratch_shapes=list(scratch), input_output_aliases=aliases or {},
        compiler_params=pltpu.CompilerParams(
            dimension_semantics=("arbitrary",) * len(grid), vmem_limit_bytes=V7X_VMEM_LIMIT))


def _sigmoid(x):
    return 1.0 / (1.0 + jnp.exp(-x))


def _silu_and_grad(x):
    s = _sigmoid(x)
    return x * s, s * (1.0 + x * (1.0 - s))


def _dot(a, b):
    return jnp.dot(a, b, preferred_element_type=F32)


def _dot_nt(a, b):
    return lax.dot_general(a, b, (((1,), (1,)), ((), ())), preferred_element_type=F32)


def _dot_tn(a, b):
    return lax.dot_general(a, b, (((0,), (0,)), ((), ())), preferred_element_type=F32)


def _full(shape):
    return pl.BlockSpec(shape, lambda *_: (0,) * len(shape))


_CHIP_STEPS = ((1, 0), (0, 1), (1, 1))


def _exchange_ops(s_refs, r_refs, send_sems, recv_sems, local_sems, per_target):
    n = len(s_refs)

    def copies():
        x, y, c = lax.axis_index("x"), lax.axis_index("y"), lax.axis_index("c")
        me = 2 * x + y
        pick = lambda i, j: s_refs[i].at[j] if per_target[i] else s_refs[i]
        own = [pltpu.make_async_copy(pick(i, me), r_refs[i].at[me], local_sems.at[i]) for i in range(n)]
        sent, arriving = [], []
        for k, (dx, dy) in enumerate(_CHIP_STEPS):
            tx, ty = (x + dx) % 2, (y + dy) % 2
            peer = 2 * tx + ty
            for i in range(n):
                sems = dict(send_sem=send_sems.at[3 * i + k], recv_sem=recv_sems.at[3 * i + k])
                sent.append(pltpu.make_async_remote_copy(
                    src_ref=pick(i, peer), dst_ref=r_refs[i].at[me], device_id=(tx, ty, c),
                    device_id_type=MESH, **sems))
                arriving.append(pltpu.make_async_remote_copy(
                    src_ref=pick(i, peer), dst_ref=r_refs[i].at[peer], device_id=(x, y, c),
                    device_id_type=MESH, **sems))
        return own, sent, arriving

    def start():
        own, sent, _ = copies()
        for cp in own + sent:
            cp.start()

    def wait():
        own, sent, arriving = copies()
        for cp in arriving:
            cp.wait_recv()
        for cp in sent:
            cp.wait_send()
        for cp in own:
            cp.wait()

    return start, wait


def _exchange_shapes(srcs, per_target):
    return [jax.ShapeDtypeStruct((4,) + (a.shape[1:] if pt else a.shape), a.dtype)
            for a, pt in zip(srcs, per_target)]


def _exchange_sems(n):
    return [pltpu.SemaphoreType.DMA((3 * n,)), pltpu.SemaphoreType.DMA((3 * n,)), pltpu.SemaphoreType.DMA((n,))]


def _chip_exchange(srcs, per_target, name):
    n = len(srcs)

    def body(*refs):
        start, wait = _exchange_ops(refs[:n], refs[n:2 * n], *refs[2 * n:], per_target)
        start()
        wait()

    return pl.pallas_call(
        body, name=name, in_specs=[_HBM] * n, out_specs=[_HBM] * n, out_shape=_exchange_shapes(srcs, per_target),
        scratch_shapes=_exchange_sems(n),
    )(*srcs)


def _in_proj(x, g_pre, w_in, carry=()):
    T = x.shape[0]
    tm, tn = min(2048, T), 1024
    ni, nj, n = T // tm, IN_PAD // tn, len(carry)

    def body(x_ref, g_ref, w_ref, *rest):
        p_ref, h_ref = rest[n], rest[n + 1]
        i, j = pl.program_id(0), pl.program_id(1)
        if n:
            start, wait = _exchange_ops(rest[:n], rest[n + 2:2 * n + 2], *rest[2 * n + 2:], (False,) * n)
            pl.when((i == 0) & (j == 0))(start)

        @pl.when(j == 0)
        def _():
            xf = x_ref[...]
            r = lax.rsqrt(jnp.mean(xf * xf, axis=-1, keepdims=True) + EPS)
            h_ref[...] = (xf * r * g_ref[...]).astype(BF16)
        p_ref[...] = _dot(h_ref[...], w_ref[...]).astype(p_ref.dtype)
        if n:
            pl.when((i == ni - 1) & (j == nj - 1))(wait)

    return _call(
        body, name="in_proj_gather" if n else "in_proj", grid=(ni, nj),
        in_specs=[pl.BlockSpec((tm, D), lambda i, j: (i, 0)), _full((1, D)),
                  pl.BlockSpec((D, tn), lambda i, j: (0, j))] + [_HBM] * n,
        out_specs=[pl.BlockSpec((tm, tn), lambda i, j: (i, j)), pl.BlockSpec((tm, D), lambda i, j: (i, 0))]
        + [_HBM] * n,
        out_shape=[jax.ShapeDtypeStruct((T, IN_PAD), ACT), jax.ShapeDtypeStruct((T, D), BF16)]
        + _exchange_shapes(carry, (False,) * n),
        scratch=_exchange_sems(n) if n else (),
    )(x, g_pre, w_in, *carry)


def _in_bwd(dp, w_in, x, dxn, g_pre, carry=()):
    T = x.shape[0]
    tm, tk = min(1024, T), 2048
    ni, nk, n = T // tm, IN_PAD // tk, len(carry)

    def body(dp_ref, w_ref, x_ref, dxn_ref, g_ref, *rest):
        dx_ref, dg_ref, acc = rest[n], rest[n + 1], rest[2 * n + 2]
        i, k = pl.program_id(0), pl.program_id(1)
        if n:
            start, wait = _exchange_ops(rest[:n], rest[n + 2:2 * n + 2], *rest[2 * n + 3:], (True,) * n)
            pl.when((i == 0) & (k == 0))(start)

        @pl.when(k == 0)
        def _():
            acc[...] = jnp.zeros_like(acc)

        @pl.when((i == 0) & (k == 0))
        def _():
            dg_ref[...] = jnp.zeros_like(dg_ref)

        acc[...] += _dot_nt(dp_ref[...], w_ref[...])

        @pl.when(k == nk - 1)
        def _():
            dh = acc[...]
            xf = x_ref[...]
            r = lax.rsqrt(jnp.mean(xf * xf, axis=-1, keepdims=True) + EPS)
            a = dh * g_ref[...]
            dx_ref[...] = dxn_ref[...] + r * a - xf * (r * r * r) * jnp.mean(a * xf, axis=-1, keepdims=True)
            dg_ref[...] += jnp.sum(dh * xf * r, axis=0, keepdims=True)

        if n:
            pl.when((i == ni - 1) & (k == nk - 1))(wait)

    return _call(
        body, name="in_bwd_scatter" if n else "in_bwd", grid=(ni, nk),
        in_specs=[pl.BlockSpec((tm, tk), lambda i, k: (i, k)), pl.BlockSpec((D, tk), lambda i, k: (0, k)),
                  pl.BlockSpec((tm, D), lambda i, k: (i, 0)), pl.BlockSpec((tm, D), lambda i, k: (i, 0)),
                  _full((1, D))] + [_HBM] * n,
        out_specs=[pl.BlockSpec((tm, D), lambda i, k: (i, 0)), _full((1, D))] + [_HBM] * n,
        out_shape=[jax.ShapeDtypeStruct((T, D), F32), jax.ShapeDtypeStruct((1, D), F32)]
        + _exchange_shapes(carry, (True,) * n),
        scratch=[pltpu.VMEM((tm, D), F32)] + (_exchange_sems(n) if n else []),
    )(dp, w_in, x, dxn, g_pre, *carry)


def _matmul_tn(a, b, name):
    T, K = a.shape
    N = b.shape[1]
    tk, tn, tt = min(K, 1024), min(N, 2048), min(T, 512)

    def body(a_ref, b_ref, o_ref):
        @pl.when(pl.program_id(2) == 0)
        def _():
            o_ref[...] = jnp.zeros_like(o_ref)
        o_ref[...] += _dot_tn(a_ref[...], b_ref[...])

    return _call(
        body, name=name, grid=(K // tk, N // tn, T // tt),
        in_specs=[pl.BlockSpec((tt, tk), lambda i, j, t: (t, i)), pl.BlockSpec((tt, tn), lambda i, j, t: (t, j))],
        out_specs=pl.BlockSpec((tk, tn), lambda i, j, t: (i, j)),
        out_shape=jax.ShapeDtypeStruct((K, N), F32),
    )(a, b)


def _gm_forward_parts(v_ref, gv_ref, ws_ref, bf_ref, nch):
    vf = v_ref[...].astype(F32)
    mu = jnp.mean(vf, axis=-1, keepdims=True)
    xc = vf - mu
    rstd = lax.rsqrt(jnp.mean(xc * xc, axis=-1, keepdims=True) + EPS)
    xhat = xc * rstd
    vnb = (xhat * gv_ref[...]).astype(BF16)
    row = lax.broadcasted_iota(jnp.int32, (BLOCK, BLOCK), 0)
    col = lax.broadcasted_iota(jnp.int32, (BLOCK, BLOCK), 1)
    mixed, vcats, wgs = [], [], []
    for g in range(GM_GROUPS):
        vg = vnb[:, BLOCK * g:BLOCK * (g + 1)]
        vcat = jnp.concatenate([vg[BLOCK * k:BLOCK * (k + 1), :] for k in range(nch)], axis=1)
        wg = jnp.where(row >= col, ws_ref[g], 0.0)
        m = _dot(wg.astype(BF16), vcat)
        mixed.append(jnp.concatenate(
            [m[:, BLOCK * k:BLOCK * (k + 1)] + bf_ref[g] for k in range(nch)], axis=0))
        vcats.append(vcat)
        wgs.append(wg)
    return xhat, rstd, jnp.concatenate(mixed, axis=1), vcats, wgs, row >= col


def _gmlp_fwd(p, ws, bfull, gv):
    T = p.shape[0]
    tm = min(512, T)
    nch = tm // BLOCK

    def body(u_ref, v_ref, gt_ref, ws_ref, bf_ref, gv_ref, y_ref):
        _, _, mixed, _, _, _ = _gm_forward_parts(v_ref, gv_ref, ws_ref, bf_ref, nch)
        sg, _ = _silu_and_grad(gt_ref[...].astype(F32))
        y_ref[...] = (u_ref[...].astype(F32) * mixed * sg).astype(y_ref.dtype)

    seg = lambda off: pl.BlockSpec((tm, 512), lambda i: (i, off // 512))
    return _call(
        body, name="gmlp_fwd", grid=(T // tm,),
        in_specs=[seg(N_UA), seg(N_VA), seg(N_GA), _full((GM_GROUPS, BLOCK, BLOCK)),
                  _full((GM_GROUPS, BLOCK, BLOCK)), _full((1, GM_WIDTH))],
        out_specs=pl.BlockSpec((tm, 512), lambda i: (i, 0)),
        out_shape=jax.ShapeDtypeStruct((T, GM_WIDTH), BF16),
    )(p, p, p, ws, bfull, gv)


def _gmlp_bwd(p, dy, ws, bfull, gv, dp):
    T = p.shape[0]
    tm = min(512, T)
    nch = tm // BLOCK

    def body(u_ref, v_ref, gt_ref, dy_ref, ws_ref, bf_ref, gv_ref, _, dp_ref, dws_ref, db_ref, dgv_ref):
        @pl.when(pl.program_id(0) == 0)
        def _():
            dws_ref[...] = jnp.zeros_like(dws_ref)
            db_ref[...] = jnp.zeros_like(db_ref)
            dgv_ref[...] = jnp.zeros_like(dgv_ref)

        xhat, rstd, mixed, vcats, wgs, tril = _gm_forward_parts(v_ref, gv_ref, ws_ref, bf_ref, nch)
        u = u_ref[...].astype(F32)
        gt = gt_ref[...].astype(F32)
        dyf = dy_ref[...].astype(F32)
        sg, dsg = _silu_and_grad(gt)
        du = dyf * mixed * sg
        dmixed = dyf * u * sg
        dgate = dyf * (u * mixed) * dsg
        dvn = []
        for g in range(GM_GROUPS):
            dmg = dmixed[:, BLOCK * g:BLOCK * (g + 1)]
            chunks = [dmg[BLOCK * k:BLOCK * (k + 1), :] for k in range(nch)]
            dmcat = jnp.concatenate(chunks, axis=1).astype(BF16)
            dws_ref[g] += jnp.where(tril, _dot_nt(dmcat, vcats[g]), 0.0)
            dvcat = _dot(wgs[g].T.astype(BF16), dmcat)
            dvn.append(jnp.concatenate([dvcat[:, BLOCK * k:BLOCK * (k + 1)] for k in range(nch)], axis=0))
            db_ref[:, BLOCK * g:BLOCK * (g + 1)] += functools.reduce(lambda a, b: a + b, chunks)
        dvn = jnp.concatenate(dvn, axis=1)
        dgv_ref[...] += jnp.sum(dvn * xhat, axis=0, keepdims=True)
        dxh = dvn * gv_ref[...]
        dv = rstd * (dxh - jnp.mean(dxh, axis=-1, keepdims=True)
                     - xhat * jnp.mean(dxh * xhat, axis=-1, keepdims=True))
        dp_ref[:, 0:512] = du.astype(dp_ref.dtype)
        dp_ref[:, 512:1024] = dv.astype(dp_ref.dtype)
        dp_ref[:, 1024:1536] = dgate.astype(dp_ref.dtype)

    seg = lambda off: pl.BlockSpec((tm, 512), lambda i: (i, off // 512))
    return _call(
        body, name="gmlp_bwd", grid=(T // tm,),
        in_specs=[seg(N_UA), seg(N_VA), seg(N_GA), pl.BlockSpec((tm, 512), lambda i: (i, 0)),
                  _full((GM_GROUPS, BLOCK, BLOCK)), _full((GM_GROUPS, BLOCK, BLOCK)), _full((1, GM_WIDTH)), _HBM],
        out_specs=[pl.BlockSpec((tm, 1536), lambda i: (i, N_UA // 1536)), _full((GM_GROUPS, BLOCK, BLOCK)),
                   _full((BLOCK, GM_WIDTH)), _full((1, GM_WIDTH))],
        out_shape=[jax.ShapeDtypeStruct(dp.shape, dp.dtype), jax.ShapeDtypeStruct((GM_GROUPS, BLOCK, BLOCK), F32),
                   jax.ShapeDtypeStruct((BLOCK, GM_WIDTH), F32), jax.ShapeDtypeStruct((1, GM_WIDTH), F32)],
        aliases={7: 0},
    )(p, p, p, dy, ws, bfull, gv, dp)


def _swap_halves(x):
    lane = lax.broadcasted_iota(jnp.int32, x.shape, 1) % HEAD_DIM
    return jnp.where(lane < HEAD_DIM // 2, pltpu.roll(x, 96, 1), pltpu.roll(x, 32, 1))


def _rope_fwd(p, cos_t, sin_t):
    T = p.shape[0]
    tm = min(512, T)

    def body(q_ref, k_ref, c_ref, s_ref, o_ref):
        c, s = c_ref[...], s_ref[...]
        for G in range(5):
            xg = (q_ref[:, 128 * G:128 * (G + 1)] if G < 4 else k_ref[...]).astype(F32)
            o_ref[:, 128 * G:128 * (G + 1)] = (xg * c + _swap_halves(xg) * s).astype(o_ref.dtype)

    return _call(
        body, name="rope_fwd", grid=(T // tm,),
        in_specs=[pl.BlockSpec((tm, 512), lambda i: (i, N_QB // 512)),
                  pl.BlockSpec((tm, 128), lambda i: (i, N_KB // 128)),
                  pl.BlockSpec((tm, 128), lambda i: (i, 0)), pl.BlockSpec((tm, 128), lambda i: (i, 0))],
        out_specs=pl.BlockSpec((tm, 640), lambda i: (i, 0)),
        out_shape=jax.ShapeDtypeStruct((T, 640), BF16),
    )(p, p, cos_t, sin_t)


def _unrotate(d, c, s):
    return d * c + _swap_halves(d * s)


def _dup_heads(x):
    left = lax.broadcasted_iota(jnp.int32, x.shape, 1) < HEAD_DIM
    r = pltpu.roll(x, HEAD_DIM, 1)
    return jnp.where(left, x, r), jnp.where(left, r, x)


def _fold_heads(acc0, acc1):
    left = lax.broadcasted_iota(jnp.int32, acc0.shape, 1) < HEAD_DIM
    t0 = acc0 + pltpu.roll(acc0, HEAD_DIM, 1)
    t1 = acc1 + pltpu.roll(acc1, HEAD_DIM, 1)
    return jnp.where(left, t0, t1)


def _swa_valid_t(base):
    kpos = base - BLOCK + lax.broadcasted_iota(jnp.int32, (2 * BLOCK, BLOCK), 0)
    qpos = base + lax.broadcasted_iota(jnp.int32, (2 * BLOCK, BLOCK), 1)
    return jnp.logical_and(kpos >= 0, jnp.logical_and(kpos <= qpos, kpos > qpos - BLOCK))


def _swa_probs_t(qm, kk, valid_t, sink):
    s = jnp.where(valid_t, _dot_nt(kk, qm) * (HEAD_DIM ** -0.5), NEG)
    m = jnp.maximum(jnp.max(s, axis=0, keepdims=True), sink)
    e = jnp.exp(s - m)
    es = jnp.exp(sink - m)
    inv = 1.0 / (jnp.sum(e, axis=0, keepdims=True) + es)
    return e * inv, es * inv


def _swa_operands(kh_ref, kc_ref, vh_ref, vc_ref, nsb):
    out = []
    for h_ref, c_ref in ((kh_ref, kc_ref), (vh_ref, vc_ref)):
        dup = _dup_heads(jnp.concatenate([h_ref[...], c_ref[...]], axis=0).astype(F32))
        out.append([d.astype(BF16) for d in dup])
        out.append([[d[BLOCK * c:BLOCK * (c + 1), :].T.astype(BF16) for c in range(nsb + 1)] for d in dup])
    return out


def _halves(x):
    left = lax.broadcasted_iota(jnp.int32, x.shape, 1) < HEAD_DIM
    zero = jnp.zeros_like(x)
    return jnp.where(left, x, zero), jnp.where(left, zero, x)


def _swa_specs(T, bq, rev):
    n = T // bq
    blk = (lambda i: n - 1 - i) if rev else (lambda i: i)
    halo = lambda i: jnp.maximum(blk(i) * (bq // BLOCK) - 1, 0)
    return blk, [
        pl.BlockSpec(memory_space=pltpu.SMEM),
        pl.BlockSpec((bq, 512), lambda i: (blk(i), 0)),
        pl.BlockSpec((bq, 128), lambda i: (blk(i), 4)),
        pl.BlockSpec((BLOCK, 128), lambda i: (halo(i), 4)),
        pl.BlockSpec((bq, 128), lambda i: (blk(i), N_VB // 128)),
        pl.BlockSpec((BLOCK, 128), lambda i: (halo(i), N_VB // 128)),
        pl.BlockSpec((bq, 512), lambda i: (blk(i), N_GB // 512)),
    ]


def _swa_fwd(qkr, p, sinks):
    T = p.shape[0]
    bq = min(512, T)
    nsb = bq // BLOCK
    blk, specs = _swa_specs(T, bq, False)

    def body(sink_ref, q_ref, kc_ref, kh_ref, vc_ref, vh_ref, gt_ref, y_ref):
        base = blk(pl.program_id(0)) * bq
        kk, _, _, vT = _swa_operands(kh_ref, kc_ref, vh_ref, vc_ref, nsb)
        top = lax.broadcasted_iota(jnp.int32, (BLOCK, BLOCK), 0) < HEAD_DIM
        for sb in range(nsb):
            rows = slice(sb * BLOCK, (sb + 1) * BLOCK)
            keys = slice(sb * BLOCK, (sb + 2) * BLOCK)
            valid_t = _swa_valid_t(base + sb * BLOCK)
            for G in range(4):
                g = G // 2
                cols = slice(128 * G, 128 * (G + 1))
                vvt = jnp.concatenate([vT[g][sb], vT[g][sb + 1]], axis=1)
                o_t = []
                for hh, qm in enumerate(_halves(q_ref[rows, cols])):
                    pr, _ = _swa_probs_t(qm, kk[g][keys], valid_t, sink_ref[2 * G + hh])
                    o_t.append(_dot(vvt, pr.astype(BF16)))
                o = jnp.where(top, o_t[0], o_t[1]).T
                sg, _ = _silu_and_grad(gt_ref[rows, cols].astype(F32))
                y_ref[rows, cols] = (o * sg).astype(y_ref.dtype)

    return _call(
        body, name="swa_fwd", grid=(T // bq,), in_specs=specs,
        out_specs=pl.BlockSpec((bq, 512), lambda i: (i, 0)),
        out_shape=jax.ShapeDtypeStruct((T, 512), BF16),
    )(sinks, qkr, qkr, qkr, p, p, p)


def _swa_bwd(qkr, p, dy, sinks, cos_t, sin_t, dp):
    T = p.shape[0]
    bq = min(512, T)
    nsb = bq // BLOCK
    blk, specs = _swa_specs(T, bq, True)

    def body(sink_ref, q_ref, kc_ref, kh_ref, vc_ref, vh_ref, gt_ref, dy_ref, c_ref, s_ref, _,
             dp_ref, ds_ref, dk_acc, dv_acc, k_carry, v_carry):
        @pl.when(pl.program_id(0) == 0)
        def _():
            k_carry[...] = jnp.zeros_like(k_carry)
            v_carry[...] = jnp.zeros_like(v_carry)
            ds_ref[...] = jnp.zeros_like(ds_ref)

        base = blk(pl.program_id(0)) * bq
        dk_acc[...] = jnp.zeros_like(dk_acc)
        dv_acc[...] = jnp.zeros_like(dv_acc)
        kk, kT, vv, vT = _swa_operands(kh_ref, kc_ref, vh_ref, vc_ref, nsb)
        top = lax.broadcasted_iota(jnp.int32, (BLOCK, BLOCK), 0) < HEAD_DIM
        for sb in range(nsb):
            rows = slice(sb * BLOCK, (sb + 1) * BLOCK)
            keys = slice(sb * BLOCK, (sb + 2) * BLOCK)
            valid_t = _swa_valid_t(base + sb * BLOCK)
            dkp = [jnp.zeros((2 * BLOCK, BLOCK), F32)] * 2
            dvp = [jnp.zeros((2 * BLOCK, BLOCK), F32)] * 2
            for G in range(4):
                g = G // 2
                cols = slice(128 * G, 128 * (G + 1))
                kkt = jnp.concatenate([kT[g][sb], kT[g][sb + 1]], axis=1)
                vvt = jnp.concatenate([vT[g][sb], vT[g][sb + 1]], axis=1)
                qms = _halves(q_ref[rows, cols])
                prs, pss, o_t = [], [], []
                for hh in range(2):
                    pr, ps = _swa_probs_t(qms[hh], kk[g][keys], valid_t, sink_ref[2 * G + hh])
                    prs.append(pr)
                    pss.append(ps)
                    o_t.append(_dot(vvt, pr.astype(BF16)))
                o = jnp.where(top, o_t[0], o_t[1]).T
                sg, dsg = _silu_and_grad(gt_ref[rows, cols].astype(F32))
                dyf = dy_ref[rows, cols].astype(F32)
                do = dyf * sg
                dp_ref[rows, 1024 + 128 * G:1024 + 128 * (G + 1)] = (dyf * o * dsg).astype(dp_ref.dtype)
                do_t = do.T
                doms = _halves(do.astype(BF16))
                dq_t = []
                for hh in range(2):
                    dom_t = jnp.where(top if hh == 0 else jnp.logical_not(top), do_t, 0.0).astype(BF16)
                    dpv = _dot(vv[g][keys], dom_t)
                    delta = jnp.sum(prs[hh] * dpv, axis=0, keepdims=True)
                    dsb = (prs[hh] * (dpv - delta) * (HEAD_DIM ** -0.5)).astype(BF16)
                    h = 2 * G + hh
                    ds_ref[h:h + 1, :] += jnp.broadcast_to(
                        -jnp.sum(pss[hh] * delta, axis=1, keepdims=True), (1, 128))
                    dq_t.append(_dot(kkt, dsb))
                    dkp[g] = dkp[g] + _dot(dsb, qms[hh])
                    dvp[g] = dvp[g] + _dot(prs[hh].astype(BF16), doms[hh])
                dq = jnp.where(top, dq_t[0], dq_t[1]).T
                dp_ref[rows, cols] = _unrotate(dq, c_ref[rows, :], s_ref[rows, :]).astype(dp_ref.dtype)
            dk_acc[keys, :] += _fold_heads(dkp[0], dkp[1])
            dv_acc[keys, :] += _fold_heads(dvp[0], dvp[1])
        dk_acc[bq:bq + BLOCK, :] += k_carry[...]
        dv_acc[bq:bq + BLOCK, :] += v_carry[...]
        dk = _unrotate(dk_acc[BLOCK:bq + BLOCK, :], c_ref[...], s_ref[...])
        dp_ref[:, 512:640] = dk.astype(dp_ref.dtype)
        dp_ref[:, 640:768] = dv_acc[BLOCK:bq + BLOCK, :].astype(dp_ref.dtype)
        dp_ref[:, 768:1024] = jnp.zeros((bq, 256), dp_ref.dtype)
        k_carry[...] = dk_acc[0:BLOCK, :]
        v_carry[...] = dv_acc[0:BLOCK, :]

    rowblk = lambda w: pl.BlockSpec((bq, w), lambda i: (blk(i), 0))
    return _call(
        body, name="swa_bwd", grid=(T // bq,), in_specs=specs + [rowblk(512), rowblk(128), rowblk(128), _HBM],
        out_specs=[pl.BlockSpec((bq, 1536), lambda i: (blk(i), N_QB // 1536)), _full((SW_HEADS, 128))],
        out_shape=[jax.ShapeDtypeStruct(dp.shape, dp.dtype), jax.ShapeDtypeStruct((SW_HEADS, 128), F32)],
        scratch=[pltpu.VMEM((bq + BLOCK, 128), F32), pltpu.VMEM((bq + BLOCK, 128), F32),
                 pltpu.VMEM((BLOCK, 128), F32), pltpu.VMEM((BLOCK, 128), F32)],
        aliases={10: 0},
    )(sinks, qkr, qkr, qkr, p, p, p, dy, cos_t, sin_t, dp)


def _split_dot(x, tri):
    hi = x.astype(BF16)
    lo = (x - hi.astype(F32)).astype(BF16)
    return _dot(hi, tri) + _dot(lo, tri)


def _sb_logits(qs, kj):
    z = _dot_nt(qs, kj)
    e = jnp.exp(-jnp.abs(z))
    return z, e, -(jnp.maximum(z, 0.0) + jnp.log(1.0 + e))


def _sb_before(tq, qpos0, kpos0):
    qpos = qpos0 + lax.broadcasted_iota(jnp.int32, (tq, tq), 0)
    kpos = kpos0 + lax.broadcasted_iota(jnp.int32, (tq, tq), 1)
    return kpos < qpos


def _sb_tris(kb):
    row = lax.broadcasted_iota(jnp.int32, (kb, kb), 0)
    col = lax.broadcasted_iota(jnp.int32, (kb, kb), 1)
    return jnp.stack([jnp.where(row >= col, 1.0, 0.0), jnp.where(row > col, 1.0, 0.0)]).astype(BF16)


SB_TQ = 256
SB_NSUB = 2


def _sb_tiles(qi, w, tq):
    for t in range(SB_NSUB):
        jb = qi - SB_NSUB * w - t
        yield pl.ds(pl.multiple_of(jnp.maximum(jb, 0) * tq, tq), tq), jb, jb >= 0


def _sb_fwd(p, tris):
    T = p.shape[0]
    tq = min(SB_TQ, T)
    scale = SB_HEAD_DIM ** -0.5

    def body(q_ref, gt_ref, k_ref, v_ref, tri_ref, o_ref, y_ref, acc, r_ref):
        qi = pl.program_id(1)
        qs = (q_ref[...].astype(F32) * scale).astype(BF16)
        acc[...] = jnp.zeros_like(acc)
        r_ref[...] = jnp.zeros_like(r_ref)

        def step(carry):
            w, _ = carry
            r = r_ref[...]
            out = None
            for t, (rows, jb, live) in enumerate(_sb_tiles(qi, w, tq)):
                keep = _sb_before(tq, qi * tq, jb * tq) if t == 0 else live
                z, _, lf = _sb_logits(qs, k_ref[rows, :])
                lf = jnp.where(keep, lf, 0.0)
                a = jnp.where(keep, jnp.exp(z + _split_dot(lf, tri_ref[0]) + r), 0.0)
                term = _dot(a.astype(BF16), v_ref[rows, :])
                out = term if out is None else out + term
                r = r + jnp.sum(lf, axis=-1, keepdims=True)
            acc[...] += out
            r_ref[...] = r
            return w + 1, jnp.max(r) > SB_EXIT

        lax.while_loop(lambda c: jnp.logical_and(qi - SB_NSUB * c[0] >= 0, c[1]), step, (0, True))
        o = acc[...]
        o_ref[...] = o
        sg, _ = _silu_and_grad(gt_ref[...].astype(F32))
        y_ref[...] = (o * sg).astype(y_ref.dtype)

    col = lambda k: pl.BlockSpec((tq, 128), lambda h, i: (i, N_C // 128 + 4 * h + k))
    whole = lambda k: pl.BlockSpec((T, 128), lambda h, i: (0, N_C // 128 + 4 * h + k))
    out = pl.BlockSpec((tq, 128), lambda h, i: (i, h))
    return _call(
        body, name="sb_fwd", grid=(SB_HEADS, T // tq),
        in_specs=[col(0), col(1), whole(2), whole(3), _full((2, tq, tq))],
        out_specs=[out, out],
        out_shape=[jax.ShapeDtypeStruct((T, 512), F32), jax.ShapeDtypeStruct((T, 512), BF16)],
        scratch=[pltpu.VMEM((tq, 128), F32), pltpu.VMEM((tq, 1), F32)],
    )(p, p, p, p, tris)


def _sb_bwd(p, o, dy, tris, dp):
    T = p.shape[0]
    tq = min(SB_TQ, T)
    nq = T // tq
    scale = SB_HEAD_DIM ** -0.5

    def body(q_ref, gt_ref, k_ref, v_ref, o_ref, dy_ref, tri_ref, _, dqg_ref, dk_hbm, dv_hbm,
             dq_acc, r_ref, s_ref, dk_acc, dv_acc, sem):
        h, qi = pl.program_id(0), pl.program_id(1)

        @pl.when(qi == 0)
        def _():
            dk_acc[...] = jnp.zeros_like(dk_acc)
            dv_acc[...] = jnp.zeros_like(dv_acc)

        qs = (q_ref[...].astype(F32) * scale).astype(BF16)
        of = o_ref[...]
        dyf = dy_ref[...].astype(F32)
        sg, dsg = _silu_and_grad(gt_ref[...].astype(F32))
        do = dyf * sg
        dqg_ref[:, 128:256] = (dyf * of * dsg).astype(dqg_ref.dtype)
        delta = jnp.sum(do * of, axis=-1, keepdims=True)
        dob = do.astype(BF16)
        dq_acc[...] = jnp.zeros_like(dq_acc)
        r_ref[...] = jnp.zeros_like(r_ref)
        s_ref[...] = jnp.zeros_like(s_ref)

        def tile(rows, keep, r_in, s_in):
            kj, vj = k_ref[rows, :], v_ref[rows, :]
            z, e, lf = _sb_logits(qs, kj)
            lf = jnp.where(keep, lf, 0.0)
            a = jnp.where(keep, jnp.exp(z + _split_dot(lf, tri_ref[0]) + r_in), 0.0)
            gz = a * _dot_nt(dob, vj)
            later = _split_dot(gz, tri_ref[1]) + s_in
            sig = jnp.where(z >= 0.0, 1.0, e) / (1.0 + e)
            dz = jnp.where(keep, gz - sig * (delta - later), 0.0)
            dk_acc[rows, :] += _dot(dz.T.astype(BF16), qs)
            dv_acc[rows, :] += _dot(a.T.astype(BF16), dob)
            return (_dot(dz.astype(BF16), kj), jnp.sum(lf, axis=-1, keepdims=True),
                    jnp.sum(gz, axis=-1, keepdims=True))

        def step(carry):
            w, _ = carry
            r, sm, dq = r_ref[...], s_ref[...], None
            for t, (rows, jb, live) in enumerate(_sb_tiles(qi, w, tq)):
                dqt, lt, gt = tile(rows, _sb_before(tq, qi * tq, jb * tq) if t == 0 else live, r, sm)
                dq = dqt if dq is None else dq + dqt
                r, sm = r + lt, sm + gt
            dq_acc[...] += dq
            s_ref[...] = sm
            r_ref[...] = r
            return w + 1, jnp.max(r) > SB_EXIT

        lax.while_loop(lambda c: jnp.logical_and(qi - SB_NSUB * c[0] >= 0, c[1]), step, (0, True))
        dqg_ref[:, 0:128] = (dq_acc[...] * scale).astype(dqg_ref.dtype)

        @pl.when(qi == nq - 1)
        def _():
            cols = pl.ds(pl.multiple_of(h * 128, 128), 128)
            ck = pltpu.make_async_copy(dk_acc, dk_hbm.at[:, cols], sem.at[0])
            cv = pltpu.make_async_copy(dv_acc, dv_hbm.at[:, cols], sem.at[1])
            ck.start()
            cv.start()
            ck.wait()
            cv.wait()

    col = lambda k: pl.BlockSpec((tq, 128), lambda h, i: (i, N_C // 128 + 4 * h + k))
    whole = lambda k: pl.BlockSpec((T, 128), lambda h, i: (0, N_C // 128 + 4 * h + k))
    blk = pl.BlockSpec((tq, 128), lambda h, i: (i, h))
    return _call(
        body, name="sb_bwd", grid=(SB_HEADS, nq),
        in_specs=[col(0), col(1), whole(2), whole(3), blk, blk, _full((2, tq, tq)), _HBM],
        out_specs=[pl.BlockSpec((tq, 256), lambda h, i: (i, N_C // 256 + 2 * h)), _HBM, _HBM],
        out_shape=[jax.ShapeDtypeStruct(dp.shape, dp.dtype),
                   jax.ShapeDtypeStruct((T, 512), F32), jax.ShapeDtypeStruct((T, 512), F32)],
        scratch=[pltpu.VMEM((tq, 128), F32), pltpu.VMEM((tq, 1), F32), pltpu.VMEM((tq, 1), F32),
                 pltpu.VMEM((T, 128), F32), pltpu.VMEM((T, 128), F32), pltpu.SemaphoreType.DMA((2,))],
        aliases={7: 0},
    )(p, p, p, p, o, dy, tris, dp)


def _sb_kv_into_dp(dk, dv, dp):
    T = dk.shape[0]
    tm = min(1024, T)

    def body(dk_ref, dv_ref, _, o_ref):
        o_ref[:, 0:128] = dk_ref[...].astype(o_ref.dtype)
        o_ref[:, 128:256] = dv_ref[...].astype(o_ref.dtype)

    blk = pl.BlockSpec((tm, 128), lambda i, h: (i, h))
    return _call(
        body, name="sb_kv_into_dp", grid=(T // tm, SB_HEADS), in_specs=[blk, blk, _HBM],
        out_specs=pl.BlockSpec((tm, 256), lambda i, h: (i, N_C // 256 + 2 * h + 1)),
        out_shape=jax.ShapeDtypeStruct(dp.shape, dp.dtype), aliases={2: 0},
    )(dk, dv, dp)


def _post_fwd(ya, yb, yc, p, bm, wa, wb, wc, wo, g_post, x):
    T = x.shape[0]
    tm = min(512, T)

    def body(ya_ref, yb_ref, yc_ref, la_ref, lb_ref, lc_ref, bm_ref, wa_ref, wb_ref, wc_ref, wo_ref,
             g_ref, x_ref, m_ref, out_ref, xn_ref):
        merged = None
        for k, (y_ref, l_ref, w_ref) in enumerate(
                ((ya_ref, la_ref, wa_ref), (yb_ref, lb_ref, wb_ref), (yc_ref, lc_ref, wc_ref))):
            gate = _sigmoid(l_ref[...].astype(F32) + bm_ref[k:k + 1, :])
            term = gate * _dot(y_ref[...], w_ref[...])
            merged = term if merged is None else merged + term
        mb = merged.astype(BF16)
        m_ref[...] = mb
        out = _dot(mb, wo_ref[...])
        out_ref[...] = out
        r = lax.rsqrt(jnp.mean(out * out, axis=-1, keepdims=True) + EPS)
        xn_ref[...] = x_ref[...] + out * r * g_ref[...]

    yspec = pl.BlockSpec((tm, 512), lambda i: (i, 0))
    lspec = lambda k: pl.BlockSpec((tm, D), lambda i: (i, k))
    row = pl.BlockSpec((tm, D), lambda i: (i, 0))
    return _call(
        body, name="post_fwd", grid=(T // tm,),
        in_specs=[yspec, yspec, yspec, lspec(0), lspec(1), lspec(2), _full((3, D)),
                  _full((512, D)), _full((512, D)), _full((512, D)), _full((D, D)), _full((1, D)), row],
        out_specs=[row, row, row],
        out_shape=[jax.ShapeDtypeStruct((T, D), BF16), jax.ShapeDtypeStruct((T, D), F32),
                   jax.ShapeDtypeStruct((T, D), F32)],
    )(ya, yb, yc, p, p, p, bm, wa, wb, wc, wo, g_post, x)


def _post_bwd(dxn, out, ya, yb, yc, p, bm, wa, wb, wc, wo, g_post):
    T = dxn.shape[0]
    tm = min(256, T)

    def body(dxn_ref, out_ref, ya_ref, yb_ref, yc_ref, la_ref, lb_ref, lc_ref, bm_ref,
             wa_ref, wb_ref, wc_ref, wo_ref, g_ref,
             do_ref, dpa_ref, dpb_ref, dpc_ref, dl_ref, dya_ref, dyb_ref, dyc_ref, dg_ref, db_ref):
        @pl.when(pl.program_id(0) == 0)
        def _():
            dg_ref[...] = jnp.zeros_like(dg_ref)
            db_ref[...] = jnp.zeros_like(db_ref)

        out = out_ref[...]
        dxn_ = dxn_ref[...]
        r = lax.rsqrt(jnp.mean(out * out, axis=-1, keepdims=True) + EPS)
        a = dxn_ * g_ref[...]
        d_out = r * a - out * (r * r * r) * jnp.mean(a * out, axis=-1, keepdims=True)
        dg_ref[...] += jnp.sum(dxn_ * out * r, axis=0, keepdims=True)
        dob = d_out.astype(BF16)
        do_ref[...] = dob
        dmerged = _dot_nt(dob, wo_ref[...])
        for k, (y_ref, l_ref, w_ref, dp_ref, dy_ref) in enumerate((
                (ya_ref, la_ref, wa_ref, dpa_ref, dya_ref), (yb_ref, lb_ref, wb_ref, dpb_ref, dyb_ref),
                (yc_ref, lc_ref, wc_ref, dpc_ref, dyc_ref))):
            gate = _sigmoid(l_ref[...].astype(F32) + bm_ref[k:k + 1, :])
            proj = _dot(y_ref[...], w_ref[...])
            dproj = (dmerged * gate).astype(BF16)
            dp_ref[...] = dproj
            dlog = dmerged * proj * gate * (1.0 - gate)
            dl_ref[:, D * k:D * (k + 1)] = dlog.astype(dl_ref.dtype)
            db_ref[k:k + 1, :] += jnp.sum(dlog, axis=0, keepdims=True)
            dy_ref[...] = _dot_nt(dproj, w_ref[...]).astype(dy_ref.dtype)

    yspec = pl.BlockSpec((tm, 512), lambda i: (i, 0))
    lspec = lambda k: pl.BlockSpec((tm, D), lambda i: (i, k))
    row = pl.BlockSpec((tm, D), lambda i: (i, 0))
    sds = jax.ShapeDtypeStruct
    return _call(
        body, name="post_bwd", grid=(T // tm,),
        in_specs=[row, row, yspec, yspec, yspec, lspec(0), lspec(1), lspec(2), _full((3, D)),
                  _full((512, D)), _full((512, D)), _full((512, D)), _full((D, D)), _full((1, D))],
        out_specs=[row, row, row, row, pl.BlockSpec((tm, 3 * D), lambda i: (i, 0)), yspec, yspec, yspec,
                   _full((1, D)), _full((3, D))],
        out_shape=[sds((T, D), BF16), sds((T, D), BF16), sds((T, D), BF16), sds((T, D), BF16),
                   sds((T, IN_PAD), ACT), sds((T, 512), ACT), sds((T, 512), ACT), sds((T, 512), ACT),
                   sds((1, D), F32), sds((3, D), F32)],
    )(dxn, out, ya, yb, yc, p, p, p, bm, wa, wb, wc, wo, g_post)


def _loss_and_grad(y, target):
    T = y.shape[0]
    tm = min(1024, T)

    def body(y_ref, t_ref, l_ref, dy_ref):
        @pl.when(pl.program_id(0) == 0)
        def _():
            l_ref[...] = jnp.zeros_like(l_ref)
        e = y_ref[...] - t_ref[...]
        l_ref[...] += jnp.sum(e * e, axis=0, keepdims=True)
        dy_ref[...] = e * (1.0 / D)

    row = pl.BlockSpec((tm, D), lambda i: (i, 0))
    return _call(
        body, name="loss", grid=(T // tm,), in_specs=[row, row], out_specs=[_full((1, D)), row],
        out_shape=[jax.ShapeDtypeStruct((1, D), F32), jax.ShapeDtypeStruct((T, D), F32)],
    )(y, target)


def _sibling_swap(srcs, name):
    n = len(srcs)

    def body(*refs):
        s_refs, r_refs = refs[:n], refs[n:2 * n]
        send_sems, recv_sems = refs[2 * n:]
        x, y, c = lax.axis_index("x"), lax.axis_index("y"), lax.axis_index("c")
        copies = [pltpu.make_async_remote_copy(
            src_ref=s_refs[i], dst_ref=r_refs[i], send_sem=send_sems.at[i], recv_sem=recv_sems.at[i],
            device_id=(x, y, 1 - c), device_id_type=MESH) for i in range(n)]
        for cp in copies:
            cp.start()
        for cp in copies:
            cp.wait()

    return pl.pallas_call(
        body, name=name, in_specs=[_HBM] * n, out_specs=[_HBM] * n,
        out_shape=[jax.ShapeDtypeStruct(a.shape, a.dtype) for a in srcs],
        scratch_shapes=[pltpu.SemaphoreType.DMA((n,)), pltpu.SemaphoreType.DMA((n,))],
    )(*srcs)


_ELEMENTWISE_BLOCK_BYTES = 1 << 20


def _row_tile(rows, cols):
    if rows * cols * 4 <= 2 * _ELEMENTWISE_BLOCK_BYTES:
        return rows
    for tr in (2048, 1024, 512, 256, 128, 64, 32, 16, 8):
        if rows % tr == 0 and tr * cols * 4 <= _ELEMENTWISE_BLOCK_BYTES:
            return tr
    raise ValueError(f"no row tile for {(rows, cols)}")


def _sum_chips(r, name):
    _, R, C = r.shape
    tr = _row_tile(R, C)

    def body(r_ref, o_ref):
        f = lambda j: r_ref[j].astype(F32)
        o_ref[...] = ((f(0) + f(1)) + f(2)) + f(3)

    return _call(
        body, name=name, grid=(R // tr,),
        in_specs=[pl.BlockSpec((4, tr, C), lambda i: (0, i, 0))],
        out_specs=pl.BlockSpec((tr, C), lambda i: (i, 0)),
        out_shape=jax.ShapeDtypeStruct((R, C), F32),
    )(r)


def _adamw(w, m, v, g_mine, g_other, name):
    R, C = w.shape
    tr = _row_tile(R, C)

    def body(w_ref, m_ref, v_ref, a_ref, b_ref, g_ref, d_ref, nm_ref, nv_ref):
        g = a_ref[...] + b_ref[...]
        g_ref[...] = g
        m_new = ADAM_B1 * m_ref[...] + (1.0 - ADAM_B1) * g
        v_new = ADAM_B2 * v_ref[...] + (1.0 - ADAM_B2) * (g * g)
        nm_ref[...] = m_new
        nv_ref[...] = v_new
        m_hat = m_new / (1.0 - ADAM_B1 ** ADAM_STEP)
        v_hat = v_new / (1.0 - ADAM_B2 ** ADAM_STEP)
        d_ref[...] = -ADAM_LR * (m_hat / (jnp.sqrt(v_hat) + ADAM_EPS) + ADAM_WD * w_ref[...])

    blk = pl.BlockSpec((tr, C), lambda i: (i, 0))
    sds = jax.ShapeDtypeStruct((R, C), F32)
    return _call(body, name=name, grid=(R // tr,), in_specs=[blk] * 5, out_specs=[blk] * 4,
                 out_shape=[sds] * 4)(w, m, v, g_mine, g_other)


_NAMES = ('w_in', 'gm_w_s', 'gm_b_s', 'gm_norm_gain', 'sw_sinks', 'w_branch_a', 'w_branch_b',
          'w_branch_c', 'b_merge', 'w_out', 'g_pre', 'g_post')
_SMALL = ('gm_w_s', 'gm_b_s', 'gm_norm_gain', 'sw_sinks', 'g_pre', 'g_post')
_PACK_COLS = 1024


def _pack(arrays):
    flat = jnp.concatenate([a.reshape(-1).astype(F32) for a in arrays])
    rows = -(-flat.shape[0] // (8 * _PACK_COLS)) * 8
    return jnp.pad(flat, (0, rows * _PACK_COLS - flat.shape[0])).reshape(rows, _PACK_COLS)


def _unpack(buf, shapes):
    flat = buf.reshape(-1)
    out, off = [], 0
    for s in shapes:
        n = int(np.prod(s))
        out.append(flat[off:off + n].reshape(s))
        off += n
    return out


def _permute_cols(w):
    return jnp.concatenate(
        [jnp.zeros(w.shape[:-1] + (n,), w.dtype) if o is None else w[..., o:o + n] for o, n in _PERM], axis=-1)


def _unpermute_cols(w):
    new_off, off = {}, 0
    for o, n in _PERM:
        if o is not None:
            new_off[o] = (off, n)
        off += n
    return jnp.concatenate([w[..., new_off[o][0]:new_off[o][0] + new_off[o][1]] for o in sorted(new_off)], axis=-1)


def _tables(T):
    half = HEAD_DIM // 2
    freqs = ROPE_THETA ** (-jnp.arange(half, dtype=F32) / half)
    ang = jnp.arange(T).astype(F32)[:, None] * freqs[None, :]
    cos, sin = jnp.cos(ang), jnp.sin(ang)
    return (jnp.tile(jnp.concatenate([cos, cos], axis=1), (1, 2)),
            jnp.tile(jnp.concatenate([-sin, sin], axis=1), (1, 2)), _sb_tris(min(SB_TQ, T)))


def _cat_chips(got, axis):
    return jnp.concatenate([got[j] for j in range(4)], axis=axis)


def _layer_fwd(x, lw, tables, carry):
    gpre, gpost = lw['g_pre'][None, :], lw['g_post'][None, :]
    bfull = jnp.broadcast_to(lw['bs'][:, :, None], (GM_GROUPS, BLOCK, BLOCK))
    gv = lw['gv'][None, :]
    p, h, *got = _in_proj(x, gpre, lw['w_in'], carry)
    wabc = _cat_chips(got[0], 2)
    lw = dict(lw, wa=wabc[0], wb=wabc[1], wc=wabc[2], bm=_cat_chips(got[1], 1), wo=_cat_chips(got[2], 0))
    got = _permute_cols(_cat_chips(got[3], 1)) if len(got) > 3 else None
    ya = _gmlp_fwd(p, lw['ws'], bfull, gv)
    qkr = _rope_fwd(p, *tables[:2])
    yb = _swa_fwd(qkr, p, lw['sinks'])
    oc, yc = _sb_fwd(p, tables[2])
    merged, out, xn = _post_fwd(ya, yb, yc, p, lw['bm'], lw['wa'], lw['wb'], lw['wc'], lw['wo'], gpost, x)
    return xn, (lw, p, h, ya, qkr, yb, oc, yc, merged, out, bfull, gv, gpre, gpost), got


def _grad_partials(g):
    per_chip = lambda a, axis: jnp.stack(jnp.split(a, 4, axis=axis))
    g_abc = jnp.stack([g['w_branch_a'], g['w_branch_b'], g['w_branch_c']])
    return [per_chip(_unpermute_cols(g['w_in']), 1).astype(BF16),
            per_chip(g_abc, 2).astype(BF16),
            per_chip(g['b_merge'], 1),
            per_chip(g['w_out'], 0).astype(BF16)]


def _layer_bwd(x, saved, dxn, tables):
    lw, p, h, ya, qkr, yb, oc, yc, merged, out, bfull, gv, gpre, gpost = saved
    g = {}
    (d_out, dpa, dpb, dpc, dp, dya, dyb, dyc, dgpost, dbm) = _post_bwd(
        dxn, out, ya, yb, yc, p, lw['bm'], lw['wa'], lw['wb'], lw['wc'], lw['wo'], gpost)
    g['w_out'] = _matmul_tn(merged, d_out, "grad_w_out")
    g['w_branch_a'] = _matmul_tn(ya, dpa, "grad_w_a")
    g['w_branch_b'] = _matmul_tn(yb, dpb, "grad_w_b")
    g['w_branch_c'] = _matmul_tn(yc, dpc, "grad_w_c")
    g['g_post'] = dgpost[0]
    g['b_merge'] = dbm
    dp, dws, dbacc, dgv = _gmlp_bwd(p, dya, lw['ws'], bfull, gv, dp)
    g['gm_w_s'] = dws
    g['gm_b_s'] = jnp.sum(dbacc.reshape(BLOCK, GM_GROUPS, BLOCK), axis=2).T
    g['gm_norm_gain'] = dgv[0]
    dp, dsink = _swa_bwd(qkr, p, dyb, lw['sinks'], *tables[:2], dp)
    g['sw_sinks'] = dsink[:, 0]
    dp, dkc, dvc = _sb_bwd(p, oc, dyc, tables[2], dp)
    dp = _sb_kv_into_dp(dkc, dvc, dp)
    g['w_in'] = _matmul_tn(h, dp, "grad_w_in")
    dx, dgpre, *got = _in_bwd(dp, lw['w_in'], x, dxn, gpre, _grad_partials(g))
    g['g_pre'] = dgpre[0]
    return dx, g, got


def kernel(x, w_in, gm_w_s, gm_b_s, gm_norm_gain, sw_sinks, w_branch_a, w_branch_b, w_branch_c, b_merge, w_out, g_pre, g_post, loss_target, m_w_in, m_gm_w_s, m_gm_b_s, m_gm_norm_gain, m_sw_sinks, m_w_branch_a, m_w_branch_b, m_w_branch_c, m_b_merge, m_w_out, m_g_pre, m_g_post, v_w_in, v_gm_w_s, v_gm_b_s, v_gm_norm_gain, v_sw_sinks, v_w_branch_a, v_w_branch_b, v_w_branch_c, v_b_merge, v_w_out, v_g_pre, v_g_post):
    T = x.shape[1]
    weights = dict(zip(_NAMES, (w_in, gm_w_s, gm_b_s, gm_norm_gain, sw_sinks, w_branch_a, w_branch_b,
                                w_branch_c, b_merge, w_out, g_pre, g_post)))
    mom_m = dict(zip(_NAMES, (m_w_in, m_gm_w_s, m_gm_b_s, m_gm_norm_gain, m_sw_sinks, m_w_branch_a,
                              m_w_branch_b, m_w_branch_c, m_b_merge, m_w_out, m_g_pre, m_g_post)))
    mom_v = dict(zip(_NAMES, (v_w_in, v_gm_w_s, v_gm_b_s, v_gm_norm_gain, v_sw_sinks, v_w_branch_a,
                              v_w_branch_b, v_w_branch_c, v_b_merge, v_w_out, v_g_pre, v_g_post)))
    abc = lambda d: jnp.stack([d['w_branch_a'], d['w_branch_b'], d['w_branch_c']], axis=1)

    w_in_b, rest = w_in.astype(BF16), [abc(weights).astype(BF16), b_merge, w_out.astype(BF16)]
    tables = _tables(T)
    xs = [x[0]]
    saved = []
    got, = _chip_exchange([w_in_b[0]], (False,), "gather_w_in")
    w_in_l = _permute_cols(_cat_chips(got, 1))
    for l in range(DEPTH):
        lw = dict(w_in=w_in_l, ws=gm_w_s[l], bs=gm_b_s[l], gv=gm_norm_gain[l], sinks=sw_sinks[l],
                  g_pre=g_pre[l], g_post=g_post[l])
        carry = [a[l] for a in rest] + ([w_in_b[l + 1]] if l + 1 < DEPTH else [])
        xn, sv, w_in_l = _layer_fwd(xs[l], lw, tables, carry)
        xs.append(xn)
        saved.append(sv)

    lsum, dxn = _loss_and_grad(xs[DEPTH], loss_target[0])
    loss = lax.psum(0.5 * jnp.sum(lsum) / D, ("x", "y", "c"))

    small = {n: [None] * DEPTH for n in _SMALL}
    received = [None] * DEPTH
    for l in reversed(range(DEPTH)):
        dxn, g, received[l] = _layer_bwd(xs[l], saved[l], dxn, tables)
        for n in _SMALL:
            small[n][l] = g[n]
    grad_x = dxn[None]
    g_small = _pack([jnp.stack(small[n]) for n in _SMALL])
    got_small, = _chip_exchange([g_small], (False,), "gather_small_grads")

    views = [(4 * D, IN_WIDTH // 4), (4 * 3 * 512, D // 4), (4 * 3, D // 4), (D, D), g_small.shape]
    stacks = [jnp.stack([received[l][i] for l in range(DEPTH)], axis=1) for i in range(4)] + [got_small]
    sums = [_sum_chips(r.reshape((4,) + v), f"sum_chips_{i}") for i, (r, v) in enumerate(zip(stacks, views))]
    others = _sibling_swap(sums, "swap_core_sums")
    tensors = [lambda d: d['w_in'], abc, lambda d: d['b_merge'], lambda d: d['w_out'],
               lambda d: _pack([d[n] for n in _SMALL])]
    res = [_adamw(t(weights).reshape(v), t(mom_m).reshape(v), t(mom_v).reshape(v), s_, o_, f"adamw_{i}")
           for i, (t, v, s_, o_) in enumerate(zip(tensors, views, sums, others))]

    outs = []
    for kind in range(4):
        r = [res[i][kind] for i in range(5)]
        r_abc = r[1].reshape(4, 3, 512, D // 4)
        tiny = dict(zip(_SMALL, _unpack(r[4], [weights[n].shape for n in _SMALL])))
        big = dict(w_in=r[0].reshape(w_in.shape), w_branch_a=r_abc[:, 0], w_branch_b=r_abc[:, 1],
                   w_branch_c=r_abc[:, 2], b_merge=r[2].reshape(b_merge.shape), w_out=r[3].reshape(w_out.shape))
        outs.extend(big[n] if n in big else tiny[n] for n in _NAMES)
    return (loss, grad_x, *outs)
```

```python
import functools

import numpy as np
import jax
import jax.numpy as jnp
from jax import lax
from jax.experimental import pallas as pl
from jax.experimental.pallas import tpu as pltpu

F32 = jnp.float32
BF16 = jnp.bfloat16
ACT = jnp.bfloat16

D = 1024
DEPTH = 4
BLOCK = 128
EPS = 1e-6
NEG = -1e30
GM_GROUPS = 4
GM_WIDTH = 512
HEAD_DIM = 64
SW_HEADS = 8
SB_HEADS = 4
SB_HEAD_DIM = 128
ROPE_THETA = 10000.0
IN_WIDTH = 7936
IN_PAD = 8192

O_UA, O_VA, O_GA, O_QB, O_KB, O_VB, O_GB = 0, 512, 1024, 1536, 2048, 2176, 2304
O_QC, O_KC, O_VC, O_GC, O_MG = 2816, 3328, 3840, 4352, 4864
_PERM = ((O_MG, 3072), (O_UA, 512), (O_VA, 512), (O_GA, 512),
         (O_QB, 512), (O_KB, 128), (O_VB, 128), (None, IN_PAD - IN_WIDTH), (O_GB, 512)) + tuple(
    (o + 128 * h, 128) for h in range(4) for o in (O_QC, O_GC, O_KC, O_VC))
N_MG, N_UA, N_VA, N_GA = 0, 3072, 3584, 4096
N_QB, N_KB, N_VB, N_GB = 4608, 5120, 5248, 5632
N_C = 6144

ADAM_LR, ADAM_B1, ADAM_B2, ADAM_EPS, ADAM_WD, ADAM_STEP = 0.001, 0.9, 0.999, 1e-08, 0.01, 10

SB_EXIT = -104.0
V7X_VMEM_LIMIT = 56 * 1024 * 1024
MESH = pl.DeviceIdType.MESH


_HBM = pl.BlockSpec(memory_space=pl.ANY)


def _call(body, *, name, grid, in_specs, out_specs, out_shape, scratch=(), aliases=None):
    return pl.pallas_call(
        body, name=name, grid=grid, in_specs=in_specs, out_specs=out_specs, out_shape=out_shape,
        scratch_shapes=list(scratch), input_output_aliases=aliases or {},
        compiler_params=pltpu.CompilerParams(
            dimension_semantics=("arbitrary",) * len(grid), vmem_limit_bytes=V7X_VMEM_LIMIT))


def _sigmoid(x):
    return 1.0 / (1.0 + jnp.exp(-x))


def _silu_and_grad(x):
    s = _sigmoid(x)
    return x * s, s * (1.0 + x * (1.0 - s))


def _dot(a, b):
    return jnp.dot(a, b, preferred_element_type=F32)


def _dot_nt(a, b):
    return lax.dot_general(a, b, (((1,), (1,)), ((), ())), preferred_element_type=F32)


def _dot_tn(a, b):
    return lax.dot_general(a, b, (((0,), (0,)), ((), ())), preferred_element_type=F32)


def _full(shape):
    return pl.BlockSpec(shape, lambda *_: (0,) * len(shape))


_CHIP_STEPS = ((1, 0), (0, 1), (1, 1))


def _exchange_ops(s_refs, r_refs, send_sems, recv_sems, local_sems, per_target):
    n = len(s_refs)

    def copies():
        x, y, c = lax.axis_index("x"), lax.axis_index("y"), lax.axis_index("c")
        me = 2 * x + y
        pick = lambda i, j: s_refs[i].at[j] if per_target[i] else s_refs[i]
        own = [pltpu.make_async_copy(pick(i, me), r_refs[i].at[me], local_sems.at[i]) for i in range(n)]
        sent, arriving = [], []
        for k, (dx, dy) in enumerate(_CHIP_STEPS):
            tx, ty = (x + dx) % 2, (y + dy) % 2
            peer = 2 * tx + ty
            for i in range(n):
                sems = dict(send_sem=send_sems.at[3 * i + k], recv_sem=recv_sems.at[3 * i + k])
                sent.append(pltpu.make_async_remote_copy(
                    src_ref=pick(i, peer), dst_ref=r_refs[i].at[me], device_id=(tx, ty, c),
                    device_id_type=MESH, **sems))
                arriving.append(pltpu.make_async_remote_copy(
                    src_ref=pick(i, peer), dst_ref=r_refs[i].at[peer], device_id=(x, y, c),
                    device_id_type=MESH, **sems))
        return own, sent, arriving

    def start():
        own, sent, _ = copies()
        for cp in own + sent:
            cp.start()

    def wait():
        own, sent, arriving = copies()
        for cp in arriving:
            cp.wait_recv()
        for cp in sent:
            cp.wait_send()
        for cp in own:
            cp.wait()

    return start, wait


def _exchange_shapes(srcs, per_target):
    return [jax.ShapeDtypeStruct((4,) + (a.shape[1:] if pt else a.shape), a.dtype)
            for a, pt in zip(srcs, per_target)]


def _exchange_sems(n):
    return [pltpu.SemaphoreType.DMA((3 * n,)), pltpu.SemaphoreType.DMA((3 * n,)), pltpu.SemaphoreType.DMA((n,))]


def _chip_exchange(srcs, per_target, name):
    n = len(srcs)

    def body(*refs):
        start, wait = _exchange_ops(refs[:n], refs[n:2 * n], *refs[2 * n:], per_target)
        start()
        wait()

    return pl.pallas_call(
        body, name=name, in_specs=[_HBM] * n, out_specs=[_HBM] * n, out_shape=_exchange_shapes(srcs, per_target),
        scratch_shapes=_exchange_sems(n),
    )(*srcs)


def _in_proj(x, g_pre, w_in, carry=()):
    T = x.shape[0]
    tm, tn = min(2048, T), 1024
    ni, nj, n = T // tm, IN_PAD // tn, len(carry)

    def body(x_ref, g_ref, w_ref, *rest):
        p_ref, h_ref = rest[n], rest[n + 1]
        i, j = pl.program_id(0), pl.program_id(1)
        if n:
            start, wait = _exchange_ops(rest[:n], rest[n + 2:2 * n + 2], *rest[2 * n + 2:], (False,) * n)
            pl.when((i == 0) & (j == 0))(start)

        @pl.when(j == 0)
        def _():
            xf = x_ref[...]
            r = lax.rsqrt(jnp.mean(xf * xf, axis=-1, keepdims=True) + EPS)
            h_ref[...] = (xf * r * g_ref[...]).astype(BF16)
        p_ref[...] = _dot(h_ref[...], w_ref[...]).astype(p_ref.dtype)
        if n:
            pl.when((i == ni - 1) & (j == nj - 1))(wait)

    return _call(
        body, name="in_proj_gather" if n else "in_proj", grid=(ni, nj),
        in_specs=[pl.BlockSpec((tm, D), lambda i, j: (i, 0)), _full((1, D)),
                  pl.BlockSpec((D, tn), lambda i, j: (0, j))] + [_HBM] * n,
        out_specs=[pl.BlockSpec((tm, tn), lambda i, j: (i, j)), pl.BlockSpec((tm, D), lambda i, j: (i, 0))]
        + [_HBM] * n,
        out_shape=[jax.ShapeDtypeStruct((T, IN_PAD), ACT), jax.ShapeDtypeStruct((T, D), BF16)]
        + _exchange_shapes(carry, (False,) * n),
        scratch=_exchange_sems(n) if n else (),
    )(x, g_pre, w_in, *carry)


def _in_bwd(dp, w_in, x, dxn, g_pre, carry=()):
    T = x.shape[0]
    tm, tk = min(1024, T), 2048
    ni, nk, n = T // tm, IN_PAD // tk, len(carry)

    def body(dp_ref, w_ref, x_ref, dxn_ref, g_ref, *rest):
        dx_ref, dg_ref, acc = rest[n], rest[n + 1], rest[2 * n + 2]
        i, k = pl.program_id(0), pl.program_id(1)
        if n:
            start, wait = _exchange_ops(rest[:n], rest[n + 2:2 * n + 2], *rest[2 * n + 3:], (True,) * n)
            pl.when((i == 0) & (k == 0))(start)

        @pl.when(k == 0)
        def _():
            acc[...] = jnp.zeros_like(acc)

        @pl.when((i == 0) & (k == 0))
        def _():
            dg_ref[...] = jnp.zeros_like(dg_ref)

        acc[...] += _dot_nt(dp_ref[...], w_ref[...])

        @pl.when(k == nk - 1)
        def _():
            dh = acc[...]
            xf = x_ref[...]
            r = lax.rsqrt(jnp.mean(xf * xf, axis=-1, keepdims=True) + EPS)
            a = dh * g_ref[...]
            dx_ref[...] = dxn_ref[...] + r * a - xf * (r * r * r) * jnp.mean(a * xf, axis=-1, keepdims=True)
            dg_ref[...] += jnp.sum(dh * xf * r, axis=0, keepdims=True)

        if n:
            pl.when((i == ni - 1) & (k == nk - 1))(wait)

    return _call(
        body, name="in_bwd_scatter" if n else "in_bwd", grid=(ni, nk),
        in_specs=[pl.BlockSpec((tm, tk), lambda i, k: (i, k)), pl.BlockSpec((D, tk), lambda i, k: (0, k)),
                  pl.BlockSpec((tm, D), lambda i, k: (i, 0)), pl.BlockSpec((tm, D), lambda i, k: (i, 0)),
                  _full((1, D))] + [_HBM] * n,
        out_specs=[pl.BlockSpec((tm, D), lambda i, k: (i, 0)), _full((1, D))] + [_HBM] * n,
        out_shape=[jax.ShapeDtypeStruct((T, D), F32), jax.ShapeDtypeStruct((1, D), F32)]
        + _exchange_shapes(carry, (True,) * n),
        scratch=[pltpu.VMEM((tm, D), F32)] + (_exchange_sems(n) if n else []),
    )(dp, w_in, x, dxn, g_pre, *carry)


def _matmul_tn(a, b, name):
    T, K = a.shape
    N = b.shape[1]
    tk, tn, tt = min(K, 1024), min(N, 2048), min(T, 1024)

    def body(a_ref, b_ref, o_ref):
        @pl.when(pl.program_id(2) == 0)
        def _():
            o_ref[...] = jnp.zeros_like(o_ref)
        o_ref[...] += _dot_tn(a_ref[...], b_ref[...])

    return _call(
        body, name=name, grid=(K // tk, N // tn, T // tt),
        in_specs=[pl.BlockSpec((tt, tk), lambda i, j, t: (t, i)), pl.BlockSpec((tt, tn), lambda i, j, t: (t, j))],
        out_specs=pl.BlockSpec((tk, tn), lambda i, j, t: (i, j)),
        out_shape=jax.ShapeDtypeStruct((K, N), F32),
    )(a, b)


def _gm_forward_parts(v_ref, gv_ref, ws_ref, bf_ref, nch):
    vf = v_ref[...].astype(F32)
    mu = jnp.mean(vf, axis=-1, keepdims=True)
    xc = vf - mu
    rstd = lax.rsqrt(jnp.mean(xc * xc, axis=-1, keepdims=True) + EPS)
    xhat = xc * rstd
    vnb = (xhat * gv_ref[...]).astype(BF16)
    row = lax.broadcasted_iota(jnp.int32, (BLOCK, BLOCK), 0)
    col = lax.broadcasted_iota(jnp.int32, (BLOCK, BLOCK), 1)
    mixed, vcats, wgs = [], [], []
    for g in range(GM_GROUPS):
        vg = vnb[:, BLOCK * g:BLOCK * (g + 1)]
        vcat = jnp.concatenate([vg[BLOCK * k:BLOCK * (k + 1), :] for k in range(nch)], axis=1)
        wg = jnp.where(row >= col, ws_ref[g], 0.0)
        m = _dot(wg.astype(BF16), vcat)
        mixed.append(jnp.concatenate(
            [m[:, BLOCK * k:BLOCK * (k + 1)] + bf_ref[g] for k in range(nch)], axis=0))
        vcats.append(vcat)
        wgs.append(wg)
    return xhat, rstd, jnp.concatenate(mixed, axis=1), vcats, wgs, row >= col


def _gmlp_fwd(p, ws, bfull, gv):
    T = p.shape[0]
    tm = min(512, T)
    nch = tm // BLOCK

    def body(u_ref, v_ref, gt_ref, ws_ref, bf_ref, gv_ref, y_ref):
        _, _, mixed, _, _, _ = _gm_forward_parts(v_ref, gv_ref, ws_ref, bf_ref, nch)
        sg, _ = _silu_and_grad(gt_ref[...].astype(F32))
        y_ref[...] = (u_ref[...].astype(F32) * mixed * sg).astype(y_ref.dtype)

    seg = lambda off: pl.BlockSpec((tm, 512), lambda i: (i, off // 512))
    return _call(
        body, name="gmlp_fwd", grid=(T // tm,),
        in_specs=[seg(N_UA), seg(N_VA), seg(N_GA), _full((GM_GROUPS, BLOCK, BLOCK)),
                  _full((GM_GROUPS, BLOCK, BLOCK)), _full((1, GM_WIDTH))],
        out_specs=pl.BlockSpec((tm, 512), lambda i: (i, 0)),
        out_shape=jax.ShapeDtypeStruct((T, GM_WIDTH), BF16),
    )(p, p, p, ws, bfull, gv)


def _gmlp_bwd(p, dy, ws, bfull, gv, dp):
    T = p.shape[0]
    tm = min(512, T)
    nch = tm // BLOCK

    def body(u_ref, v_ref, gt_ref, dy_ref, ws_ref, bf_ref, gv_ref, _, dp_ref, dws_ref, db_ref, dgv_ref):
        @pl.when(pl.program_id(0) == 0)
        def _():
            dws_ref[...] = jnp.zeros_like(dws_ref)
            db_ref[...] = jnp.zeros_like(db_ref)
            dgv_ref[...] = jnp.zeros_like(dgv_ref)

        xhat, rstd, mixed, vcats, wgs, tril = _gm_forward_parts(v_ref, gv_ref, ws_ref, bf_ref, nch)
        u = u_ref[...].astype(F32)
        gt = gt_ref[...].astype(F32)
        dyf = dy_ref[...].astype(F32)
        sg, dsg = _silu_and_grad(gt)
        du = dyf * mixed * sg
        dmixed = dyf * u * sg
        dgate = dyf * (u * mixed) * dsg
        dvn = []
        for g in range(GM_GROUPS):
            dmg = dmixed[:, BLOCK * g:BLOCK * (g + 1)]
            chunks = [dmg[BLOCK * k:BLOCK * (k + 1), :] for k in range(nch)]
            dmcat = jnp.concatenate(chunks, axis=1).astype(BF16)
            dws_ref[g] += jnp.where(tril, _dot_nt(dmcat, vcats[g]), 0.0)
            dvcat = _dot(wgs[g].T.astype(BF16), dmcat)
            dvn.append(jnp.concatenate([dvcat[:, BLOCK * k:BLOCK * (k + 1)] for k in range(nch)], axis=0))
            db_ref[:, BLOCK * g:BLOCK * (g + 1)] += functools.reduce(lambda a, b: a + b, chunks)
        dvn = jnp.concatenate(dvn, axis=1)
        dgv_ref[...] += jnp.sum(dvn * xhat, axis=0, keepdims=True)
        dxh = dvn * gv_ref[...]
        dv = rstd * (dxh - jnp.mean(dxh, axis=-1, keepdims=True)
                     - xhat * jnp.mean(dxh * xhat, axis=-1, keepdims=True))
        dp_ref[:, 0:512] = du.astype(dp_ref.dtype)
        dp_ref[:, 512:1024] = dv.astype(dp_ref.dtype)
        dp_ref[:, 1024:1536] = dgate.astype(dp_ref.dtype)

    seg = lambda off: pl.BlockSpec((tm, 512), lambda i: (i, off // 512))
    return _call(
        body, name="gmlp_bwd", grid=(T // tm,),
        in_specs=[seg(N_UA), seg(N_VA), seg(N_GA), pl.BlockSpec((tm, 512), lambda i: (i, 0)),
                  _full((GM_GROUPS, BLOCK, BLOCK)), _full((GM_GROUPS, BLOCK, BLOCK)), _full((1, GM_WIDTH)), _HBM],
        out_specs=[pl.BlockSpec((tm, 1536), lambda i: (i, N_UA // 1536)), _full((GM_GROUPS, BLOCK, BLOCK)),
                   _full((BLOCK, GM_WIDTH)), _full((1, GM_WIDTH))],
        out_shape=[jax.ShapeDtypeStruct(dp.shape, dp.dtype), jax.ShapeDtypeStruct((GM_GROUPS, BLOCK, BLOCK), F32),
                   jax.ShapeDtypeStruct((BLOCK, GM_WIDTH), F32), jax.ShapeDtypeStruct((1, GM_WIDTH), F32)],
        aliases={7: 0},
    )(p, p, p, dy, ws, bfull, gv, dp)


def _swap_halves(x):
    lane = lax.broadcasted_iota(jnp.int32, x.shape, 1) % HEAD_DIM
    return jnp.where(lane < HEAD_DIM // 2, pltpu.roll(x, 96, 1), pltpu.roll(x, 32, 1))


def _rope_fwd(p, cos_t, sin_t):
    T = p.shape[0]
    tm = min(512, T)

    def body(q_ref, k_ref, c_ref, s_ref, o_ref):
        c, s = c_ref[...], s_ref[...]
        for G in range(5):
            xg = (q_ref[:, 128 * G:128 * (G + 1)] if G < 4 else k_ref[...]).astype(F32)
            o_ref[:, 128 * G:128 * (G + 1)] = (xg * c + _swap_halves(xg) * s).astype(o_ref.dtype)

    return _call(
        body, name="rope_fwd", grid=(T // tm,),
        in_specs=[pl.BlockSpec((tm, 512), lambda i: (i, N_QB // 512)),
                  pl.BlockSpec((tm, 128), lambda i: (i, N_KB // 128)),
                  pl.BlockSpec((tm, 128), lambda i: (i, 0)), pl.BlockSpec((tm, 128), lambda i: (i, 0))],
        out_specs=pl.BlockSpec((tm, 640), lambda i: (i, 0)),
        out_shape=jax.ShapeDtypeStruct((T, 640), BF16),
    )(p, p, cos_t, sin_t)


def _unrotate(d, c, s):
    return d * c + _swap_halves(d * s)


def _dup_heads(x):
    left = lax.broadcasted_iota(jnp.int32, x.shape, 1) < HEAD_DIM
    r = pltpu.roll(x, HEAD_DIM, 1)
    return jnp.where(left, x, r), jnp.where(left, r, x)


def _fold_heads(acc0, acc1):
    left = lax.broadcasted_iota(jnp.int32, acc0.shape, 1) < HEAD_DIM
    t0 = acc0 + pltpu.roll(acc0, HEAD_DIM, 1)
    t1 = acc1 + pltpu.roll(acc1, HEAD_DIM, 1)
    return jnp.where(left, t0, t1)


def _swa_valid_t(base):
    kpos = base - BLOCK + lax.broadcasted_iota(jnp.int32, (2 * BLOCK, BLOCK), 0)
    qpos = base + lax.broadcasted_iota(jnp.int32, (2 * BLOCK, BLOCK), 1)
    return jnp.logical_and(kpos >= 0, jnp.logical_and(kpos <= qpos, kpos > qpos - BLOCK))


def _swa_probs_t(qm, kk, valid_t, sink):
    s = jnp.where(valid_t, _dot_nt(kk, qm) * (HEAD_DIM ** -0.5), NEG)
    m = jnp.maximum(jnp.max(s, axis=0, keepdims=True), sink)
    e = jnp.exp(s - m)
    es = jnp.exp(sink - m)
    inv = 1.0 / (jnp.sum(e, axis=0, keepdims=True) + es)
    return e * inv, es * inv


def _swa_operands(kh_ref, kc_ref, vh_ref, vc_ref, nsb):
    out = []
    for h_ref, c_ref in ((kh_ref, kc_ref), (vh_ref, vc_ref)):
        dup = _dup_heads(jnp.concatenate([h_ref[...], c_ref[...]], axis=0).astype(F32))
        out.append([d.astype(BF16) for d in dup])
        out.append([[d[BLOCK * c:BLOCK * (c + 1), :].T.astype(BF16) for c in range(nsb + 1)] for d in dup])
    return out


def _halves(x):
    left = lax.broadcasted_iota(jnp.int32, x.shape, 1) < HEAD_DIM
    zero = jnp.zeros_like(x)
    return jnp.where(left, x, zero), jnp.where(left, zero, x)


def _swa_specs(T, bq, rev):
    n = T // bq
    blk = (lambda i: n - 1 - i) if rev else (lambda i: i)
    halo = lambda i: jnp.maximum(blk(i) * (bq // BLOCK) - 1, 0)
    return blk, [
        pl.BlockSpec(memory_space=pltpu.SMEM),
        pl.BlockSpec((bq, 512), lambda i: (blk(i), 0)),
        pl.BlockSpec((bq, 128), lambda i: (blk(i), 4)),
        pl.BlockSpec((BLOCK, 128), lambda i: (halo(i), 4)),
        pl.BlockSpec((bq, 128), lambda i: (blk(i), N_VB // 128)),
        pl.BlockSpec((BLOCK, 128), lambda i: (halo(i), N_VB // 128)),
        pl.BlockSpec((bq, 512), lambda i: (blk(i), N_GB // 512)),
    ]


def _swa_fwd(qkr, p, sinks):
    T = p.shape[0]
    bq = min(512, T)
    nsb = bq // BLOCK
    blk, specs = _swa_specs(T, bq, False)

    def body(sink_ref, q_ref, kc_ref, kh_ref, vc_ref, vh_ref, gt_ref, y_ref):
        base = blk(pl.program_id(0)) * bq
        kk, _, _, vT = _swa_operands(kh_ref, kc_ref, vh_ref, vc_ref, nsb)
        top = lax.broadcasted_iota(jnp.int32, (BLOCK, BLOCK), 0) < HEAD_DIM
        for sb in range(nsb):
            rows = slice(sb * BLOCK, (sb + 1) * BLOCK)
            keys = slice(sb * BLOCK, (sb + 2) * BLOCK)
            valid_t = _swa_valid_t(base + sb * BLOCK)
            for G in range(4):
                g = G // 2
                cols = slice(128 * G, 128 * (G + 1))
                vvt = jnp.concatenate([vT[g][sb], vT[g][sb + 1]], axis=1)
                o_t = []
                for hh, qm in enumerate(_halves(q_ref[rows, cols])):
                    pr, _ = _swa_probs_t(qm, kk[g][keys], valid_t, sink_ref[2 * G + hh])
                    o_t.append(_dot(vvt, pr.astype(BF16)))
                o = jnp.where(top, o_t[0], o_t[1]).T
                sg, _ = _silu_and_grad(gt_ref[rows, cols].astype(F32))
                y_ref[rows, cols] = (o * sg).astype(y_ref.dtype)

    return _call(
        body, name="swa_fwd", grid=(T // bq,), in_specs=specs,
        out_specs=pl.BlockSpec((bq, 512), lambda i: (i, 0)),
        out_shape=jax.ShapeDtypeStruct((T, 512), BF16),
    )(sinks, qkr, qkr, qkr, p, p, p)


def _swa_bwd(qkr, p, dy, sinks, cos_t, sin_t, dp):
    T = p.shape[0]
    bq = min(512, T)
    nsb = bq // BLOCK
    blk, specs = _swa_specs(T, bq, True)

    def body(sink_ref, q_ref, kc_ref, kh_ref, vc_ref, vh_ref, gt_ref, dy_ref, c_ref, s_ref, _,
             dp_ref, ds_ref, dk_acc, dv_acc, k_carry, v_carry):
        @pl.when(pl.program_id(0) == 0)
        def _():
            k_carry[...] = jnp.zeros_like(k_carry)
            v_carry[...] = jnp.zeros_like(v_carry)
            ds_ref[...] = jnp.zeros_like(ds_ref)

        base = blk(pl.program_id(0)) * bq
        dk_acc[...] = jnp.zeros_like(dk_acc)
        dv_acc[...] = jnp.zeros_like(dv_acc)
        kk, kT, vv, vT = _swa_operands(kh_ref, kc_ref, vh_ref, vc_ref, nsb)
        top = lax.broadcasted_iota(jnp.int32, (BLOCK, BLOCK), 0) < HEAD_DIM
        for sb in range(nsb):
            rows = slice(sb * BLOCK, (sb + 1) * BLOCK)
            keys = slice(sb * BLOCK, (sb + 2) * BLOCK)
            valid_t = _swa_valid_t(base + sb * BLOCK)
            dkp = [jnp.zeros((2 * BLOCK, BLOCK), F32)] * 2
            dvp = [jnp.zeros((2 * BLOCK, BLOCK), F32)] * 2
            for G in range(4):
                g = G // 2
                cols = slice(128 * G, 128 * (G + 1))
                kkt = jnp.concatenate([kT[g][sb], kT[g][sb + 1]], axis=1)
                vvt = jnp.concatenate([vT[g][sb], vT[g][sb + 1]], axis=1)
                qms = _halves(q_ref[rows, cols])
                prs, pss, o_t = [], [], []
                for hh in range(2):
                    pr, ps = _swa_probs_t(qms[hh], kk[g][keys], valid_t, sink_ref[2 * G + hh])
                    prs.append(pr)
                    pss.append(ps)
                    o_t.append(_dot(vvt, pr.astype(BF16)))
                o = jnp.where(top, o_t[0], o_t[1]).T
                sg, dsg = _silu_and_grad(gt_ref[rows, cols].astype(F32))
                dyf = dy_ref[rows, cols].astype(F32)
                do = dyf * sg
                dp_ref[rows, 1024 + 128 * G:1024 + 128 * (G + 1)] = (dyf * o * dsg).astype(dp_ref.dtype)
                do_t = do.T
                doms = _halves(do.astype(BF16))
                dq_t = []
                for hh in range(2):
                    dom_t = jnp.where(top if hh == 0 else jnp.logical_not(top), do_t, 0.0).astype(BF16)
                    dpv = _dot(vv[g][keys], dom_t)
                    delta = jnp.sum(prs[hh] * dpv, axis=0, keepdims=True)
                    dsb = (prs[hh] * (dpv - delta) * (HEAD_DIM ** -0.5)).astype(BF16)
                    h = 2 * G + hh
                    ds_ref[h:h + 1, :] += jnp.broadcast_to(
                        -jnp.sum(pss[hh] * delta, axis=1, keepdims=True), (1, 128))
                    dq_t.append(_dot(kkt, dsb))
                    dkp[g] = dkp[g] + _dot(dsb, qms[hh])
                    dvp[g] = dvp[g] + _dot(prs[hh].astype(BF16), doms[hh])
                dq = jnp.where(top, dq_t[0], dq_t[1]).T
                dp_ref[rows, cols] = _unrotate(dq, c_ref[rows, :], s_ref[rows, :]).astype(dp_ref.dtype)
            dk_acc[keys, :] += _fold_heads(dkp[0], dkp[1])
            dv_acc[keys, :] += _fold_heads(dvp[0], dvp[1])
        dk_acc[bq:bq + BLOCK, :] += k_carry[...]
        dv_acc[bq:bq + BLOCK, :] += v_carry[...]
        dk = _unrotate(dk_acc[BLOCK:bq + BLOCK, :], c_ref[...], s_ref[...])
        dp_ref[:, 512:640] = dk.astype(dp_ref.dtype)
        dp_ref[:, 640:768] = dv_acc[BLOCK:bq + BLOCK, :].astype(dp_ref.dtype)
        dp_ref[:, 768:1024] = jnp.zeros((bq, 256), dp_ref.dtype)
        k_carry[...] = dk_acc[0:BLOCK, :]
        v_carry[...] = dv_acc[0:BLOCK, :]

    rowblk = lambda w: pl.BlockSpec((bq, w), lambda i: (blk(i), 0))
    return _call(
        body, name="swa_bwd", grid=(T // bq,), in_specs=specs + [rowblk(512), rowblk(128), rowblk(128), _HBM],
        out_specs=[pl.BlockSpec((bq, 1536), lambda i: (blk(i), N_QB // 1536)), _full((SW_HEADS, 128))],
        out_shape=[jax.ShapeDtypeStruct(dp.shape, dp.dtype), jax.ShapeDtypeStruct((SW_HEADS, 128), F32)],
        scratch=[pltpu.VMEM((bq + BLOCK, 128), F32), pltpu.VMEM((bq + BLOCK, 128), F32),
                 pltpu.VMEM((BLOCK, 128), F32), pltpu.VMEM((BLOCK, 128), F32)],
        aliases={10: 0},
    )(sinks, qkr, qkr, qkr, p, p, p, dy, cos_t, sin_t, dp)


def _split_dot(x, tri):
    hi = x.astype(BF16)
    lo = (x - hi.astype(F32)).astype(BF16)
    return _dot(hi, tri) + _dot(lo, tri)


def _sb_logits(qs, kj):
    z = _dot_nt(qs, kj)
    e = jnp.exp(-jnp.abs(z))
    return z, e, -(jnp.maximum(z, 0.0) + jnp.log(1.0 + e))


def _sb_before(tq, qpos0, kpos0):
    qpos = qpos0 + lax.broadcasted_iota(jnp.int32, (tq, tq), 0)
    kpos = kpos0 + lax.broadcasted_iota(jnp.int32, (tq, tq), 1)
    return kpos < qpos


def _sb_tris(kb):
    row = lax.broadcasted_iota(jnp.int32, (kb, kb), 0)
    col = lax.broadcasted_iota(jnp.int32, (kb, kb), 1)
    return jnp.stack([jnp.where(row >= col, 1.0, 0.0), jnp.where(row > col, 1.0, 0.0)]).astype(BF16)


SB_TQ = 256
SB_NSUB = 2


def _sb_tiles(qi, w, tq):
    for t in range(SB_NSUB):
        jb = qi - SB_NSUB * w - t
        yield pl.ds(pl.multiple_of(jnp.maximum(jb, 0) * tq, tq), tq), jb, jb >= 0


def _sb_fwd(p, tris):
    T = p.shape[0]
    tq = min(SB_TQ, T)
    scale = SB_HEAD_DIM ** -0.5

    def body(q_ref, gt_ref, k_ref, v_ref, tri_ref, o_ref, y_ref, acc, r_ref):
        qi = pl.program_id(1)
        qs = (q_ref[...].astype(F32) * scale).astype(BF16)
        acc[...] = jnp.zeros_like(acc)
        r_ref[...] = jnp.zeros_like(r_ref)

        def step(carry):
            w, _ = carry
            r = r_ref[...]
            out = None
            for t, (rows, jb, live) in enumerate(_sb_tiles(qi, w, tq)):
                keep = _sb_before(tq, qi * tq, jb * tq) if t == 0 else live
                z, _, lf = _sb_logits(qs, k_ref[rows, :])
                lf = jnp.where(keep, lf, 0.0)
                a = jnp.where(keep, jnp.exp(z + _split_dot(lf, tri_ref[0]) + r), 0.0)
                term = _dot(a.astype(BF16), v_ref[rows, :])
                out = term if out is None else out + term
                r = r + jnp.sum(lf, axis=-1, keepdims=True)
            acc[...] += out
            r_ref[...] = r
            return w + 1, jnp.max(r) > SB_EXIT

        lax.while_loop(lambda c: jnp.logical_and(qi - SB_NSUB * c[0] >= 0, c[1]), step, (0, True))
        o = acc[...]
        o_ref[...] = o
        sg, _ = _silu_and_grad(gt_ref[...].astype(F32))
        y_ref[...] = (o * sg).astype(y_ref.dtype)

    col = lambda k: pl.BlockSpec((tq, 128), lambda h, i: (i, N_C // 128 + 4 * h + k))
    whole = lambda k: pl.BlockSpec((T, 128), lambda h, i: (0, N_C // 128 + 4 * h + k))
    out = pl.BlockSpec((tq, 128), lambda h, i: (i, h))
    return _call(
        body, name="sb_fwd", grid=(SB_HEADS, T // tq),
        in_specs=[col(0), col(1), whole(2), whole(3), _full((2, tq, tq))],
        out_specs=[out, out],
        out_shape=[jax.ShapeDtypeStruct((T, 512), F32), jax.ShapeDtypeStruct((T, 512), BF16)],
        scratch=[pltpu.VMEM((tq, 128), F32), pltpu.VMEM((tq, 1), F32)],
    )(p, p, p, p, tris)


def _sb_bwd(p, o, dy, tris, dp):
    T = p.shape[0]
    tq = min(SB_TQ, T)
    nq = T // tq
    scale = SB_HEAD_DIM ** -0.5

    def body(q_ref, gt_ref, k_ref, v_ref, o_ref, dy_ref, tri_ref, _, dqg_ref, dk_hbm, dv_hbm,
             dq_acc, r_ref, s_ref, dk_acc, dv_acc, sem):
        h, qi = pl.program_id(0), pl.program_id(1)

        @pl.when(qi == 0)
        def _():
            dk_acc[...] = jnp.zeros_like(dk_acc)
            dv_acc[...] = jnp.zeros_like(dv_acc)

        qs = (q_ref[...].astype(F32) * scale).astype(BF16)
        of = o_ref[...]
        dyf = dy_ref[...].astype(F32)
        sg, dsg = _silu_and_grad(gt_ref[...].astype(F32))
        do = dyf * sg
        dqg_ref[:, 128:256] = (dyf * of * dsg).astype(dqg_ref.dtype)
        delta = jnp.sum(do * of, axis=-1, keepdims=True)
        dob = do.astype(BF16)
        dq_acc[...] = jnp.zeros_like(dq_acc)
        r_ref[...] = jnp.zeros_like(r_ref)
        s_ref[...] = jnp.zeros_like(s_ref)

        def tile(rows, keep, r_in, s_in):
            kj, vj = k_ref[rows, :], v_ref[rows, :]
            z, e, lf = _sb_logits(qs, kj)
            lf = jnp.where(keep, lf, 0.0)
            a = jnp.where(keep, jnp.exp(z + _split_dot(lf, tri_ref[0]) + r_in), 0.0)
            gz = a * _dot_nt(dob, vj)
            later = _split_dot(gz, tri_ref[1]) + s_in
            sig = jnp.where(z >= 0.0, 1.0, e) / (1.0 + e)
            dz = jnp.where(keep, gz - sig * (delta - later), 0.0)
            dk_acc[rows, :] += _dot(dz.T.astype(BF16), qs)
            dv_acc[rows, :] += _dot(a.T.astype(BF16), dob)
            return (_dot(dz.astype(BF16), kj), jnp.sum(lf, axis=-1, keepdims=True),
                    jnp.sum(gz, axis=-1, keepdims=True))

        def step(carry):
            w, _ = carry
            r, sm, dq = r_ref[...], s_ref[...], None
            for t, (rows, jb, live) in enumerate(_sb_tiles(qi, w, tq)):
                dqt, lt, gt = tile(rows, _sb_before(tq, qi * tq, jb * tq) if t == 0 else live, r, sm)
                dq = dqt if dq is None else dq + dqt
                r, sm = r + lt, sm + gt
            dq_acc[...] += dq
            s_ref[...] = sm
            r_ref[...] = r
            return w + 1, jnp.max(r) > SB_EXIT

        lax.while_loop(lambda c: jnp.logical_and(qi - SB_NSUB * c[0] >= 0, c[1]), step, (0, True))
        dqg_ref[:, 0:128] = (dq_acc[...] * scale).astype(dqg_ref.dtype)

        @pl.when(qi == nq - 1)
        def _():
            cols = pl.ds(pl.multiple_of(h * 128, 128), 128)
            ck = pltpu.make_async_copy(dk_acc, dk_hbm.at[:, cols], sem.at[0])
            cv = pltpu.make_async_copy(dv_acc, dv_hbm.at[:, cols], sem.at[1])
            ck.start()
            cv.start()
            ck.wait()
            cv.wait()

    col = lambda k: pl.BlockSpec((tq, 128), lambda h, i: (i, N_C // 128 + 4 * h + k))
    whole = lambda k: pl.BlockSpec((T, 128), lambda h, i: (0, N_C // 128 + 4 * h + k))
    blk = pl.BlockSpec((tq, 128), lambda h, i: (i, h))
    return _call(
        body, name="sb_bwd", grid=(SB_HEADS, nq),
        in_specs=[col(0), col(1), whole(2), whole(3), blk, blk, _full((2, tq, tq)), _HBM],
        out_specs=[pl.BlockSpec((tq, 256), lambda h, i: (i, N_C // 256 + 2 * h)), _HBM, _HBM],
        out_shape=[jax.ShapeDtypeStruct(dp.shape, dp.dtype),
                   jax.ShapeDtypeStruct((T, 512), F32), jax.ShapeDtypeStruct((T, 512), F32)],
        scratch=[pltpu.VMEM((tq, 128), F32), pltpu.VMEM((tq, 1), F32), pltpu.VMEM((tq, 1), F32),
                 pltpu.VMEM((T, 128), F32), pltpu.VMEM((T, 128), F32), pltpu.SemaphoreType.DMA((2,))],
        aliases={7: 0},
    )(p, p, p, p, o, dy, tris, dp)


def _sb_kv_into_dp(dk, dv, dp):
    T = dk.shape[0]
    tm = min(1024, T)

    def body(dk_ref, dv_ref, _, o_ref):
        o_ref[:, 0:128] = dk_ref[...].astype(o_ref.dtype)
        o_ref[:, 128:256] = dv_ref[...].astype(o_ref.dtype)

    blk = pl.BlockSpec((tm, 128), lambda i, h: (i, h))
    return _call(
        body, name="sb_kv_into_dp", grid=(T // tm, SB_HEADS), in_specs=[blk, blk, _HBM],
        out_specs=pl.BlockSpec((tm, 256), lambda i, h: (i, N_C // 256 + 2 * h + 1)),
        out_shape=jax.ShapeDtypeStruct(dp.shape, dp.dtype), aliases={2: 0},
    )(dk, dv, dp)


def _post_fwd(ya, yb, yc, p, bm, wa, wb, wc, wo, g_post, x):
    T = x.shape[0]
    tm = min(512, T)

    def body(ya_ref, yb_ref, yc_ref, la_ref, lb_ref, lc_ref, bm_ref, wa_ref, wb_ref, wc_ref, wo_ref,
             g_ref, x_ref, m_ref, out_ref, xn_ref):
        merged = None
        for k, (y_ref, l_ref, w_ref) in enumerate(
                ((ya_ref, la_ref, wa_ref), (yb_ref, lb_ref, wb_ref), (yc_ref, lc_ref, wc_ref))):
            gate = _sigmoid(l_ref[...].astype(F32) + bm_ref[k:k + 1, :])
            term = gate * _dot(y_ref[...], w_ref[...])
            merged = term if merged is None else merged + term
        mb = merged.astype(BF16)
        m_ref[...] = mb
        out = _dot(mb, wo_ref[...])
        out_ref[...] = out
        r = lax.rsqrt(jnp.mean(out * out, axis=-1, keepdims=True) + EPS)
        xn_ref[...] = x_ref[...] + out * r * g_ref[...]

    yspec = pl.BlockSpec((tm, 512), lambda i: (i, 0))
    lspec = lambda k: pl.BlockSpec((tm, D), lambda i: (i, k))
    row = pl.BlockSpec((tm, D), lambda i: (i, 0))
    return _call(
        body, name="post_fwd", grid=(T // tm,),
        in_specs=[yspec, yspec, yspec, lspec(0), lspec(1), lspec(2), _full((3, D)),
                  _full((512, D)), _full((512, D)), _full((512, D)), _full((D, D)), _full((1, D)), row],
        out_specs=[row, row, row],
        out_shape=[jax.ShapeDtypeStruct((T, D), BF16), jax.ShapeDtypeStruct((T, D), F32),
                   jax.ShapeDtypeStruct((T, D), F32)],
    )(ya, yb, yc, p, p, p, bm, wa, wb, wc, wo, g_post, x)


def _post_bwd(dxn, out, ya, yb, yc, p, bm, wa, wb, wc, wo, g_post):
    T = dxn.shape[0]
    tm = min(256, T)

    def body(dxn_ref, out_ref, ya_ref, yb_ref, yc_ref, la_ref, lb_ref, lc_ref, bm_ref,
             wa_ref, wb_ref, wc_ref, wo_ref, g_ref,
             do_ref, dpa_ref, dpb_ref, dpc_ref, dl_ref, dya_ref, dyb_ref, dyc_ref, dg_ref, db_ref):
        @pl.when(pl.program_id(0) == 0)
        def _():
            dg_ref[...] = jnp.zeros_like(dg_ref)
            db_ref[...] = jnp.zeros_like(db_ref)

        out = out_ref[...]
        dxn_ = dxn_ref[...]
        r = lax.rsqrt(jnp.mean(out * out, axis=-1, keepdims=True) + EPS)
        a = dxn_ * g_ref[...]
        d_out = r * a - out * (r * r * r) * jnp.mean(a * out, axis=-1, keepdims=True)
        dg_ref[...] += jnp.sum(dxn_ * out * r, axis=0, keepdims=True)
        dob = d_out.astype(BF16)
        do_ref[...] = dob
        dmerged = _dot_nt(dob, wo_ref[...])
        for k, (y_ref, l_ref, w_ref, dp_ref, dy_ref) in enumerate((
                (ya_ref, la_ref, wa_ref, dpa_ref, dya_ref), (yb_ref, lb_ref, wb_ref, dpb_ref, dyb_ref),
                (yc_ref, lc_ref, wc_ref, dpc_ref, dyc_ref))):
            gate = _sigmoid(l_ref[...].astype(F32) + bm_ref[k:k + 1, :])
            proj = _dot(y_ref[...], w_ref[...])
            dproj = (dmerged * gate).astype(BF16)
            dp_ref[...] = dproj
            dlog = dmerged * proj * gate * (1.0 - gate)
            dl_ref[:, D * k:D * (k + 1)] = dlog.astype(dl_ref.dtype)
            db_ref[k:k + 1, :] += jnp.sum(dlog, axis=0, keepdims=True)
            dy_ref[...] = _dot_nt(dproj, w_ref[...]).astype(dy_ref.dtype)

    yspec = pl.BlockSpec((tm, 512), lambda i: (i, 0))
    lspec = lambda k: pl.BlockSpec((tm, D), lambda i: (i, k))
    row = pl.BlockSpec((tm, D), lambda i: (i, 0))
    sds = jax.ShapeDtypeStruct
    return _call(
        body, name="post_bwd", grid=(T // tm,),
        in_specs=[row, row, yspec, yspec, yspec, lspec(0), lspec(1), lspec(2), _full((3, D)),
                  _full((512, D)), _full((512, D)), _full((512, D)), _full((D, D)), _full((1, D))],
        out_specs=[row, row, row, row, pl.BlockSpec((tm, 3 * D), lambda i: (i, 0)), yspec, yspec, yspec,
                   _full((1, D)), _full((3, D))],
        out_shape=[sds((T, D), BF16), sds((T, D), BF16), sds((T, D), BF16), sds((T, D), BF16),
                   sds((T, IN_PAD), ACT), sds((T, 512), ACT), sds((T, 512), ACT), sds((T, 512), ACT),
                   sds((1, D), F32), sds((3, D), F32)],
    )(dxn, out, ya, yb, yc, p, p, p, bm, wa, wb, wc, wo, g_post)


def _loss_and_grad(y, target):
    T = y.shape[0]
    tm = min(1024, T)

    def body(y_ref, t_ref, l_ref, dy_ref):
        @pl.when(pl.program_id(0) == 0)
        def _():
            l_ref[...] = jnp.zeros_like(l_ref)
        e = y_ref[...] - t_ref[...]
        l_ref[...] += jnp.sum(e * e, axis=0, keepdims=True)
        dy_ref[...] = e * (1.0 / D)

    row = pl.BlockSpec((tm, D), lambda i: (i, 0))
    return _call(
        body, name="loss", grid=(T // tm,), in_specs=[row, row], out_specs=[_full((1, D)), row],
        out_shape=[jax.ShapeDtypeStruct((1, D), F32), jax.ShapeDtypeStruct((T, D), F32)],
    )(y, target)


def _sibling_swap(srcs, name):
    n = len(srcs)

    def body(*refs):
        s_refs, r_refs = refs[:n], refs[n:2 * n]
        send_sems, recv_sems = refs[2 * n:]
        x, y, c = lax.axis_index("x"), lax.axis_index("y"), lax.axis_index("c")
        copies = [pltpu.make_async_remote_copy(
            src_ref=s_refs[i], dst_ref=r_refs[i], send_sem=send_sems.at[i], recv_sem=recv_sems.at[i],
            device_id=(x, y, 1 - c), device_id_type=MESH) for i in range(n)]
        for cp in copies:
            cp.start()
        for cp in copies:
            cp.wait()

    return pl.pallas_call(
        body, name=name, in_specs=[_HBM] * n, out_specs=[_HBM] * n,
        out_shape=[jax.ShapeDtypeStruct(a.shape, a.dtype) for a in srcs],
        scratch_shapes=[pltpu.SemaphoreType.DMA((n,)), pltpu.SemaphoreType.DMA((n,))],
    )(*srcs)


_ELEMENTWISE_BLOCK_BYTES = 1 << 20


def _row_tile(rows, cols):
    if rows * cols * 4 <= 2 * _ELEMENTWISE_BLOCK_BYTES:
        return rows
    for tr in (2048, 1024, 512, 256, 128, 64, 32, 16, 8):
        if rows % tr == 0 and tr * cols * 4 <= _ELEMENTWISE_BLOCK_BYTES:
            return tr
    raise ValueError(f"no row tile for {(rows, cols)}")


def _sum_chips(r, name):
    _, R, C = r.shape
    tr = _row_tile(R, C)

    def body(r_ref, o_ref):
        f = lambda j: r_ref[j].astype(F32)
        o_ref[...] = ((f(0) + f(1)) + f(2)) + f(3)

    return _call(
        body, name=name, grid=(R // tr,),
        in_specs=[pl.BlockSpec((4, tr, C), lambda i: (0, i, 0))],
        out_specs=pl.BlockSpec((tr, C), lambda i: (i, 0)),
        out_shape=jax.ShapeDtypeStruct((R, C), F32),
    )(r)


def _adamw(w, m, v, g_mine, g_other, name):
    R, C = w.shape
    tr = _row_tile(R, C)

    def body(w_ref, m_ref, v_ref, a_ref, b_ref, g_ref, d_ref, nm_ref, nv_ref):
        g = a_ref[...] + b_ref[...]
        g_ref[...] = g
        m_new = ADAM_B1 * m_ref[...] + (1.0 - ADAM_B1) * g
        v_new = ADAM_B2 * v_ref[...] + (1.0 - ADAM_B2) * (g * g)
        nm_ref[...] = m_new
        nv_ref[...] = v_new
        m_hat = m_new / (1.0 - ADAM_B1 ** ADAM_STEP)
        v_hat = v_new / (1.0 - ADAM_B2 ** ADAM_STEP)
        d_ref[...] = -ADAM_LR * (m_hat / (jnp.sqrt(v_hat) + ADAM_EPS) + ADAM_WD * w_ref[...])

    blk = pl.BlockSpec((tr, C), lambda i: (i, 0))
    sds = jax.ShapeDtypeStruct((R, C), F32)
    return _call(body, name=name, grid=(R // tr,), in_specs=[blk] * 5, out_specs=[blk] * 4,
                 out_shape=[sds] * 4)(w, m, v, g_mine, g_other)


_NAMES = ('w_in', 'gm_w_s', 'gm_b_s', 'gm_norm_gain', 'sw_sinks', 'w_branch_a', 'w_branch_b',
          'w_branch_c', 'b_merge', 'w_out', 'g_pre', 'g_post')
_SMALL = ('gm_w_s', 'gm_b_s', 'gm_norm_gain', 'sw_sinks', 'g_pre', 'g_post')
_PACK_COLS = 1024


def _pack(arrays):
    flat = jnp.concatenate([a.reshape(-1).astype(F32) for a in arrays])
    rows = -(-flat.shape[0] // (8 * _PACK_COLS)) * 8
    return jnp.pad(flat, (0, rows * _PACK_COLS - flat.shape[0])).reshape(rows, _PACK_COLS)


def _unpack(buf, shapes):
    flat = buf.reshape(-1)
    out, off = [], 0
    for s in shapes:
        n = int(np.prod(s))
        out.append(flat[off:off + n].reshape(s))
        off += n
    return out


def _permute_cols(w):
    return jnp.concatenate(
        [jnp.zeros(w.shape[:-1] + (n,), w.dtype) if o is None else w[..., o:o + n] for o, n in _PERM], axis=-1)


def _unpermute_cols(w):
    new_off, off = {}, 0
    for o, n in _PERM:
        if o is not None:
            new_off[o] = (off, n)
        off += n
    return jnp.concatenate([w[..., new_off[o][0]:new_off[o][0] + new_off[o][1]] for o in sorted(new_off)], axis=-1)


def _tables(T):
    half = HEAD_DIM // 2
    freqs = ROPE_THETA ** (-jnp.arange(half, dtype=F32) / half)
    ang = jnp.arange(T).astype(F32)[:, None] * freqs[None, :]
    cos, sin = jnp.cos(ang), jnp.sin(ang)
    return (jnp.tile(jnp.concatenate([cos, cos], axis=1), (1, 2)),
            jnp.tile(jnp.concatenate([-sin, sin], axis=1), (1, 2)), _sb_tris(min(SB_TQ, T)))


def _cat_chips(got, axis):
    return jnp.concatenate([got[j] for j in range(4)], axis=axis)


def _layer_fwd(x, lw, tables, carry):
    gpre, gpost = lw['g_pre'][None, :], lw['g_post'][None, :]
    bfull = jnp.broadcast_to(lw['bs'][:, :, None], (GM_GROUPS, BLOCK, BLOCK))
    gv = lw['gv'][None, :]
    p, h, *got = _in_proj(x, gpre, lw['w_in'], carry)
    wabc = _cat_chips(got[0], 2)
    lw = dict(lw, wa=wabc[0], wb=wabc[1], wc=wabc[2], bm=_cat_chips(got[1], 1), wo=_cat_chips(got[2], 0))
    got = _permute_cols(_cat_chips(got[3], 1)) if len(got) > 3 else None
    ya = _gmlp_fwd(p, lw['ws'], bfull, gv)
    qkr = _rope_fwd(p, *tables[:2])
    yb = _swa_fwd(qkr, p, lw['sinks'])
    oc, yc = _sb_fwd(p, tables[2])
    merged, out, xn = _post_fwd(ya, yb, yc, p, lw['bm'], lw['wa'], lw['wb'], lw['wc'], lw['wo'], gpost, x)
    return xn, (lw, p, h, ya, qkr, yb, oc, yc, merged, out, bfull, gv, gpre, gpost), got


def _grad_partials(g):
    per_chip = lambda a, axis: jnp.stack(jnp.split(a, 4, axis=axis))
    g_abc = jnp.stack([g['w_branch_a'], g['w_branch_b'], g['w_branch_c']])
    return [per_chip(_unpermute_cols(g['w_in']), 1).astype(BF16),
            per_chip(g_abc, 2).astype(BF16),
            per_chip(g['b_merge'], 1),
            per_chip(g['w_out'], 0).astype(BF16)]


def _layer_bwd(x, saved, dxn, tables):
    lw, p, h, ya, qkr, yb, oc, yc, merged, out, bfull, gv, gpre, gpost = saved
    g = {}
    (d_out, dpa, dpb, dpc, dp, dya, dyb, dyc, dgpost, dbm) = _post_bwd(
        dxn, out, ya, yb, yc, p, lw['bm'], lw['wa'], lw['wb'], lw['wc'], lw['wo'], gpost)
    g['w_out'] = _matmul_tn(merged, d_out, "grad_w_out")
    g['w_branch_a'] = _matmul_tn(ya, dpa, "grad_w_a")
    g['w_branch_b'] = _matmul_tn(yb, dpb, "grad_w_b")
    g['w_branch_c'] = _matmul_tn(yc, dpc, "grad_w_c")
    g['g_post'] = dgpost[0]
    g['b_merge'] = dbm
    dp, dws, dbacc, dgv = _gmlp_bwd(p, dya, lw['ws'], bfull, gv, dp)
    g['gm_w_s'] = dws
    g['gm_b_s'] = jnp.sum(dbacc.reshape(BLOCK, GM_GROUPS, BLOCK), axis=2).T
    g['gm_norm_gain'] = dgv[0]
    dp, dsink = _swa_bwd(qkr, p, dyb, lw['sinks'], *tables[:2], dp)
    g['sw_sinks'] = dsink[:, 0]
    dp, dkc, dvc = _sb_bwd(p, oc, dyc, tables[2], dp)
    dp = _sb_kv_into_dp(dkc, dvc, dp)
    g['w_in'] = _matmul_tn(h, dp, "grad_w_in")
    dx, dgpre, *got = _in_bwd(dp, lw['w_in'], x, dxn, gpre, _grad_partials(g))
    g['g_pre'] = dgpre[0]
    return dx, g, got


def kernel(x, w_in, gm_w_s, gm_b_s, gm_norm_gain, sw_sinks, w_branch_a, w_branch_b, w_branch_c, b_merge, w_out, g_pre, g_post, loss_target, m_w_in, m_gm_w_s, m_gm_b_s, m_gm_norm_gain, m_sw_sinks, m_w_branch_a, m_w_branch_b, m_w_branch_c, m_b_merge, m_w_out, m_g_pre, m_g_post, v_w_in, v_gm_w_s, v_gm_b_s, v_gm_norm_gain, v_sw_sinks, v_w_branch_a, v_w_branch_b, v_w_branch_c, v_b_merge, v_w_out, v_g_pre, v_g_post):
    T = x.shape[1]
    weights = dict(zip(_NAMES, (w_in, gm_w_s, gm_b_s, gm_norm_gain, sw_sinks, w_branch_a, w_branch_b,
                                w_branch_c, b_merge, w_out, g_pre, g_post)))
    mom_m = dict(zip(_NAMES, (m_w_in, m_gm_w_s, m_gm_b_s, m_gm_norm_gain, m_sw_sinks, m_w_branch_a,
                              m_w_branch_b, m_w_branch_c, m_b_merge, m_w_out, m_g_pre, m_g_post)))
    mom_v = dict(zip(_NAMES, (v_w_in, v_gm_w_s, v_gm_b_s, v_gm_norm_gain, v_sw_sinks, v_w_branch_a,
                              v_w_branch_b, v_w_branch_c, v_b_merge, v_w_out, v_g_pre, v_g_post)))
    abc = lambda d: jnp.stack([d['w_branch_a'], d['w_branch_b'], d['w_branch_c']], axis=1)

    w_in_b, rest = w_in.astype(BF16), [abc(weights).astype(BF16), b_merge, w_out.astype(BF16)]
    tables = _tables(T)
    xs = [x[0]]
    saved = []
    got, = _chip_exchange([w_in_b[0]], (False,), "gather_w_in")
    w_in_l = _permute_cols(_cat_chips(got, 1))
    for l in range(DEPTH):
        lw = dict(w_in=w_in_l, ws=gm_w_s[l], bs=gm_b_s[l], gv=gm_norm_gain[l], sinks=sw_sinks[l],
                  g_pre=g_pre[l], g_post=g_post[l])
        carry = [a[l] for a in rest] + ([w_in_b[l + 1]] if l + 1 < DEPTH else [])
        xn, sv, w_in_l = _layer_fwd(xs[l], lw, tables, carry)
        xs.append(xn)
        saved.append(sv)

    lsum, dxn = _loss_and_grad(xs[DEPTH], loss_target[0])
    loss = lax.psum(0.5 * jnp.sum(lsum) / D, ("x", "y", "c"))

    small = {n: [None] * DEPTH for n in _SMALL}
    received = [None] * DEPTH
    for l in reversed(range(DEPTH)):
        dxn, g, received[l] = _layer_bwd(xs[l], saved[l], dxn, tables)
        for n in _SMALL:
            small[n][l] = g[n]
    grad_x = dxn[None]
    g_small = _pack([jnp.stack(small[n]) for n in _SMALL])
    got_small, = _chip_exchange([g_small], (False,), "gather_small_grads")

    views = [(4 * D, IN_WIDTH // 4), (4 * 3 * 512, D // 4), (4 * 3, D // 4), (D, D), g_small.shape]
    stacks = [jnp.stack([received[l][i] for l in range(DEPTH)], axis=1) for i in range(4)] + [got_small]
    sums = [_sum_chips(r.reshape((4,) + v), f"sum_chips_{i}") for i, (r, v) in enumerate(zip(stacks, views))]
    others = _sibling_swap(sums, "swap_core_sums")
    tensors = [lambda d: d['w_in'], abc, lambda d: d['b_merge'], lambda d: d['w_out'],
               lambda d: _pack([d[n] for n in _SMALL])]
    res = [_adamw(t(weights).reshape(v), t(mom_m).reshape(v), t(mom_v).reshape(v), s_, o_, f"adamw_{i}")
           for i, (t, v, s_, o_) in enumerate(zip(tensors, views, sums, others))]

    outs = []
    for kind in range(4):
        r = [res[i][kind] for i in range(5)]
        r_abc = r[1].reshape(4, 3, 512, D // 4)
        tiny = dict(zip(_SMALL, _unpack(r[4], [weights[n].shape for n in _SMALL])))
        big = dict(w_in=r[0].reshape(w_in.shape), w_branch_a=r_abc[:, 0], w_branch_b=r_abc[:, 1],
                   w_branch_c=r_abc[:, 2], b_merge=r[2].reshape(b_merge.shape), w_out=r[3].reshape(w_out.shape))
        outs.extend(big[n] if n in big else tiny[n] for n in _NAMES)
    return (loss, grad_x, *outs)
```

```python
import functools

import numpy as np
import jax
import jax.numpy as jnp
from jax import lax
from jax.experimental import pallas as pl
from jax.experimental.pallas import tpu as pltpu

F32 = jnp.float32
BF16 = jnp.bfloat16
ACT = jnp.bfloat16

D = 1024
DEPTH = 4
BLOCK = 128
EPS = 1e-6
NEG = -1e30
GM_GROUPS = 4
GM_WIDTH = 512
HEAD_DIM = 64
SW_HEADS = 8
SB_HEADS = 4
SB_HEAD_DIM = 128
ROPE_THETA = 10000.0
IN_WIDTH = 7936
IN_PAD = 8192

O_UA, O_VA, O_GA, O_QB, O_KB, O_VB, O_GB = 0, 512, 1024, 1536, 2048, 2176, 2304
O_QC, O_KC, O_VC, O_GC, O_MG = 2816, 3328, 3840, 4352, 4864
_PERM = ((O_MG, 3072), (O_UA, 512), (O_VA, 512), (O_GA, 512),
         (O_QB, 512), (O_KB, 128), (O_VB, 128), (None, IN_PAD - IN_WIDTH), (O_GB, 512)) + tuple(
    (o + 128 * h, 128) for h in range(4) for o in (O_QC, O_GC, O_KC, O_VC))
N_MG, N_UA, N_VA, N_GA = 0, 3072, 3584, 4096
N_QB, N_KB, N_VB, N_GB = 4608, 5120, 5248, 5632
N_C = 6144

ADAM_LR, ADAM_B1, ADAM_B2, ADAM_EPS, ADAM_WD, ADAM_STEP = 0.001, 0.9, 0.999, 1e-08, 0.01, 10

SB_EXIT = -104.0
V7X_VMEM_LIMIT = 56 * 1024 * 1024
MESH = pl.DeviceIdType.MESH


_HBM = pl.BlockSpec(memory_space=pl.ANY)


def _call(body, *, name, grid, in_specs, out_specs, out_shape, scratch=(), aliases=None):
    return pl.pallas_call(
        body, name=name, grid=grid, in_specs=in_specs, out_specs=out_specs, out_shape=out_shape,
        scratch_shapes=list(scratch), input_output_aliases=aliases or {},
        compiler_params=pltpu.CompilerParams(
            dimension_semantics=("arbitrary",) * len(grid), vmem_limit_bytes=V7X_VMEM_LIMIT))


def _sigmoid(x):
    return 1.0 / (1.0 + jnp.exp(-x))


def _silu_and_grad(x):
    s = _sigmoid(x)
    return x * s, s * (1.0 + x * (1.0 - s))


def _dot(a, b):
    return jnp.dot(a, b, preferred_element_type=F32)


def _dot_nt(a, b):
    return lax.dot_general(a, b, (((1,), (1,)), ((), ())), preferred_element_type=F32)


def _dot_tn(a, b):
    return lax.dot_general(a, b, (((0,), (0,)), ((), ())), preferred_element_type=F32)


def _full(shape):
    return pl.BlockSpec(shape, lambda *_: (0,) * len(shape))


_CHIP_STEPS = ((1, 0), (0, 1), (1, 1))


def _exchange_ops(s_refs, r_refs, send_sems, recv_sems, local_sems, per_target):
    n = len(s_refs)

    def copies():
        x, y, c = lax.axis_index("x"), lax.axis_index("y"), lax.axis_index("c")
        me = 2 * x + y
        pick = lambda i, j: s_refs[i].at[j] if per_target[i] else s_refs[i]
        own = [pltpu.make_async_copy(pick(i, me), r_refs[i].at[me], local_sems.at[i]) for i in range(n)]
        sent, arriving = [], []
        for k, (dx, dy) in enumerate(_CHIP_STEPS):
            tx, ty = (x + dx) % 2, (y + dy) % 2
            peer = 2 * tx + ty
            for i in range(n):
                sems = dict(send_sem=send_sems.at[3 * i + k], recv_sem=recv_sems.at[3 * i + k])
                sent.append(pltpu.make_async_remote_copy(
                    src_ref=pick(i, peer), dst_ref=r_refs[i].at[me], device_id=(tx, ty, c),
                    device_id_type=MESH, **sems))
                arriving.append(pltpu.make_async_remote_copy(
                    src_ref=pick(i, peer), dst_ref=r_refs[i].at[peer], device_id=(x, y, c),
                    device_id_type=MESH, **sems))
        return own, sent, arriving

    def start():
        own, sent, _ = copies()
        for cp in own + sent:
            cp.start()

    def wait():
        own, sent, arriving = copies()
        for cp in arriving:
            cp.wait_recv()
        for cp in sent:
            cp.wait_send()
        for cp in own:
            cp.wait()

    return start, wait


def _exchange_shapes(srcs, per_target):
    return [jax.ShapeDtypeStruct((4,) + (a.shape[1:] if pt else a.shape), a.dtype)
            for a, pt in zip(srcs, per_target)]


def _exchange_sems(n):
    return [pltpu.SemaphoreType.DMA((3 * n,)), pltpu.SemaphoreType.DMA((3 * n,)), pltpu.SemaphoreType.DMA((n,))]


def _chip_exchange(srcs, per_target, name):
    n = len(srcs)

    def body(*refs):
        start, wait = _exchange_ops(refs[:n], refs[n:2 * n], *refs[2 * n:], per_target)
        start()
        wait()

    return pl.pallas_call(
        body, name=name, in_specs=[_HBM] * n, out_specs=[_HBM] * n, out_shape=_exchange_shapes(srcs, per_target),
        scratch_shapes=_exchange_sems(n),
    )(*srcs)


def _in_proj(x, g_pre, w_in, carry=()):
    T = x.shape[0]
    tm, tn = min(2048, T), 2048
    ni, nj, n = T // tm, IN_PAD // tn, len(carry)

    def body(x_ref, g_ref, w_ref, *rest):
        p_ref, h_ref = rest[n], rest[n + 1]
        i, j = pl.program_id(0), pl.program_id(1)
        if n:
            start, wait = _exchange_ops(rest[:n], rest[n + 2:2 * n + 2], *rest[2 * n + 2:], (False,) * n)
            pl.when((i == 0) & (j == 0))(start)

        @pl.when(j == 0)
        def _():
            xf = x_ref[...]
            r = lax.rsqrt(jnp.mean(xf * xf, axis=-1, keepdims=True) + EPS)
            h_ref[...] = (xf * r * g_ref[...]).astype(BF16)
        p_ref[...] = _dot(h_ref[...], w_ref[...]).astype(p_ref.dtype)
        if n:
            pl.when((i == ni - 1) & (j == nj - 1))(wait)

    return _call(
        body, name="in_proj_gather" if n else "in_proj", grid=(ni, nj),
        in_specs=[pl.BlockSpec((tm, D), lambda i, j: (i, 0)), _full((1, D)),
                  pl.BlockSpec((D, tn), lambda i, j: (0, j))] + [_HBM] * n,
        out_specs=[pl.BlockSpec((tm, tn), lambda i, j: (i, j)), pl.BlockSpec((tm, D), lambda i, j: (i, 0))]
        + [_HBM] * n,
        out_shape=[jax.ShapeDtypeStruct((T, IN_PAD), ACT), jax.ShapeDtypeStruct((T, D), BF16)]
        + _exchange_shapes(carry, (False,) * n),
        scratch=_exchange_sems(n) if n else (),
    )(x, g_pre, w_in, *carry)


def _in_bwd(dp, w_in, x, dxn, g_pre, carry=()):
    T = x.shape[0]
    tm, tk = min(1024, T), 2048
    ni, nk, n = T // tm, IN_PAD // tk, len(carry)

    def body(dp_ref, w_ref, x_ref, dxn_ref, g_ref, *rest):
        dx_ref, dg_ref, acc = rest[n], rest[n + 1], rest[2 * n + 2]
        i, k = pl.program_id(0), pl.program_id(1)
        if n:
            start, wait = _exchange_ops(rest[:n], rest[n + 2:2 * n + 2], *rest[2 * n + 3:], (True,) * n)
            pl.when((i == 0) & (k == 0))(start)

        @pl.when(k == 0)
        def _():
            acc[...] = jnp.zeros_like(acc)

        @pl.when((i == 0) & (k == 0))
        def _():
            dg_ref[...] = jnp.zeros_like(dg_ref)

        acc[...] += _dot_nt(dp_ref[...], w_ref[...])

        @pl.when(k == nk - 1)
        def _():
            dh = acc[...]
            xf = x_ref[...]
            r = lax.rsqrt(jnp.mean(xf * xf, axis=-1, keepdims=True) + EPS)
            a = dh * g_ref[...]
            dx_ref[...] = dxn_ref[...] + r * a - xf * (r * r * r) * jnp.mean(a * xf, axis=-1, keepdims=True)
            dg_ref[...] += jnp.sum(dh * xf * r, axis=0, keepdims=True)

        if n:
            pl.when((i == ni - 1) & (k == nk - 1))(wait)

    return _call(
        body, name="in_bwd_scatter" if n else "in_bwd", grid=(ni, nk),
        in_specs=[pl.BlockSpec((tm, tk), lambda i, k: (i, k)), pl.BlockSpec((D, tk), lambda i, k: (0, k)),
                  pl.BlockSpec((tm, D), lambda i, k: (i, 0)), pl.BlockSpec((tm, D), lambda i, k: (i, 0)),
                  _full((1, D))] + [_HBM] * n,
        out_specs=[pl.BlockSpec((tm, D), lambda i, k: (i, 0)), _full((1, D))] + [_HBM] * n,
        out_shape=[jax.ShapeDtypeStruct((T, D), F32), jax.ShapeDtypeStruct((1, D), F32)]
        + _exchange_shapes(carry, (True,) * n),
        scratch=[pltpu.VMEM((tm, D), F32)] + (_exchange_sems(n) if n else []),
    )(dp, w_in, x, dxn, g_pre, *carry)


def _matmul_tn(a, b, name):
    T, K = a.shape
    N = b.shape[1]
    tk, tn, tt = min(K, 1024), min(N, 2048), min(T, 1024)

    def body(a_ref, b_ref, o_ref):
        @pl.when(pl.program_id(2) == 0)
        def _():
            o_ref[...] = jnp.zeros_like(o_ref)
        o_ref[...] += _dot_tn(a_ref[...], b_ref[...])

    return _call(
        body, name=name, grid=(K // tk, N // tn, T // tt),
        in_specs=[pl.BlockSpec((tt, tk), lambda i, j, t: (t, i)), pl.BlockSpec((tt, tn), lambda i, j, t: (t, j))],
        out_specs=pl.BlockSpec((tk, tn), lambda i, j, t: (i, j)),
        out_shape=jax.ShapeDtypeStruct((K, N), F32),
    )(a, b)


def _gm_forward_parts(v_ref, gv_ref, ws_ref, bf_ref, nch):
    vf = v_ref[...].astype(F32)
    mu = jnp.mean(vf, axis=-1, keepdims=True)
    xc = vf - mu
    rstd = lax.rsqrt(jnp.mean(xc * xc, axis=-1, keepdims=True) + EPS)
    xhat = xc * rstd
    vnb = (xhat * gv_ref[...]).astype(BF16)
    row = lax.broadcasted_iota(jnp.int32, (BLOCK, BLOCK), 0)
    col = lax.broadcasted_iota(jnp.int32, (BLOCK, BLOCK), 1)
    mixed, vcats, wgs = [], [], []
    for g in range(GM_GROUPS):
        vg = vnb[:, BLOCK * g:BLOCK * (g + 1)]
        vcat = jnp.concatenate([vg[BLOCK * k:BLOCK * (k + 1), :] for k in range(nch)], axis=1)
        wg = jnp.where(row >= col, ws_ref[g], 0.0)
        m = _dot(wg.astype(BF16), vcat)
        mixed.append(jnp.concatenate(
            [m[:, BLOCK * k:BLOCK * (k + 1)] + bf_ref[g] for k in range(nch)], axis=0))
        vcats.append(vcat)
        wgs.append(wg)
    return xhat, rstd, jnp.concatenate(mixed, axis=1), vcats, wgs, row >= col


def _gmlp_fwd(p, ws, bfull, gv):
    T = p.shape[0]
    tm = min(512, T)
    nch = tm // BLOCK

    def body(u_ref, v_ref, gt_ref, ws_ref, bf_ref, gv_ref, y_ref):
        _, _, mixed, _, _, _ = _gm_forward_parts(v_ref, gv_ref, ws_ref, bf_ref, nch)
        sg, _ = _silu_and_grad(gt_ref[...].astype(F32))
        y_ref[...] = (u_ref[...].astype(F32) * mixed * sg).astype(y_ref.dtype)

    seg = lambda off: pl.BlockSpec((tm, 512), lambda i: (i, off // 512))
    return _call(
        body, name="gmlp_fwd", grid=(T // tm,),
        in_specs=[seg(N_UA), seg(N_VA), seg(N_GA), _full((GM_GROUPS, BLOCK, BLOCK)),
                  _full((GM_GROUPS, BLOCK, BLOCK)), _full((1, GM_WIDTH))],
        out_specs=pl.BlockSpec((tm, 512), lambda i: (i, 0)),
        out_shape=jax.ShapeDtypeStruct((T, GM_WIDTH), BF16),
    )(p, p, p, ws, bfull, gv)


def _gmlp_bwd(p, dy, ws, bfull, gv, dp):
    T = p.shape[0]
    tm = min(512, T)
    nch = tm // BLOCK

    def body(u_ref, v_ref, gt_ref, dy_ref, ws_ref, bf_ref, gv_ref, _, dp_ref, dws_ref, db_ref, dgv_ref):
        @pl.when(pl.program_id(0) == 0)
        def _():
            dws_ref[...] = jnp.zeros_like(dws_ref)
            db_ref[...] = jnp.zeros_like(db_ref)
            dgv_ref[...] = jnp.zeros_like(dgv_ref)

        xhat, rstd, mixed, vcats, wgs, tril = _gm_forward_parts(v_ref, gv_ref, ws_ref, bf_ref, nch)
        u = u_ref[...].astype(F32)
        gt = gt_ref[...].astype(F32)
        dyf = dy_ref[...].astype(F32)
        sg, dsg = _silu_and_grad(gt)
        du = dyf * mixed * sg
        dmixed = dyf * u * sg
        dgate = dyf * (u * mixed) * dsg
        dvn = []
        for g in range(GM_GROUPS):
            dmg = dmixed[:, BLOCK * g:BLOCK * (g + 1)]
            chunks = [dmg[BLOCK * k:BLOCK * (k + 1), :] for k in range(nch)]
            dmcat = jnp.concatenate(chunks, axis=1).astype(BF16)
            dws_ref[g] += jnp.where(tril, _dot_nt(dmcat, vcats[g]), 0.0)
            dvcat = _dot(wgs[g].T.astype(BF16), dmcat)
            dvn.append(jnp.concatenate([dvcat[:, BLOCK * k:BLOCK * (k + 1)] for k in range(nch)], axis=0))
            db_ref[:, BLOCK * g:BLOCK * (g + 1)] += functools.reduce(lambda a, b: a + b, chunks)
        dvn = jnp.concatenate(dvn, axis=1)
        dgv_ref[...] += jnp.sum(dvn * xhat, axis=0, keepdims=True)
        dxh = dvn * gv_ref[...]
        dv = rstd * (dxh - jnp.mean(dxh, axis=-1, keepdims=True)
                     - xhat * jnp.mean(dxh * xhat, axis=-1, keepdims=True))
        dp_ref[:, 0:512] = du.astype(dp_ref.dtype)
        dp_ref[:, 512:1024] = dv.astype(dp_ref.dtype)
        dp_ref[:, 1024:1536] = dgate.astype(dp_ref.dtype)

    seg = lambda off: pl.BlockSpec((tm, 512), lambda i: (i, off // 512))
    return _call(
        body, name="gmlp_bwd", grid=(T // tm,),
        in_specs=[seg(N_UA), seg(N_VA), seg(N_GA), pl.BlockSpec((tm, 512), lambda i: (i, 0)),
                  _full((GM_GROUPS, BLOCK, BLOCK)), _full((GM_GROUPS, BLOCK, BLOCK)), _full((1, GM_WIDTH)), _HBM],
        out_specs=[pl.BlockSpec((tm, 1536), lambda i: (i, N_UA // 1536)), _full((GM_GROUPS, BLOCK, BLOCK)),
                   _full((BLOCK, GM_WIDTH)), _full((1, GM_WIDTH))],
        out_shape=[jax.ShapeDtypeStruct(dp.shape, dp.dtype), jax.ShapeDtypeStruct((GM_GROUPS, BLOCK, BLOCK), F32),
                   jax.ShapeDtypeStruct((BLOCK, GM_WIDTH), F32), jax.ShapeDtypeStruct((1, GM_WIDTH), F32)],
        aliases={7: 0},
    )(p, p, p, dy, ws, bfull, gv, dp)


def _swap_halves(x):
    lane = lax.broadcasted_iota(jnp.int32, x.shape, 1) % HEAD_DIM
    return jnp.where(lane < HEAD_DIM // 2, pltpu.roll(x, 96, 1), pltpu.roll(x, 32, 1))


def _rope_fwd(p, cos_t, sin_t):
    T = p.shape[0]
    tm = min(512, T)

    def body(q_ref, k_ref, c_ref, s_ref, o_ref):
        c, s = c_ref[...], s_ref[...]
        for G in range(5):
            xg = (q_ref[:, 128 * G:128 * (G + 1)] if G < 4 else k_ref[...]).astype(F32)
            o_ref[:, 128 * G:128 * (G + 1)] = (xg * c + _swap_halves(xg) * s).astype(o_ref.dtype)

    return _call(
        body, name="rope_fwd", grid=(T // tm,),
        in_specs=[pl.BlockSpec((tm, 512), lambda i: (i, N_QB // 512)),
                  pl.BlockSpec((tm, 128), lambda i: (i, N_KB // 128)),
                  pl.BlockSpec((tm, 128), lambda i: (i, 0)), pl.BlockSpec((tm, 128), lambda i: (i, 0))],
        out_specs=pl.BlockSpec((tm, 640), lambda i: (i, 0)),
        out_shape=jax.ShapeDtypeStruct((T, 640), BF16),
    )(p, p, cos_t, sin_t)


def _unrotate(d, c, s):
    return d * c + _swap_halves(d * s)


def _dup_heads(x):
    left = lax.broadcasted_iota(jnp.int32, x.shape, 1) < HEAD_DIM
    r = pltpu.roll(x, HEAD_DIM, 1)
    return jnp.where(left, x, r), jnp.where(left, r, x)


def _fold_heads(acc0, acc1):
    left = lax.broadcasted_iota(jnp.int32, acc0.shape, 1) < HEAD_DIM
    t0 = acc0 + pltpu.roll(acc0, HEAD_DIM, 1)
    t1 = acc1 + pltpu.roll(acc1, HEAD_DIM, 1)
    return jnp.where(left, t0, t1)


def _swa_valid_t(base):
    kpos = base - BLOCK + lax.broadcasted_iota(jnp.int32, (2 * BLOCK, BLOCK), 0)
    qpos = base + lax.broadcasted_iota(jnp.int32, (2 * BLOCK, BLOCK), 1)
    return jnp.logical_and(kpos >= 0, jnp.logical_and(kpos <= qpos, kpos > qpos - BLOCK))


def _swa_probs_t(qm, kk, valid_t, sink):
    s = jnp.where(valid_t, _dot_nt(kk, qm) * (HEAD_DIM ** -0.5), NEG)
    m = jnp.maximum(jnp.max(s, axis=0, keepdims=True), sink)
    e = jnp.exp(s - m)
    es = jnp.exp(sink - m)
    inv = 1.0 / (jnp.sum(e, axis=0, keepdims=True) + es)
    return e * inv, es * inv


def _swa_operands(kh_ref, kc_ref, vh_ref, vc_ref, nsb):
    out = []
    for h_ref, c_ref in ((kh_ref, kc_ref), (vh_ref, vc_ref)):
        dup = _dup_heads(jnp.concatenate([h_ref[...], c_ref[...]], axis=0).astype(F32))
        out.append([d.astype(BF16) for d in dup])
        out.append([[d[BLOCK * c:BLOCK * (c + 1), :].T.astype(BF16) for c in range(nsb + 1)] for d in dup])
    return out


def _halves(x):
    left = lax.broadcasted_iota(jnp.int32, x.shape, 1) < HEAD_DIM
    zero = jnp.zeros_like(x)
    return jnp.where(left, x, zero), jnp.where(left, zero, x)


def _swa_specs(T, bq, rev):
    n = T // bq
    blk = (lambda i: n - 1 - i) if rev else (lambda i: i)
    halo = lambda i: jnp.maximum(blk(i) * (bq // BLOCK) - 1, 0)
    return blk, [
        pl.BlockSpec(memory_space=pltpu.SMEM),
        pl.BlockSpec((bq, 512), lambda i: (blk(i), 0)),
        pl.BlockSpec((bq, 128), lambda i: (blk(i), 4)),
        pl.BlockSpec((BLOCK, 128), lambda i: (halo(i), 4)),
        pl.BlockSpec((bq, 128), lambda i: (blk(i), N_VB // 128)),
        pl.BlockSpec((BLOCK, 128), lambda i: (halo(i), N_VB // 128)),
        pl.BlockSpec((bq, 512), lambda i: (blk(i), N_GB // 512)),
    ]


def _swa_fwd(qkr, p, sinks):
    T = p.shape[0]
    bq = min(512, T)
    nsb = bq // BLOCK
    blk, specs = _swa_specs(T, bq, False)

    def body(sink_ref, q_ref, kc_ref, kh_ref, vc_ref, vh_ref, gt_ref, y_ref):
        base = blk(pl.program_id(0)) * bq
        kk, _, _, vT = _swa_operands(kh_ref, kc_ref, vh_ref, vc_ref, nsb)
        top = lax.broadcasted_iota(jnp.int32, (BLOCK, BLOCK), 0) < HEAD_DIM
        for sb in range(nsb):
            rows = slice(sb * BLOCK, (sb + 1) * BLOCK)
            keys = slice(sb * BLOCK, (sb + 2) * BLOCK)
            valid_t = _swa_valid_t(base + sb * BLOCK)
            for G in range(4):
                g = G // 2
                cols = slice(128 * G, 128 * (G + 1))
                vvt = jnp.concatenate([vT[g][sb], vT[g][sb + 1]], axis=1)
                o_t = []
                for hh, qm in enumerate(_halves(q_ref[rows, cols])):
                    pr, _ = _swa_probs_t(qm, kk[g][keys], valid_t, sink_ref[2 * G + hh])
                    o_t.append(_dot(vvt, pr.astype(BF16)))
                o = jnp.where(top, o_t[0], o_t[1]).T
                sg, _ = _silu_and_grad(gt_ref[rows, cols].astype(F32))
                y_ref[rows, cols] = (o * sg).astype(y_ref.dtype)

    return _call(
        body, name="swa_fwd", grid=(T // bq,), in_specs=specs,
        out_specs=pl.BlockSpec((bq, 512), lambda i: (i, 0)),
        out_shape=jax.ShapeDtypeStruct((T, 512), BF16),
    )(sinks, qkr, qkr, qkr, p, p, p)


def _swa_bwd(qkr, p, dy, sinks, cos_t, sin_t, dp):
    T = p.shape[0]
    bq = min(512, T)
    nsb = bq // BLOCK
    blk, specs = _swa_specs(T, bq, True)

    def body(sink_ref, q_ref, kc_ref, kh_ref, vc_ref, vh_ref, gt_ref, dy_ref, c_ref, s_ref, _,
             dp_ref, ds_ref, dk_acc, dv_acc, k_carry, v_carry):
        @pl.when(pl.program_id(0) == 0)
        def _():
            k_carry[...] = jnp.zeros_like(k_carry)
            v_carry[...] = jnp.zeros_like(v_carry)
            ds_ref[...] = jnp.zeros_like(ds_ref)

        base = blk(pl.program_id(0)) * bq
        dk_acc[...] = jnp.zeros_like(dk_acc)
        dv_acc[...] = jnp.zeros_like(dv_acc)
        kk, kT, vv, vT = _swa_operands(kh_ref, kc_ref, vh_ref, vc_ref, nsb)
        top = lax.broadcasted_iota(jnp.int32, (BLOCK, BLOCK), 0) < HEAD_DIM
        for sb in range(nsb):
            rows = slice(sb * BLOCK, (sb + 1) * BLOCK)
            keys = slice(sb * BLOCK, (sb + 2) * BLOCK)
            valid_t = _swa_valid_t(base + sb * BLOCK)
            dkp = [jnp.zeros((2 * BLOCK, BLOCK), F32)] * 2
            dvp = [jnp.zeros((2 * BLOCK, BLOCK), F32)] * 2
            for G in range(4):
                g = G // 2
                cols = slice(128 * G, 128 * (G + 1))
                kkt = jnp.concatenate([kT[g][sb], kT[g][sb + 1]], axis=1)
                vvt = jnp.concatenate([vT[g][sb], vT[g][sb + 1]], axis=1)
                qms = _halves(q_ref[rows, cols])
                prs, pss, o_t = [], [], []
                for hh in range(2):
                    pr, ps = _swa_probs_t(qms[hh], kk[g][keys], valid_t, sink_ref[2 * G + hh])
                    prs.append(pr)
                    pss.append(ps)
                    o_t.append(_dot(vvt, pr.astype(BF16)))
                o = jnp.where(top, o_t[0], o_t[1]).T
                sg, dsg = _silu_and_grad(gt_ref[rows, cols].astype(F32))
                dyf = dy_ref[rows, cols].astype(F32)
                do = dyf * sg
                dp_ref[rows, 1024 + 128 * G:1024 + 128 * (G + 1)] = (dyf * o * dsg).astype(dp_ref.dtype)
                do_t = do.T
                doms = _halves(do.astype(BF16))
                dq_t = []
                for hh in range(2):
                    dom_t = jnp.where(top if hh == 0 else jnp.logical_not(top), do_t, 0.0).astype(BF16)
                    dpv = _dot(vv[g][keys], dom_t)
                    delta = jnp.sum(prs[hh] * dpv, axis=0, keepdims=True)
                    dsb = (prs[hh] * (dpv - delta) * (HEAD_DIM ** -0.5)).astype(BF16)
                    h = 2 * G + hh
                    ds_ref[h:h + 1, :] += jnp.broadcast_to(
                        -jnp.sum(pss[hh] * delta, axis=1, keepdims=True), (1, 128))
                    dq_t.append(_dot(kkt, dsb))
                    dkp[g] = dkp[g] + _dot(dsb, qms[hh])
                    dvp[g] = dvp[g] + _dot(prs[hh].astype(BF16), doms[hh])
                dq = jnp.where(top, dq_t[0], dq_t[1]).T
                dp_ref[rows, cols] = _unrotate(dq, c_ref[rows, :], s_ref[rows, :]).astype(dp_ref.dtype)
            dk_acc[keys, :] += _fold_heads(dkp[0], dkp[1])
            dv_acc[keys, :] += _fold_heads(dvp[0], dvp[1])
        dk_acc[bq:bq + BLOCK, :] += k_carry[...]
        dv_acc[bq:bq + BLOCK, :] += v_carry[...]
        dk = _unrotate(dk_acc[BLOCK:bq + BLOCK, :], c_ref[...], s_ref[...])
        dp_ref[:, 512:640] = dk.astype(dp_ref.dtype)
        dp_ref[:, 640:768] = dv_acc[BLOCK:bq + BLOCK, :].astype(dp_ref.dtype)
        dp_ref[:, 768:1024] = jnp.zeros((bq, 256), dp_ref.dtype)
        k_carry[...] = dk_acc[0:BLOCK, :]
        v_carry[...] = dv_acc[0:BLOCK, :]

    rowblk = lambda w: pl.BlockSpec((bq, w), lambda i: (blk(i), 0))
    return _call(
        body, name="swa_bwd", grid=(T // bq,), in_specs=specs + [rowblk(512), rowblk(128), rowblk(128), _HBM],
        out_specs=[pl.BlockSpec((bq, 1536), lambda i: (blk(i), N_QB // 1536)), _full((SW_HEADS, 128))],
        out_shape=[jax.ShapeDtypeStruct(dp.shape, dp.dtype), jax.ShapeDtypeStruct((SW_HEADS, 128), F32)],
        scratch=[pltpu.VMEM((bq + BLOCK, 128), F32), pltpu.VMEM((bq + BLOCK, 128), F32),
                 pltpu.VMEM((BLOCK, 128), F32), pltpu.VMEM((BLOCK, 128), F32)],
        aliases={10: 0},
    )(sinks, qkr, qkr, qkr, p, p, p, dy, cos_t, sin_t, dp)


def _split_dot(x, tri):
    hi = x.astype(BF16)
    lo = (x - hi.astype(F32)).astype(BF16)
    return _dot(hi, tri) + _dot(lo, tri)


def _sb_logits(qs, kj):
    z = _dot_nt(qs, kj)
    e = jnp.exp(-jnp.abs(z))
    return z, e, -(jnp.maximum(z, 0.0) + jnp.log(1.0 + e))


def _sb_before(tq, qpos0, kpos0):
    qpos = qpos0 + lax.broadcasted_iota(jnp.int32, (tq, tq), 0)
    kpos = kpos0 + lax.broadcasted_iota(jnp.int32, (tq, tq), 1)
    return kpos < qpos


def _sb_tris(kb):
    row = lax.broadcasted_iota(jnp.int32, (kb, kb), 0)
    col = lax.broadcasted_iota(jnp.int32, (kb, kb), 1)
    return jnp.stack([jnp.where(row >= col, 1.0, 0.0), jnp.where(row > col, 1.0, 0.0)]).astype(BF16)


SB_TQ = 256
SB_NSUB = 2


def _sb_tiles(qi, w, tq):
    for t in range(SB_NSUB):
        jb = qi - SB_NSUB * w - t
        yield pl.ds(pl.multiple_of(jnp.maximum(jb, 0) * tq, tq), tq), jb, jb >= 0


def _sb_fwd(p, tris):
    T = p.shape[0]
    tq = min(SB_TQ, T)
    scale = SB_HEAD_DIM ** -0.5

    def body(q_ref, gt_ref, k_ref, v_ref, tri_ref, o_ref, y_ref, acc, r_ref):
        qi = pl.program_id(1)
        qs = (q_ref[...].astype(F32) * scale).astype(BF16)
        acc[...] = jnp.zeros_like(acc)
        r_ref[...] = jnp.zeros_like(r_ref)

        def step(carry):
            w, _ = carry
            r = r_ref[...]
            out = None
            for t, (rows, jb, live) in enumerate(_sb_tiles(qi, w, tq)):
                keep = _sb_before(tq, qi * tq, jb * tq) if t == 0 else live
                z, _, lf = _sb_logits(qs, k_ref[rows, :])
                lf = jnp.where(keep, lf, 0.0)
                a = jnp.where(keep, jnp.exp(z + _split_dot(lf, tri_ref[0]) + r), 0.0)
                term = _dot(a.astype(BF16), v_ref[rows, :])
                out = term if out is None else out + term
                r = r + jnp.sum(lf, axis=-1, keepdims=True)
            acc[...] += out
            r_ref[...] = r
            return w + 1, jnp.max(r) > SB_EXIT

        lax.while_loop(lambda c: jnp.logical_and(qi - SB_NSUB * c[0] >= 0, c[1]), step, (0, True))
        o = acc[...]
        o_ref[...] = o
        sg, _ = _silu_and_grad(gt_ref[...].astype(F32))
        y_ref[...] = (o * sg).astype(y_ref.dtype)

    col = lambda k: pl.BlockSpec((tq, 128), lambda h, i: (i, N_C // 128 + 4 * h + k))
    whole = lambda k: pl.BlockSpec((T, 128), lambda h, i: (0, N_C // 128 + 4 * h + k))
    out = pl.BlockSpec((tq, 128), lambda h, i: (i, h))
    return _call(
        body, name="sb_fwd", grid=(SB_HEADS, T // tq),
        in_specs=[col(0), col(1), whole(2), whole(3), _full((2, tq, tq))],
        out_specs=[out, out],
        out_shape=[jax.ShapeDtypeStruct((T, 512), F32), jax.ShapeDtypeStruct((T, 512), BF16)],
        scratch=[pltpu.VMEM((tq, 128), F32), pltpu.VMEM((tq, 1), F32)],
    )(p, p, p, p, tris)


def _sb_bwd(p, o, dy, tris, dp):
    T = p.shape[0]
    tq = min(SB_TQ, T)
    nq = T // tq
    scale = SB_HEAD_DIM ** -0.5

    def body(q_ref, gt_ref, k_ref, v_ref, o_ref, dy_ref, tri_ref, _, dqg_ref, dk_hbm, dv_hbm,
             dq_acc, r_ref, s_ref, dk_acc, dv_acc, sem):
        h, qi = pl.program_id(0), pl.program_id(1)

        @pl.when(qi == 0)
        def _():
            dk_acc[...] = jnp.zeros_like(dk_acc)
            dv_acc[...] = jnp.zeros_like(dv_acc)

        qs = (q_ref[...].astype(F32) * scale).astype(BF16)
        of = o_ref[...]
        dyf = dy_ref[...].astype(F32)
        sg, dsg = _silu_and_grad(gt_ref[...].astype(F32))
        do = dyf * sg
        dqg_ref[:, 128:256] = (dyf * of * dsg).astype(dqg_ref.dtype)
        delta = jnp.sum(do * of, axis=-1, keepdims=True)
        dob = do.astype(BF16)
        dq_acc[...] = jnp.zeros_like(dq_acc)
        r_ref[...] = jnp.zeros_like(r_ref)
        s_ref[...] = jnp.zeros_like(s_ref)

        def tile(rows, keep, r_in, s_in):
            kj, vj = k_ref[rows, :], v_ref[rows, :]
            z, e, lf = _sb_logits(qs, kj)
            lf = jnp.where(keep, lf, 0.0)
            a = jnp.where(keep, jnp.exp(z + _split_dot(lf, tri_ref[0]) + r_in), 0.0)
            gz = a * _dot_nt(dob, vj)
            later = _split_dot(gz, tri_ref[1]) + s_in
            sig = jnp.where(z >= 0.0, 1.0, e) / (1.0 + e)
            dz = jnp.where(keep, gz - sig * (delta - later), 0.0)
            dk_acc[rows, :] += _dot(dz.T.astype(BF16), qs)
            dv_acc[rows, :] += _dot(a.T.astype(BF16), dob)
            return (_dot(dz.astype(BF16), kj), jnp.sum(lf, axis=-1, keepdims=True),
                    jnp.sum(gz, axis=-1, keepdims=True))

        def step(carry):
            w, _ = carry
            r, sm, dq = r_ref[...], s_ref[...], None
            for t, (rows, jb, live) in enumerate(_sb_tiles(qi, w, tq)):
                dqt, lt, gt = tile(rows, _sb_before(tq, qi * tq, jb * tq) if t == 0 else live, r, sm)
                dq = dqt if dq is None else dq + dqt
                r, sm = r + lt, sm + gt
            dq_acc[...] += dq
            s_ref[...] = sm
            r_ref[...] = r
            return w + 1, jnp.max(r) > SB_EXIT

        lax.while_loop(lambda c: jnp.logical_and(qi - SB_NSUB * c[0] >= 0, c[1]), step, (0, True))
        dqg_ref[:, 0:128] = (dq_acc[...] * scale).astype(dqg_ref.dtype)

        @pl.when(qi == nq - 1)
        def _():
            cols = pl.ds(pl.multiple_of(h * 128, 128), 128)
            ck = pltpu.make_async_copy(dk_acc, dk_hbm.at[:, cols], sem.at[0])
            cv = pltpu.make_async_copy(dv_acc, dv_hbm.at[:, cols], sem.at[1])
            ck.start()
            cv.start()
            ck.wait()
            cv.wait()

    col = lambda k: pl.BlockSpec((tq, 128), lambda h, i: (i, N_C // 128 + 4 * h + k))
    whole = lambda k: pl.BlockSpec((T, 128), lambda h, i: (0, N_C // 128 + 4 * h + k))
    blk = pl.BlockSpec((tq, 128), lambda h, i: (i, h))
    return _call(
        body, name="sb_bwd", grid=(SB_HEADS, nq),
        in_specs=[col(0), col(1), whole(2), whole(3), blk, blk, _full((2, tq, tq)), _HBM],
        out_specs=[pl.BlockSpec((tq, 256), lambda h, i: (i, N_C // 256 + 2 * h)), _HBM, _HBM],
        out_shape=[jax.ShapeDtypeStruct(dp.shape, dp.dtype),
                   jax.ShapeDtypeStruct((T, 512), F32), jax.ShapeDtypeStruct((T, 512), F32)],
        scratch=[pltpu.VMEM((tq, 128), F32), pltpu.VMEM((tq, 1), F32), pltpu.VMEM((tq, 1), F32),
                 pltpu.VMEM((T, 128), F32), pltpu.VMEM((T, 128), F32), pltpu.SemaphoreType.DMA((2,))],
        aliases={7: 0},
    )(p, p, p, p, o, dy, tris, dp)


def _sb_kv_into_dp(dk, dv, dp):
    T = dk.shape[0]
    tm = min(1024, T)

    def body(dk_ref, dv_ref, _, o_ref):
        o_ref[:, 0:128] = dk_ref[...].astype(o_ref.dtype)
        o_ref[:, 128:256] = dv_ref[...].astype(o_ref.dtype)

    blk = pl.BlockSpec((tm, 128), lambda i, h: (i, h))
    return _call(
        body, name="sb_kv_into_dp", grid=(T // tm, SB_HEADS), in_specs=[blk, blk, _HBM],
        out_specs=pl.BlockSpec((tm, 256), lambda i, h: (i, N_C // 256 + 2 * h + 1)),
        out_shape=jax.ShapeDtypeStruct(dp.shape, dp.dtype), aliases={2: 0},
    )(dk, dv, dp)


def _post_fwd(ya, yb, yc, p, bm, wa, wb, wc, wo, g_post, x):
    T = x.shape[0]
    tm = min(512, T)

    def body(ya_ref, yb_ref, yc_ref, la_ref, lb_ref, lc_ref, bm_ref, wa_ref, wb_ref, wc_ref, wo_ref,
             g_ref, x_ref, m_ref, out_ref, xn_ref):
        merged = None
        for k, (y_ref, l_ref, w_ref) in enumerate(
                ((ya_ref, la_ref, wa_ref), (yb_ref, lb_ref, wb_ref), (yc_ref, lc_ref, wc_ref))):
            gate = _sigmoid(l_ref[...].astype(F32) + bm_ref[k:k + 1, :])
            term = gate * _dot(y_ref[...], w_ref[...])
            merged = term if merged is None else merged + term
        mb = merged.astype(BF16)
        m_ref[...] = mb
        out = _dot(mb, wo_ref[...])
        out_ref[...] = out
        r = lax.rsqrt(jnp.mean(out * out, axis=-1, keepdims=True) + EPS)
        xn_ref[...] = x_ref[...] + out * r * g_ref[...]

    yspec = pl.BlockSpec((tm, 512), lambda i: (i, 0))
    lspec = lambda k: pl.BlockSpec((tm, D), lambda i: (i, k))
    row = pl.BlockSpec((tm, D), lambda i: (i, 0))
    return _call(
        body, name="post_fwd", grid=(T // tm,),
        in_specs=[yspec, yspec, yspec, lspec(0), lspec(1), lspec(2), _full((3, D)),
                  _full((512, D)), _full((512, D)), _full((512, D)), _full((D, D)), _full((1, D)), row],
        out_specs=[row, row, row],
        out_shape=[jax.ShapeDtypeStruct((T, D), BF16), jax.ShapeDtypeStruct((T, D), F32),
                   jax.ShapeDtypeStruct((T, D), F32)],
    )(ya, yb, yc, p, p, p, bm, wa, wb, wc, wo, g_post, x)


def _post_bwd(dxn, out, ya, yb, yc, p, bm, wa, wb, wc, wo, g_post):
    T = dxn.shape[0]
    tm = min(256, T)

    def body(dxn_ref, out_ref, ya_ref, yb_ref, yc_ref, la_ref, lb_ref, lc_ref, bm_ref,
             wa_ref, wb_ref, wc_ref, wo_ref, g_ref,
             do_ref, dpa_ref, dpb_ref, dpc_ref, dl_ref, dya_ref, dyb_ref, dyc_ref, dg_ref, db_ref):
        @pl.when(pl.program_id(0) == 0)
        def _():
            dg_ref[...] = jnp.zeros_like(dg_ref)
            db_ref[...] = jnp.zeros_like(db_ref)

        out = out_ref[...]
        dxn_ = dxn_ref[...]
        r = lax.rsqrt(jnp.mean(out * out, axis=-1, keepdims=True) + EPS)
        a = dxn_ * g_ref[...]
        d_out = r * a - out * (r * r * r) * jnp.mean(a * out, axis=-1, keepdims=True)
        dg_ref[...] += jnp.sum(dxn_ * out * r, axis=0, keepdims=True)
        dob = d_out.astype(BF16)
        do_ref[...] = dob
        dmerged = _dot_nt(dob, wo_ref[...])
        for k, (y_ref, l_ref, w_ref, dp_ref, dy_ref) in enumerate((
                (ya_ref, la_ref, wa_ref, dpa_ref, dya_ref), (yb_ref, lb_ref, wb_ref, dpb_ref, dyb_ref),
                (yc_ref, lc_ref, wc_ref, dpc_ref, dyc_ref))):
            gate = _sigmoid(l_ref[...].astype(F32) + bm_ref[k:k + 1, :])
            proj = _dot(y_ref[...], w_ref[...])
            dproj = (dmerged * gate).astype(BF16)
            dp_ref[...] = dproj
            dlog = dmerged * proj * gate * (1.0 - gate)
            dl_ref[:, D * k:D * (k + 1)] = dlog.astype(dl_ref.dtype)
            db_ref[k:k + 1, :] += jnp.sum(dlog, axis=0, keepdims=True)
            dy_ref[...] = _dot_nt(dproj, w_ref[...]).astype(dy_ref.dtype)

    yspec = pl.BlockSpec((tm, 512), lambda i: (i, 0))
    lspec = lambda k: pl.BlockSpec((tm, D), lambda i: (i, k))
    row = pl.BlockSpec((tm, D), lambda i: (i, 0))
    sds = jax.ShapeDtypeStruct
    return _call(
        body, name="post_bwd", grid=(T // tm,),
        in_specs=[row, row, yspec, yspec, yspec, lspec(0), lspec(1), lspec(2), _full((3, D)),
                  _full((512, D)), _full((512, D)), _full((512, D)), _full((D, D)), _full((1, D))],
        out_specs=[row, row, row, row, pl.BlockSpec((tm, 3 * D), lambda i: (i, 0)), yspec, yspec, yspec,
                   _full((1, D)), _full((3, D))],
        out_shape=[sds((T, D), BF16), sds((T, D), BF16), sds((T, D), BF16), sds((T, D), BF16),
                   sds((T, IN_PAD), ACT), sds((T, 512), ACT), sds((T, 512), ACT), sds((T, 512), ACT),
                   sds((1, D), F32), sds((3, D), F32)],
    )(dxn, out, ya, yb, yc, p, p, p, bm, wa, wb, wc, wo, g_post)


def _loss_and_grad(y, target):
    T = y.shape[0]
    tm = min(1024, T)

    def body(y_ref, t_ref, l_ref, dy_ref):
        @pl.when(pl.program_id(0) == 0)
        def _():
            l_ref[...] = jnp.zeros_like(l_ref)
        e = y_ref[...] - t_ref[...]
        l_ref[...] += jnp.sum(e * e, axis=0, keepdims=True)
        dy_ref[...] = e * (1.0 / D)

    row = pl.BlockSpec((tm, D), lambda i: (i, 0))
    return _call(
        body, name="loss", grid=(T // tm,), in_specs=[row, row], out_specs=[_full((1, D)), row],
        out_shape=[jax.ShapeDtypeStruct((1, D), F32), jax.ShapeDtypeStruct((T, D), F32)],
    )(y, target)


def _sibling_swap(srcs, name):
    n = len(srcs)

    def body(*refs):
        s_refs, r_refs = refs[:n], refs[n:2 * n]
        send_sems, recv_sems = refs[2 * n:]
        x, y, c = lax.axis_index("x"), lax.axis_index("y"), lax.axis_index("c")
        copies = [pltpu.make_async_remote_copy(
            src_ref=s_refs[i], dst_ref=r_refs[i], send_sem=send_sems.at[i], recv_sem=recv_sems.at[i],
            device_id=(x, y, 1 - c), device_id_type=MESH) for i in range(n)]
        for cp in copies:
            cp.start()
        for cp in copies:
            cp.wait()

    return pl.pallas_call(
        body, name=name, in_specs=[_HBM] * n, out_specs=[_HBM] * n,
        out_shape=[jax.ShapeDtypeStruct(a.shape, a.dtype) for a in srcs],
        scratch_shapes=[pltpu.SemaphoreType.DMA((n,)), pltpu.SemaphoreType.DMA((n,))],
    )(*srcs)


_ELEMENTWISE_BLOCK_BYTES = 1 << 20


def _row_tile(rows, cols):
    if rows * cols * 4 <= 2 * _ELEMENTWISE_BLOCK_BYTES:
        return rows
    for tr in (2048, 1024, 512, 256, 128, 64, 32, 16, 8):
        if rows % tr == 0 and tr * cols * 4 <= _ELEMENTWISE_BLOCK_BYTES:
            return tr
    raise ValueError(f"no row tile for {(rows, cols)}")


def _sum_chips(r, name):
    _, R, C = r.shape
    tr = _row_tile(R, C)

    def body(r_ref, o_ref):
        f = lambda j: r_ref[j].astype(F32)
        o_ref[...] = ((f(0) + f(1)) + f(2)) + f(3)

    return _call(
        body, name=name, grid=(R // tr,),
        in_specs=[pl.BlockSpec((4, tr, C), lambda i: (0, i, 0))],
        out_specs=pl.BlockSpec((tr, C), lambda i: (i, 0)),
        out_shape=jax.ShapeDtypeStruct((R, C), F32),
    )(r)


def _adamw(w, m, v, g_mine, g_other, name):
    R, C = w.shape
    tr = _row_tile(R, C)

    def body(w_ref, m_ref, v_ref, a_ref, b_ref, g_ref, d_ref, nm_ref, nv_ref):
        g = a_ref[...] + b_ref[...]
        g_ref[...] = g
        m_new = ADAM_B1 * m_ref[...] + (1.0 - ADAM_B1) * g
        v_new = ADAM_B2 * v_ref[...] + (1.0 - ADAM_B2) * (g * g)
        nm_ref[...] = m_new
        nv_ref[...] = v_new
        m_hat = m_new / (1.0 - ADAM_B1 ** ADAM_STEP)
        v_hat = v_new / (1.0 - ADAM_B2 ** ADAM_STEP)
        d_ref[...] = -ADAM_LR * (m_hat / (jnp.sqrt(v_hat) + ADAM_EPS) + ADAM_WD * w_ref[...])

    blk = pl.BlockSpec((tr, C), lambda i: (i, 0))
    sds = jax.ShapeDtypeStruct((R, C), F32)
    return _call(body, name=name, grid=(R // tr,), in_specs=[blk] * 5, out_specs=[blk] * 4,
                 out_shape=[sds] * 4)(w, m, v, g_mine, g_other)


_NAMES = ('w_in', 'gm_w_s', 'gm_b_s', 'gm_norm_gain', 'sw_sinks', 'w_branch_a', 'w_branch_b',
          'w_branch_c', 'b_merge', 'w_out', 'g_pre', 'g_post')
_SMALL = ('gm_w_s', 'gm_b_s', 'gm_norm_gain', 'sw_sinks', 'g_pre', 'g_post')
_PACK_COLS = 1024


def _pack(arrays):
    flat = jnp.concatenate([a.reshape(-1).astype(F32) for a in arrays])
    rows = -(-flat.shape[0] // (8 * _PACK_COLS)) * 8
    return jnp.pad(flat, (0, rows * _PACK_COLS - flat.shape[0])).reshape(rows, _PACK_COLS)


def _unpack(buf, shapes):
    flat = buf.reshape(-1)
    out, off = [], 0
    for s in shapes:
        n = int(np.prod(s))
        out.append(flat[off:off + n].reshape(s))
        off += n
    return out


def _permute_cols(w):
    return jnp.concatenate(
        [jnp.zeros(w.shape[:-1] + (n,), w.dtype) if o is None else w[..., o:o + n] for o, n in _PERM], axis=-1)


def _unpermute_cols(w):
    new_off, off = {}, 0
    for o, n in _PERM:
        if o is not None:
            new_off[o] = (off, n)
        off += n
    return jnp.concatenate([w[..., new_off[o][0]:new_off[o][0] + new_off[o][1]] for o in sorted(new_off)], axis=-1)


def _tables(T):
    half = HEAD_DIM // 2
    freqs = ROPE_THETA ** (-jnp.arange(half, dtype=F32) / half)
    ang = jnp.arange(T).astype(F32)[:, None] * freqs[None, :]
    cos, sin = jnp.cos(ang), jnp.sin(ang)
    return (jnp.tile(jnp.concatenate([cos, cos], axis=1), (1, 2)),
            jnp.tile(jnp.concatenate([-sin, sin], axis=1), (1, 2)), _sb_tris(min(SB_TQ, T)))


def _cat_chips(got, axis):
    return jnp.concatenate([got[j] for j in range(4)], axis=axis)


def _layer_fwd(x, lw, tables, carry):
    gpre, gpost = lw['g_pre'][None, :], lw['g_post'][None, :]
    bfull = jnp.broadcast_to(lw['bs'][:, :, None], (GM_GROUPS, BLOCK, BLOCK))
    gv = lw['gv'][None, :]
    p, h, *got = _in_proj(x, gpre, lw['w_in'], carry)
    wabc = _cat_chips(got[0], 2)
    lw = dict(lw, wa=wabc[0], wb=wabc[1], wc=wabc[2], bm=_cat_chips(got[1], 1), wo=_cat_chips(got[2], 0))
    got = _permute_cols(_cat_chips(got[3], 1)) if len(got) > 3 else None
    ya = _gmlp_fwd(p, lw['ws'], bfull, gv)
    qkr = _rope_fwd(p, *tables[:2])
    yb = _swa_fwd(qkr, p, lw['sinks'])
    oc, yc = _sb_fwd(p, tables[2])
    merged, out, xn = _post_fwd(ya, yb, yc, p, lw['bm'], lw['wa'], lw['wb'], lw['wc'], lw['wo'], gpost, x)
    return xn, (lw, p, h, ya, qkr, yb, oc, yc, merged, out, bfull, gv, gpre, gpost), got


def _grad_partials(g):
    per_chip = lambda a, axis: jnp.stack(jnp.split(a, 4, axis=axis))
    g_abc = jnp.stack([g['w_branch_a'], g['w_branch_b'], g['w_branch_c']])
    return [per_chip(_unpermute_cols(g['w_in']), 1).astype(BF16),
            per_chip(g_abc, 2).astype(BF16),
            per_chip(g['b_merge'], 1),
            per_chip(g['w_out'], 0).astype(BF16)]


def _layer_bwd(x, saved, dxn, tables):
    lw, p, h, ya, qkr, yb, oc, yc, merged, out, bfull, gv, gpre, gpost = saved
    g = {}
    (d_out, dpa, dpb, dpc, dp, dya, dyb, dyc, dgpost, dbm) = _post_bwd(
        dxn, out, ya, yb, yc, p, lw['bm'], lw['wa'], lw['wb'], lw['wc'], lw['wo'], gpost)
    g['w_out'] = _matmul_tn(merged, d_out, "grad_w_out")
    g['w_branch_a'] = _matmul_tn(ya, dpa, "grad_w_a")
    g['w_branch_b'] = _matmul_tn(yb, dpb, "grad_w_b")
    g['w_branch_c'] = _matmul_tn(yc, dpc, "grad_w_c")
    g['g_post'] = dgpost[0]
    g['b_merge'] = dbm
    dp, dws, dbacc, dgv = _gmlp_bwd(p, dya, lw['ws'], bfull, gv, dp)
    g['gm_w_s'] = dws
    g['gm_b_s'] = jnp.sum(dbacc.reshape(BLOCK, GM_GROUPS, BLOCK), axis=2).T
    g['gm_norm_gain'] = dgv[0]
    dp, dsink = _swa_bwd(qkr, p, dyb, lw['sinks'], *tables[:2], dp)
    g['sw_sinks'] = dsink[:, 0]
    dp, dkc, dvc = _sb_bwd(p, oc, dyc, tables[2], dp)
    dp = _sb_kv_into_dp(dkc, dvc, dp)
    g['w_in'] = _matmul_tn(h, dp, "grad_w_in")
    dx, dgpre, *got = _in_bwd(dp, lw['w_in'], x, dxn, gpre, _grad_partials(g))
    g['g_pre'] = dgpre[0]
    return dx, g, got


def kernel(x, w_in, gm_w_s, gm_b_s, gm_norm_gain, sw_sinks, w_branch_a, w_branch_b, w_branch_c, b_merge, w_out, g_pre, g_post, loss_target, m_w_in, m_gm_w_s, m_gm_b_s, m_gm_norm_gain, m_sw_sinks, m_w_branch_a, m_w_branch_b, m_w_branch_c, m_b_merge, m_w_out, m_g_pre, m_g_post, v_w_in, v_gm_w_s, v_gm_b_s, v_gm_norm_gain, v_sw_sinks, v_w_branch_a, v_w_branch_b, v_w_branch_c, v_b_merge, v_w_out, v_g_pre, v_g_post):
    T = x.shape[1]
    weights = dict(zip(_NAMES, (w_in, gm_w_s, gm_b_s, gm_norm_gain, sw_sinks, w_branch_a, w_branch_b,
                                w_branch_c, b_merge, w_out, g_pre, g_post)))
    mom_m = dict(zip(_NAMES, (m_w_in, m_gm_w_s, m_gm_b_s, m_gm_norm_gain, m_sw_sinks, m_w_branch_a,
                              m_w_branch_b, m_w_branch_c, m_b_merge, m_w_out, m_g_pre, m_g_post)))
    mom_v = dict(zip(_NAMES, (v_w_in, v_gm_w_s, v_gm_b_s, v_gm_norm_gain, v_sw_sinks, v_w_branch_a,
                              v_w_branch_b, v_w_branch_c, v_b_merge, v_w_out, v_g_pre, v_g_post)))
    abc = lambda d: jnp.stack([d['w_branch_a'], d['w_branch_b'], d['w_branch_c']], axis=1)

    w_in_b, rest = w_in.astype(BF16), [abc(weights).astype(BF16), b_merge, w_out.astype(BF16)]
    tables = _tables(T)
    xs = [x[0]]
    saved = []
    got, = _chip_exchange([w_in_b[0]], (False,), "gather_w_in")
    w_in_l = _permute_cols(_cat_chips(got, 1))
    for l in range(DEPTH):
        lw = dict(w_in=w_in_l, ws=gm_w_s[l], bs=gm_b_s[l], gv=gm_norm_gain[l], sinks=sw_sinks[l],
                  g_pre=g_pre[l], g_post=g_post[l])
        carry = [a[l] for a in rest] + ([w_in_b[l + 1]] if l + 1 < DEPTH else [])
        xn, sv, w_in_l = _layer_fwd(xs[l], lw, tables, carry)
        xs.append(xn)
        saved.append(sv)

    lsum, dxn = _loss_and_grad(xs[DEPTH], loss_target[0])
    loss = lax.psum(0.5 * jnp.sum(lsum) / D, ("x", "y", "c"))

    small = {n: [None] * DEPTH for n in _SMALL}
    received = [None] * DEPTH
    for l in reversed(range(DEPTH)):
        dxn, g, received[l] = _layer_bwd(xs[l], saved[l], dxn, tables)
        for n in _SMALL:
            small[n][l] = g[n]
    grad_x = dxn[None]
    g_small = _pack([jnp.stack(small[n]) for n in _SMALL])
    got_small, = _chip_exchange([g_small], (False,), "gather_small_grads")

    views = [(4 * D, IN_WIDTH // 4), (4 * 3 * 512, D // 4), (4 * 3, D // 4), (D, D), g_small.shape]
    stacks = [jnp.stack([received[l][i] for l in range(DEPTH)], axis=1) for i in range(4)] + [got_small]
    sums = [_sum_chips(r.reshape((4,) + v), f"sum_chips_{i}") for i, (r, v) in enumerate(zip(stacks, views))]
    others = _sibling_swap(sums, "swap_core_sums")
    tensors = [lambda d: d['w_in'], abc, lambda d: d['b_merge'], lambda d: d['w_out'],
               lambda d: _pack([d[n] for n in _SMALL])]
    res = [_adamw(t(weights).reshape(v), t(mom_m).reshape(v), t(mom_v).reshape(v), s_, o_, f"adamw_{i}")
           for i, (t, v, s_, o_) in enumerate(zip(tensors, views, sums, others))]

    outs = []
    for kind in range(4):
        r = [res[i][kind] for i in range(5)]
        r_abc = r[1].reshape(4, 3, 512, D // 4)
        tiny = dict(zip(_SMALL, _unpack(r[4], [weights[n].shape for n in _SMALL])))
        big = dict(w_in=r[0].reshape(w_in.shape), w_branch_a=r_abc[:, 0], w_branch_b=r_abc[:, 1],
                   w_branch_c=r_abc[:, 2], b_merge=r[2].reshape(b_merge.shape), w_out=r[3].reshape(w_out.shape))
        outs.extend(big[n] if n in big else tiny[n] for n in _NAMES)
    return (loss, grad_x, *outs)
```
